```python
import jax, jax.numpy as jnp
from jax import lax
import numpy as np

D_MODEL = 1024
BATCH = 16
SEQ = 2048
DEPTH = 1

CHUNK = 64
SC_W = D_MODEL
SC_K = 3
RG_W = 1280
RG_HEADS = 16
RG_HEAD_DIM = RG_W // RG_HEADS
RG_K = 4
LRU_C = 8.0
EPS = 1e-6
P_IN = 4 * SC_W + 2 * RG_W + 2 * D_MODEL

kernel_name = "hybrid_shortconv_rglru_gated_merge_block"


def _rmsnorm(x, g):
    xf = x.astype(jnp.float32)
    y = xf * lax.rsqrt(jnp.mean(xf * xf, axis=-1, keepdims=True) + EPS)
    return (y * g.astype(jnp.float32)).astype(x.dtype)


def _causal_dwconv(u, w, b):
    k_taps = w.shape[0]
    s = u.shape[1]
    up = jnp.pad(u, ((0, 0), (k_taps - 1, 0), (0, 0)))
    y = b + up[:, 0:s] * w[0]
    for k in range(1, k_taps):
        y = y + up[:, k:k + s] * w[k]
    return y


def _lin_combine(e1, e2):
    a1, b1 = e1
    a2, b2 = e2
    return a1 * a2, a2 * b1 + b2


def _rg_lru(v, w_a, b_a, w_x, b_x, lam):
    bn, s, w = v.shape
    vh = v.reshape(bn, s, RG_HEADS, RG_HEAD_DIM)
    r = jax.nn.sigmoid(jnp.einsum('bshd,hde->bshe', vh, w_a).reshape(bn, s, w) + b_a)
    i = jax.nn.sigmoid(jnp.einsum('bshd,hde->bshe', vh, w_x).reshape(bn, s, w) + b_x)
    log_a = -LRU_C * r.astype(jnp.float32) * jax.nn.softplus(-lam.astype(jnp.float32))
    a = jnp.exp(log_a)
    bterm = jnp.sqrt(-jnp.expm1(2.0 * log_a)) * (i * v).astype(jnp.float32)
    nc = s // CHUNK
    a = a.reshape(bn, nc, CHUNK, w)
    bterm = bterm.reshape(bn, nc, CHUNK, w)
    cum_a, h_loc = lax.associative_scan(_lin_combine, (a, bterm), axis=2)

    def step(h, inp):
        ca, hl = inp
        hc = hl + ca * h[:, None, :]
        return hc[:, -1], hc

    h0 = jnp.zeros((bn, w), jnp.float32)
    _, hs = lax.scan(step, h0, (cum_a.transpose(1, 0, 2, 3), h_loc.transpose(1, 0, 2, 3)))
    return hs.transpose(1, 0, 2, 3).reshape(bn, s, w).astype(v.dtype)


def setup_inputs(seed: int = 0) -> dict:
    key = jax.random.key(seed)
    ks = jax.random.split(key, 24)
    f32 = jnp.float32
    L, D = DEPTH, D_MODEL
    nrm = lambda k, shp, sc: jax.random.normal(k, shp, f32) * sc
    u = jax.random.uniform(ks[15], (L, RG_W), f32, 0.9, 0.999)
    s_l = u ** (1.0 / LRU_C)
    rg_lambda = jnp.log(s_l) - jnp.log1p(-s_l)
    return {
        "x": nrm(ks[0], (BATCH, SEQ, D), 1.0),
        "c": nrm(ks[1], (BATCH, D), 1.0),
        "w_ada": nrm(ks[2], (L, D, 3 * D), 0.5 * D ** -0.5),
        "b_ada": nrm(ks[3], (L, 3 * D), 0.02),
        "g_norm": 1.0 + nrm(ks[4], (L, D), 0.02),
        "w_in": nrm(ks[5], (L, D, P_IN), D ** -0.5),
        "sc_conv_w": nrm(ks[6], (L, SC_K, SC_W), SC_K ** -0.5),
        "sc_conv_b": nrm(ks[7], (L, SC_W), 0.02),
        "sc_w_out": nrm(ks[8], (L, SC_W, D), SC_W ** -0.5),
        "rg_conv_w": nrm(ks[9], (L, RG_K, RG_W), RG_K ** -0.5),
        "rg_conv_b": nrm(ks[10], (L, RG_W), 0.02),
        "rg_w_a": nrm(ks[11], (L, RG_HEADS, RG_HEAD_DIM, RG_HEAD_DIM), RG_HEAD_DIM ** -0.5),
        "rg_b_a": nrm(ks[12], (L, RG_W), 0.02),
        "rg_w_x": nrm(ks[13], (L, RG_HEADS, RG_HEAD_DIM, RG_HEAD_DIM), RG_HEAD_DIM ** -0.5),
        "rg_b_x": nrm(ks[14], (L, RG_W), 0.02),
        "rg_lambda": rg_lambda,
        "rg_w_out": nrm(ks[16], (L, RG_W, D), RG_W ** -0.5),
        "b_merge": nrm(ks[17], (L, 2, D), 0.02),
        "w_out": nrm(ks[18], (L, D, D), D ** -0.5),
        "g_final": 1.0 + nrm(ks[19], (D,), 0.02),
    }


def reference(x, c, w_ada, b_ada, g_norm, w_in, sc_conv_w, sc_conv_b, sc_w_out,
              rg_conv_w, rg_conv_b, rg_w_a, rg_b_a, rg_w_x, rg_b_x, rg_lambda,
              rg_w_out, b_merge, w_out, g_final):
    sizes = [SC_W] * 4 + [RG_W] * 2 + [D_MODEL] * 2
    split_idx = [int(v) for v in np.cumsum(sizes)[:-1]]
    c_act = jax.nn.silu(c)
    for l in range(DEPTH):
        mod = c_act @ w_ada[l] + b_ada[l]
        shift, scale, gate = jnp.split(mod, 3, axis=-1)
        h = _rmsnorm(x, g_norm[l]) * (1.0 + scale[:, None, :]) + shift[:, None, :]
        z = h @ w_in[l]
        sc_b, sc_c, sc_v, sc_g, rg_v, rg_g, m_a, m_b = jnp.split(z, split_idx, axis=-1)
        u = _causal_dwconv(sc_c * sc_v, sc_conv_w[l], sc_conv_b[l])
        y_a = (sc_b * u * jax.nn.silu(sc_g)) @ sc_w_out[l]
        v = _causal_dwconv(rg_v, rg_conv_w[l], rg_conv_b[l])
        y_rg = _rg_lru(v, rg_w_a[l], rg_b_a[l], rg_w_x[l], rg_b_x[l], rg_lambda[l])
        y_b = (y_rg * jax.nn.silu(rg_g)) @ rg_w_out[l]
        g_a = jax.nn.sigmoid(m_a + b_merge[l, 0])
        g_b = jax.nn.sigmoid(m_b + b_merge[l, 1])
        merged = g_a * y_a + g_b * y_b
        x = x + gate[:, None, :] * (merged @ w_out[l])
    return _rmsnorm(x, g_final)
```

```python
import functools

import jax
import jax.numpy as jnp
from jax import lax
from jax.experimental import pallas as pl
from jax.experimental.pallas import tpu as pltpu

D_MODEL = 1024
BATCH = 16
SEQ = 2048
SC_K = 3
RG_W = 1280
RG_HEADS = 16
RG_HEAD_DIM = RG_W // RG_HEADS
RG_K = 4
LRU_C = 8.0
EPS = 1e-6

LANES = 128
MXU_DIM = 256
TS = 32
SUB = MXU_DIM // BATCH
NSUB = TS // SUB
ROWS = TS * BATCH
CB = MXU_DIM
NA = D_MODEL // CB
NB = RG_W // CB
GATE_WIN = 2 * CB
GATE_WIN_SLABS = GATE_WIN // LANES
SC_HIST = (SC_K - 1) * BATCH
RG_HIST = (RG_K - 1) * BATCH
VMEM_LIMIT_BYTES = 58 * 1024 * 1024

F32 = jnp.float32
BF16 = jnp.bfloat16


def _gate_win_start_slab(k):
    return jnp.clip(2 * k - 1, 0, RG_W // LANES - GATE_WIN_SLABS)


def _silu(v):
    return v * jax.nn.sigmoid(v)


def _adaln_kernel(c_ref, w_ref, b_ref, o_ref):
    c = c_ref[...]
    o_ref[...] = jnp.dot(_silu(c), w_ref[...], preferred_element_type=F32,
                         precision=lax.Precision.HIGHEST) + b_ref[...]


def _adaln(c, w_ada, b_ada):
    n = w_ada.shape[1]
    nblk = n // D_MODEL
    return pl.pallas_call(
        _adaln_kernel,
        grid=(nblk,),
        in_specs=[
            pl.BlockSpec((BATCH, D_MODEL), lambda i: (0, 0)),
            pl.BlockSpec((D_MODEL, D_MODEL), lambda i: (0, i)),
            pl.BlockSpec((1, D_MODEL), lambda i: (0, i)),
        ],
        out_specs=pl.BlockSpec((BATCH, D_MODEL), lambda i: (0, i)),
        out_shape=jax.ShapeDtypeStruct((BATCH, n), F32),
        name="adaln",
    )(c, w_ada, b_ada.reshape(1, n))


def _block_kernel(x_ref, shift_ref, scale_ref, gate_ref, gnorm_ref, gfinal_ref, perm_ref,
                  wsc_ref, sccw_ref, sccb_ref, scwout_ref,
                  wrgv_ref, rgcw_ref, rgcb_ref, wg_ref, gba_ref, gbx_ref, lam_ref,
                  wrgg_ref, rgwout_ref, wm_ref, bm_ref, wout_ref,
                  out_ref,
                  h_tb, ubuf, s_scr, vbuf, vf, vbf, a_scr, b_scr, yrg_scr, s2_scr,
                  hstate, mbt_scr):
    step = pl.program_id(0)

    @pl.when(step == 0)
    def _():
        ubuf[:, 0:SC_HIST, :] = jnp.zeros((NA, SC_HIST, CB), F32)
        vbuf[0:RG_HIST, :] = jnp.zeros((RG_HIST, RG_W), F32)
        hstate[...] = jnp.zeros_like(hstate)

    perm = perm_ref[...]

    gnorm = gnorm_ref[...].reshape(1, 1, D_MODEL)
    scale1 = 1.0 + scale_ref[...]
    for j in range(NSUB):
        x3 = x_ref[:, j * SUB:(j + 1) * SUB, :]
        ms = jnp.mean(x3 * x3, axis=-1, keepdims=True)
        hn = (x3 * lax.rsqrt(ms + EPS) * gnorm) * scale1 + shift_ref[...]
        hb = hn.reshape(SUB * BATCH, D_MODEL).astype(BF16)
        h_tb[j * MXU_DIM:(j + 1) * MXU_DIM, :] = jnp.dot(
            perm, hb, preferred_element_type=F32).astype(BF16)

    def body_a(j, carry):
        z = jnp.dot(h_tb[...], wsc_ref[j], preferred_element_type=F32)
        zb = z[:, 0 * CB:1 * CB]
        zc = z[:, 1 * CB:2 * CB]
        zv = z[:, 2 * CB:3 * CB]
        zg = z[:, 3 * CB:4 * CB]
        u = zc * zv
        ubuf[j, SC_HIST:SC_HIST + ROWS, :] = u
        cw = sccw_ref[j]
        conv = sccb_ref[j] + ubuf[j, 0:ROWS, :] * cw[0:1, :]
        conv = conv + ubuf[j, BATCH:BATCH + ROWS, :] * cw[1:2, :]
        conv = conv + u * cw[2:3, :]
        ubuf[j, 0:SC_HIST, :] = u[ROWS - SC_HIST:ROWS, :]
        s_scr[j] = (zb * conv * _silu(zg)).astype(BF16)
        return carry

    lax.fori_loop(0, NA, body_a, 0)
    s_all = jnp.concatenate([s_scr[j] for j in range(NA)], axis=1)
    ya = jnp.dot(s_all, scwout_ref[...], preferred_element_type=F32)

    vz = jnp.dot(h_tb[...], wrgv_ref[...], preferred_element_type=F32)
    vbuf[RG_HIST:RG_HIST + ROWS, :] = vz
    rcw = rgcw_ref[...]
    v = rgcb_ref[...] + vbuf[0:ROWS, :] * rcw[0:1, :]
    v = v + vbuf[BATCH:BATCH + ROWS, :] * rcw[1:2, :]
    v = v + vbuf[2 * BATCH:2 * BATCH + ROWS, :] * rcw[2:3, :]
    v = v + vz * rcw[3:4, :]
    vbuf[0:RG_HIST, :] = vz[ROWS - RG_HIST:ROWS, :]
    for cidx in range(RG_W // LANES):
        vbf[cidx] = v[:, cidx * LANES:(cidx + 1) * LANES].astype(BF16)
    for k in range(NB):
        vf[k] = v[:, k * CB:(k + 1) * CB]

    def body_b(k, carry):
        win = vbf[pl.ds(_gate_win_start_slab(k), GATE_WIN_SLABS)]
        lhs = jnp.concatenate([win[i] for i in range(GATE_WIN_SLABS)], axis=1)
        g = jnp.dot(lhs, wg_ref[k], preferred_element_type=F32)
        r = jax.nn.sigmoid(g[:, 0:CB] + gba_ref[k])
        ig = jax.nn.sigmoid(g[:, CB:2 * CB] + gbx_ref[k])
        log_a = -LRU_C * r * jax.nn.softplus(-lam_ref[k])
        a = jnp.exp(log_a)
        a_scr[...] = a
        one_minus_a2 = -jnp.tanh(log_a) * (a * a + 1.0)
        b_scr[...] = jnp.sqrt(one_minus_a2) * (ig * vf[k])
        h = hstate[k]
        for t in range(TS):
            rows = slice(t * BATCH, (t + 1) * BATCH)
            h = a_scr[rows, :] * h + b_scr[rows, :]
            yrg_scr[rows, :] = h
        hstate[k] = h
        zg = jnp.dot(h_tb[...], wrgg_ref[k], preferred_element_type=F32)
        s2_scr[k] = (yrg_scr[...] * _silu(zg)).astype(BF16)
        return carry

    lax.fori_loop(0, NB, body_b, 0)
    s2_all = jnp.concatenate([s2_scr[k] for k in range(NB)], axis=1)
    yb = jnp.dot(s2_all, rgwout_ref[...], preferred_element_type=F32)

    m = jnp.dot(h_tb[...], wm_ref[...], preferred_element_type=F32)
    g_a = jax.nn.sigmoid(m[:, 0:D_MODEL] + bm_ref[0:1, :])
    g_b = jax.nn.sigmoid(m[:, D_MODEL:2 * D_MODEL] + bm_ref[1:2, :])
    merged = (g_a * ya + g_b * yb).astype(BF16)
    for j in range(NSUB):
        mbt_scr[j * MXU_DIM:(j + 1) * MXU_DIM, :] = jnp.dot(
            perm, merged[j * MXU_DIM:(j + 1) * MXU_DIM, :],
            preferred_element_type=F32).astype(BF16)
    y = jnp.dot(mbt_scr[...], wout_ref[...], preferred_element_type=F32)
    gfinal = gfinal_ref[...].reshape(1, 1, D_MODEL)
    for j in range(NSUB):
        y3 = y[j * MXU_DIM:(j + 1) * MXU_DIM, :].reshape(BATCH, SUB, D_MODEL)
        xo = x_ref[:, j * SUB:(j + 1) * SUB, :] + gate_ref[...] * y3
        ms = jnp.mean(xo * xo, axis=-1, keepdims=True)
        out_ref[:, j * SUB:(j + 1) * SUB, :] = xo * lax.rsqrt(ms + EPS) * gfinal


def _const_spec(shape):
    nd = len(shape)
    return pl.BlockSpec(shape, lambda i, _nd=nd: (0,) * _nd, pipeline_mode=pl.Buffered(1))


def _row_perm():
    r = jnp.arange(SUB * BATCH)
    src = (r % BATCH) * SUB + r // BATCH
    return (src[:, None] == r[None, :]).astype(BF16)


def _block_diag(w):
    h, d, _ = w.shape
    eye = jnp.eye(h, dtype=w.dtype)
    return (eye[:, None, :, None] * w[:, :, None, :]).reshape(h * d, h * d)


def _gate_weights(w_a, w_x):
    da = _block_diag(w_a)
    dx = _block_diag(w_x)
    blocks = []
    for k in range(NB):
        s = min(max(2 * k - 1, 0), RG_W // LANES - GATE_WIN_SLABS) * LANES
        cols = slice(k * CB, (k + 1) * CB)
        blocks.append(jnp.concatenate([da[s:s + GATE_WIN, cols], dx[s:s + GATE_WIN, cols]], axis=1))
    return jnp.stack(blocks).astype(BF16)


def kernel(x, c, w_ada, b_ada, g_norm, w_in, sc_conv_w, sc_conv_b, sc_w_out, rg_conv_w, rg_conv_b,
           rg_w_a, rg_b_a, rg_w_x, rg_b_x, rg_lambda, rg_w_out, b_merge, w_out, g_final):
    assert x.shape == (BATCH, SEQ, D_MODEL) and w_ada.shape[0] == 1
    l = 0
    mod = _adaln(c, w_ada[l], b_ada[l])
    shift = mod[:, 0:D_MODEL].reshape(BATCH, 1, D_MODEL)
    scale = mod[:, D_MODEL:2 * D_MODEL].reshape(BATCH, 1, D_MODEL)
    gate = mod[:, 2 * D_MODEL:3 * D_MODEL].reshape(BATCH, 1, D_MODEL)

    w = w_in[l].astype(BF16)
    o_scb, o_scc, o_scv, o_scg = 0, D_MODEL, 2 * D_MODEL, 3 * D_MODEL
    o_rgv = 4 * D_MODEL
    o_rgg = o_rgv + RG_W
    o_m = o_rgg + RG_W
    wsc = jnp.stack([
        jnp.concatenate([w[:, o + j * CB:o + (j + 1) * CB] for o in (o_scb, o_scc, o_scv, o_scg)], axis=1)
        for j in range(NA)])
    wrgv = w[:, o_rgv:o_rgv + RG_W]
    wrgg = jnp.stack([w[:, o_rgg + k * CB:o_rgg + (k + 1) * CB] for k in range(NB)])
    wm = w[:, o_m:o_m + 2 * D_MODEL]

    sccw = sc_conv_w[l].reshape(SC_K, NA, CB).transpose(1, 0, 2)
    sccb = sc_conv_b[l].reshape(NA, 1, CB)
    wg = _gate_weights(rg_w_a[l], rg_w_x[l])
    gba = rg_b_a[l].reshape(NB, 1, CB)
    gbx = rg_b_x[l].reshape(NB, 1, CB)
    lam = rg_lambda[l].reshape(NB, 1, CB)

    operands = (
        x, shift, scale, gate, g_norm[l].reshape(1, D_MODEL), g_final.reshape(1, D_MODEL), _row_perm(),
        wsc, sccw, sccb, sc_w_out[l].astype(BF16),
        wrgv, rg_conv_w[l], rg_conv_b[l].reshape(1, RG_W), wg, gba, gbx, lam,
        wrgg, rg_w_out[l].astype(BF16), wm, b_merge[l], w_out[l].astype(BF16),
    )
    x_spec = pl.BlockSpec((BATCH, TS, D_MODEL), lambda i: (0, i, 0))
    in_specs = [x_spec] + [_const_spec(op.shape) for op in operands[1:]]
    scratch = [
        pltpu.VMEM((ROWS, D_MODEL), BF16),
        pltpu.VMEM((NA, SC_HIST + ROWS, CB), F32),
        pltpu.VMEM((NA, ROWS, CB), BF16),
        pltpu.VMEM((RG_HIST + ROWS, RG_W), F32),
        pltpu.VMEM((NB, ROWS, CB), F32),
        pltpu.VMEM((RG_W // LANES, ROWS, LANES), BF16),
        pltpu.VMEM((ROWS, CB), F32),
        pltpu.VMEM((ROWS, CB), F32),
        pltpu.VMEM((ROWS, CB), F32),
        pltpu.VMEM((NB, ROWS, CB), BF16),
        pltpu.VMEM((NB, BATCH, CB), F32),
        pltpu.VMEM((ROWS, D_MODEL), BF16),
    ]
    return pl.pallas_call(
        _block_kernel,
        grid=(SEQ // TS,),
        in_specs=in_specs,
        out_specs=pl.BlockSpec((BATCH, TS, D_MODEL), lambda i: (0, i, 0)),
        out_shape=jax.ShapeDtypeStruct((BATCH, SEQ, D_MODEL), F32),
        scratch_shapes=scratch,
        compiler_params=pltpu.CompilerParams(
            dimension_semantics=("arbitrary",), vmem_limit_bytes=VMEM_LIMIT_BYTES),
        name="hybrid_block",
    )(*operands)
```

```python
import jax
import jax.numpy as jnp
from jax import lax
from jax.experimental import pallas as pl
from jax.experimental.pallas import tpu as pltpu

D_MODEL = 1024
BATCH = 16
SEQ = 2048
SC_K = 3
RG_W = 1280
RG_HEADS = 16
RG_HEAD_DIM = RG_W // RG_HEADS
RG_K = 4
LRU_C = 8.0
EPS = 1e-6

LANES = 128
MXU_DIM = 256
V7X_VMEM_BYTES = 64 * 1024 * 1024
TS = 32
SUB = MXU_DIM // BATCH
NSUB = TS // SUB
ROWS = TS * BATCH
CB = MXU_DIM
NA = D_MODEL // CB
NB = RG_W // CB
GATE_WIN = 2 * CB
GATE_WIN_SLABS = GATE_WIN // LANES
SC_HIST = (SC_K - 1) * BATCH
RG_HIST = (RG_K - 1) * BATCH
VMEM_LIMIT_BYTES = V7X_VMEM_BYTES - 6 * 1024 * 1024

F32 = jnp.float32
BF16 = jnp.bfloat16


def _gate_win_start_slab(k):
    return min(max(2 * k - 1, 0), RG_W // LANES - GATE_WIN_SLABS)


def _adaln_kernel(c_ref, w_ref, b_ref, o_ref):
    c = c_ref[...]
    o_ref[...] = jnp.dot(c * jax.nn.sigmoid(c), w_ref[...], preferred_element_type=F32,
                         precision=lax.Precision.HIGHEST) + b_ref[...]


def _adaln(c, w_ada, b_ada):
    n = w_ada.shape[1]
    nblk = n // D_MODEL
    return pl.pallas_call(
        _adaln_kernel,
        grid=(nblk,),
        in_specs=[
            pl.BlockSpec((BATCH, D_MODEL), lambda i: (0, 0)),
            pl.BlockSpec((D_MODEL, D_MODEL), lambda i: (0, i)),
            pl.BlockSpec((1, D_MODEL), lambda i: (0, i)),
        ],
        out_specs=pl.BlockSpec((BATCH, D_MODEL), lambda i: (0, i)),
        out_shape=jax.ShapeDtypeStruct((BATCH, n), F32),
        name="adaln",
    )(c, w_ada, b_ada.reshape(1, n))


def _dot(a, b):
    return jnp.dot(a, b, preferred_element_type=F32)


def _half_silu_from_half(hz):
    return hz + hz * jnp.tanh(hz)


def _block_kernel(x_ref, shift_ref, scale_ref, gate_ref, gnorm_ref, gfinal_ref, perm_ref,
                  wsc_ref, sccw_ref, sccb_ref, scwout_ref,
                  wrgv_ref, rgcw_ref, rgcb_ref, wg_ref, gba_ref, gbx_ref, lam_ref,
                  wrgg_ref, rgwout_ref, wm_ref, bm_ref, wout_ref,
                  out_ref,
                  h_tb, ubuf, s_scr, vbuf, hvf, vbf, a_scr, b_scr, yrg_scr, s2_scr,
                  hstate, mbt_scr):
    step = pl.program_id(0)

    @pl.when(step == 0)
    def _():
        ubuf[:, 0:SC_HIST, :] = jnp.zeros((NA, SC_HIST, CB), F32)
        vbuf[0:RG_HIST, :] = jnp.zeros((RG_HIST, RG_W), F32)
        hstate[...] = jnp.zeros_like(hstate)

    perm = perm_ref[...]

    gnorm = gnorm_ref[...].reshape(1, 1, D_MODEL)
    scale1 = 1.0 + scale_ref[...]
    for j in range(NSUB):
        x3 = x_ref[:, j * SUB:(j + 1) * SUB, :]
        ms = jnp.mean(x3 * x3, axis=-1, keepdims=True)
        hn = (x3 * lax.rsqrt(ms + EPS) * gnorm) * scale1 + shift_ref[...]
        hb = hn.reshape(SUB * BATCH, D_MODEL).astype(BF16)
        h_tb[j * MXU_DIM:(j + 1) * MXU_DIM, :] = _dot(perm, hb).astype(BF16)

    def mixer_a_block(j, z):
        zb = z[:, 0 * CB:1 * CB]
        zc = z[:, 1 * CB:2 * CB]
        zv = z[:, 2 * CB:3 * CB]
        hzg = z[:, 3 * CB:4 * CB]
        u = zc * zv
        ubuf[j, SC_HIST:SC_HIST + ROWS, :] = u
        cw = sccw_ref[j]
        conv = sccb_ref[j] + ubuf[j, 0:ROWS, :] * cw[0:1, :]
        conv = conv + ubuf[j, BATCH:BATCH + ROWS, :] * cw[1:2, :]
        conv = conv + u * cw[2:3, :]
        ubuf[j, 0:SC_HIST, :] = u[ROWS - SC_HIST:ROWS, :]
        s_scr[j] = (zb * conv * _half_silu_from_half(hzg)).astype(BF16)

    z = _dot(h_tb[...], wsc_ref[0])
    for j in range(NA):
        z_next = _dot(h_tb[...], wsc_ref[j + 1]) if j + 1 < NA else None
        mixer_a_block(j, z)
        z = z_next

    vz = _dot(h_tb[...], wrgv_ref[...])
    s_all = jnp.concatenate([s_scr[j] for j in range(NA)], axis=1)
    ya_half = _dot(s_all, scwout_ref[...])
    vbuf[RG_HIST:RG_HIST + ROWS, :] = vz
    rcw = rgcw_ref[...]
    hv = rgcb_ref[...] + vbuf[0:ROWS, :] * rcw[0:1, :]
    hv = hv + vbuf[BATCH:BATCH + ROWS, :] * rcw[1:2, :]
    hv = hv + vbuf[2 * BATCH:2 * BATCH + ROWS, :] * rcw[2:3, :]
    hv = hv + vz * rcw[3:4, :]
    vbuf[0:RG_HIST, :] = vz[ROWS - RG_HIST:ROWS, :]
    for cidx in range(RG_W // LANES):
        vbf[cidx] = hv[:, cidx * LANES:(cidx + 1) * LANES].astype(BF16)
    for k in range(NB):
        hvf[k] = hv[:, k * CB:(k + 1) * CB]

    def gate_dots(k):
        s0 = _gate_win_start_slab(k)
        lhs = jnp.concatenate([vbf[s0 + i] for i in range(GATE_WIN_SLABS)], axis=1)
        return _dot(lhs, wg_ref[k]), _dot(h_tb[...], wrgg_ref[k])

    def mixer_b_block(k, hg, hzg):
        t_r = jnp.tanh(hg[:, 0:CB] + gba_ref[k])
        t_i = jnp.tanh(hg[:, CB:2 * CB] + gbx_ref[k])
        half_k = (-0.5 * LRU_C) * jax.nn.softplus(-lam_ref[k])
        log_a = half_k + half_k * t_r
        a = jnp.exp(log_a)
        a_scr[...] = a
        one_minus_a2 = jnp.tanh(log_a) * (-1.0 - a * a)
        root = jnp.where(one_minus_a2 > 0.0, one_minus_a2 * lax.rsqrt(one_minus_a2), 0.0)
        hvk = hvf[k]
        b_scr[...] = root * (hvk + hvk * t_i)
        h = hstate[k]
        for t in range(TS):
            rows = slice(t * BATCH, (t + 1) * BATCH)
            h = a_scr[rows, :] * h + b_scr[rows, :]
            yrg_scr[rows, :] = h
        hstate[k] = h
        s2_scr[k] = (yrg_scr[...] * _half_silu_from_half(hzg)).astype(BF16)

    hg, hzg = gate_dots(0)
    m = None
    for k in range(NB):
        if k + 1 < NB:
            hg_next, hzg_next = gate_dots(k + 1)
        else:
            m = _dot(h_tb[...], wm_ref[...])
            hg_next = hzg_next = None
        mixer_b_block(k, hg, hzg)
        hg, hzg = hg_next, hzg_next

    s2_all = jnp.concatenate([s2_scr[k] for k in range(NB)], axis=1)
    yb_half = _dot(s2_all, rgwout_ref[...])

    t_a = jnp.tanh(m[:, 0:D_MODEL] + bm_ref[0:1, :])
    t_b = jnp.tanh(m[:, D_MODEL:2 * D_MODEL] + bm_ref[1:2, :])
    merged = ((ya_half + ya_half * t_a) + (yb_half + yb_half * t_b)).astype(BF16)
    for j in range(NSUB):
        mbt_scr[j * MXU_DIM:(j + 1) * MXU_DIM, :] = _dot(
            perm, merged[j * MXU_DIM:(j + 1) * MXU_DIM, :]).astype(BF16)
    y = _dot(mbt_scr[...], wout_ref[...])
    gfinal = gfinal_ref[...].reshape(1, 1, D_MODEL)
    for j in range(NSUB):
        y3 = y[j * MXU_DIM:(j + 1) * MXU_DIM, :].reshape(BATCH, SUB, D_MODEL)
        xo = x_ref[:, j * SUB:(j + 1) * SUB, :] + gate_ref[...] * y3
        ms = jnp.mean(xo * xo, axis=-1, keepdims=True)
        out_ref[:, j * SUB:(j + 1) * SUB, :] = xo * lax.rsqrt(ms + EPS) * gfinal


def _const_spec(shape):
    nd = len(shape)
    return pl.BlockSpec(shape, lambda i, _nd=nd: (0,) * _nd, pipeline_mode=pl.Buffered(1))


def _row_perm():
    r = jnp.arange(SUB * BATCH)
    src = (r % BATCH) * SUB + r // BATCH
    return (src[:, None] == r[None, :]).astype(BF16)


def _block_diag(w):
    h, d, _ = w.shape
    eye = jnp.eye(h, dtype=w.dtype)
    return (eye[:, None, :, None] * w[:, :, None, :]).reshape(h * d, h * d)


def _gate_weights(w_a, w_x):
    da = _block_diag(w_a)
    dx = _block_diag(w_x)
    blocks = []
    for k in range(NB):
        s = _gate_win_start_slab(k) * LANES
        cols = slice(k * CB, (k + 1) * CB)
        blocks.append(jnp.concatenate([da[s:s + GATE_WIN, cols], dx[s:s + GATE_WIN, cols]], axis=1))
    return jnp.stack(blocks).astype(BF16)


def kernel(x, c, w_ada, b_ada, g_norm, w_in, sc_conv_w, sc_conv_b, sc_w_out, rg_conv_w, rg_conv_b,
           rg_w_a, rg_b_a, rg_w_x, rg_b_x, rg_lambda, rg_w_out, b_merge, w_out, g_final):
    assert x.shape == (BATCH, SEQ, D_MODEL) and w_ada.shape[0] == 1
    l = 0
    mod = _adaln(c, w_ada[l], b_ada[l])
    shift = mod[:, 0:D_MODEL].reshape(BATCH, 1, D_MODEL)
    scale = mod[:, D_MODEL:2 * D_MODEL].reshape(BATCH, 1, D_MODEL)
    gate = mod[:, 2 * D_MODEL:3 * D_MODEL].reshape(BATCH, 1, D_MODEL)

    w = w_in[l]
    o_scb, o_scc, o_scv, o_scg = 0, D_MODEL, 2 * D_MODEL, 3 * D_MODEL
    o_rgv = 4 * D_MODEL
    o_rgg = o_rgv + RG_W
    o_m = o_rgg + RG_W
    wsc = jnp.stack([
        jnp.concatenate([w[:, o_scb + j * CB:o_scb + (j + 1) * CB],
                         w[:, o_scc + j * CB:o_scc + (j + 1) * CB],
                         w[:, o_scv + j * CB:o_scv + (j + 1) * CB],
                         0.5 * w[:, o_scg + j * CB:o_scg + (j + 1) * CB]], axis=1)
        for j in range(NA)]).astype(BF16)
    wrgv = w[:, o_rgv:o_rgv + RG_W].astype(BF16)
    wrgg = jnp.stack([0.5 * w[:, o_rgg + k * CB:o_rgg + (k + 1) * CB] for k in range(NB)]).astype(BF16)
    wm = (0.5 * w[:, o_m:o_m + 2 * D_MODEL]).astype(BF16)

    sccw = sc_conv_w[l].reshape(SC_K, NA, CB).transpose(1, 0, 2)
    sccb = sc_conv_b[l].reshape(NA, 1, CB)
    wg = _gate_weights(rg_w_a[l], rg_w_x[l])
    gba = (0.5 * rg_b_a[l]).reshape(NB, 1, CB)
    gbx = (0.5 * rg_b_x[l]).reshape(NB, 1, CB)
    lam = rg_lambda[l].reshape(NB, 1, CB)

    operands = (
        x, shift, scale, gate, g_norm[l].reshape(1, D_MODEL), g_final.reshape(1, D_MODEL), _row_perm(),
        wsc, sccw, sccb, (0.5 * sc_w_out[l]).astype(BF16),
        wrgv, 0.5 * rg_conv_w[l], (0.5 * rg_conv_b[l]).reshape(1, RG_W), wg, gba, gbx, lam,
        wrgg, (0.5 * rg_w_out[l]).astype(BF16), wm, 0.5 * b_merge[l], w_out[l].astype(BF16),
    )
    x_spec = pl.BlockSpec((BATCH, TS, D_MODEL), lambda i: (0, i, 0))
    in_specs = [x_spec] + [_const_spec(op.shape) for op in operands[1:]]
    scratch = [
        pltpu.VMEM((ROWS, D_MODEL), BF16),
        pltpu.VMEM((NA, SC_HIST + ROWS, CB), F32),
        pltpu.VMEM((NA, ROWS, CB), BF16),
        pltpu.VMEM((RG_HIST + ROWS, RG_W), F32),
        pltpu.VMEM((NB, ROWS, CB), F32),
        pltpu.VMEM((RG_W // LANES, ROWS, LANES), BF16),
        pltpu.VMEM((ROWS, CB), F32),
        pltpu.VMEM((ROWS, CB), F32),
        pltpu.VMEM((ROWS, CB), F32),
        pltpu.VMEM((NB, ROWS, CB), BF16),
        pltpu.VMEM((NB, BATCH, CB), F32),
        pltpu.VMEM((ROWS, D_MODEL), BF16),
    ]
    return pl.pallas_call(
        _block_kernel,
        grid=(SEQ // TS,),
        in_specs=in_specs,
        out_specs=pl.BlockSpec((BATCH, TS, D_MODEL), lambda i: (0, i, 0)),
        out_shape=jax.ShapeDtypeStruct((BATCH, SEQ, D_MODEL), F32),
        scratch_shapes=scratch,
        compiler_params=pltpu.CompilerParams(
            dimension_semantics=("arbitrary",), vmem_limit_bytes=VMEM_LIMIT_BYTES),
        name="hybrid_block",
    )(*operands)
```

```python
import jax
import jax.numpy as jnp
from jax import lax
from jax.experimental import pallas as pl
from jax.experimental.pallas import tpu as pltpu

D_MODEL = 1024
BATCH = 16
SEQ = 2048
SC_K = 3
RG_W = 1280
RG_HEADS = 16
RG_HEAD_DIM = RG_W // RG_HEADS
RG_K = 4
LRU_C = 8.0
EPS = 1e-6

LANES = 128
MXU_DIM = 256
V7X_VMEM_BYTES = 64 * 1024 * 1024
TS = 32
SUB = MXU_DIM // BATCH
NSUB = TS // SUB
ROWS = TS * BATCH
CB = MXU_DIM
NA = D_MODEL // CB
NB = RG_W // CB
GATE_WIN = 2 * CB
GATE_WIN_SLABS = GATE_WIN // LANES
SC_HIST = (SC_K - 1) * BATCH
RG_HIST = (RG_K - 1) * BATCH
VMEM_LIMIT_BYTES = V7X_VMEM_BYTES - 6 * 1024 * 1024

O_SCB, O_SCC, O_SCV, O_SCG = 0, D_MODEL, 2 * D_MODEL, 3 * D_MODEL
O_RGV = 4 * D_MODEL
O_RGG = O_RGV + RG_W
O_M = O_RGG + RG_W
P_IN = O_M + 2 * D_MODEL

F32 = jnp.float32
BF16 = jnp.bfloat16


def _gate_win_start_slab(k):
    return min(max(2 * k - 1, 0), RG_W // LANES - GATE_WIN_SLABS)


def _adaln_kernel(c_ref, w_ref, b_ref, o_ref):
    c = c_ref[...]
    o_ref[...] = jnp.dot(c * jax.nn.sigmoid(c), w_ref[...], preferred_element_type=F32,
                         precision=lax.Precision.HIGHEST) + b_ref[...]


def _adaln(c, w_ada, b_ada):
    n = w_ada.shape[1]
    nblk = n // D_MODEL
    return pl.pallas_call(
        _adaln_kernel,
        grid=(nblk,),
        in_specs=[
            pl.BlockSpec((BATCH, D_MODEL), lambda i: (0, 0)),
            pl.BlockSpec((D_MODEL, D_MODEL), lambda i: (0, i)),
            pl.BlockSpec((1, D_MODEL), lambda i: (0, i)),
        ],
        out_specs=pl.BlockSpec((BATCH, D_MODEL), lambda i: (0, i)),
        out_shape=jax.ShapeDtypeStruct((BATCH, n), F32),
        name="adaln",
    )(c, w_ada, b_ada.reshape(1, n))


def _dot(a, b):
    return jnp.dot(a, b, preferred_element_type=F32)


def _half_silu_from_half(hz):
    return hz + hz * jnp.tanh(hz)


def _block_kernel(x_ref, shift_ref, scale_ref, gate_ref, gnorm_ref, gfinal_ref, perm_ref,
                  win_ref, sccw_ref, sccb_ref, scwout_ref,
                  rgcw_ref, rgcb_ref, wg_ref, gba_ref, gbx_ref, lam_ref,
                  rgwout_ref, bm_ref, wout_ref,
                  out_ref,
                  h_tb, ubuf, s_scr, vbuf, hvf, vbf, a_scr, b_scr, yrg_scr, s2_scr,
                  hstate, mbt_scr):
    step = pl.program_id(0)

    @pl.when(step == 0)
    def _():
        ubuf[:, 0:SC_HIST, :] = jnp.zeros((NA, SC_HIST, CB), F32)
        vbuf[0:RG_HIST, :] = jnp.zeros((RG_HIST, RG_W), F32)
        hstate[...] = jnp.zeros_like(hstate)

    perm = perm_ref[...]

    gnorm = gnorm_ref[...].reshape(1, 1, D_MODEL)
    scale1 = 1.0 + scale_ref[...]
    for j in range(NSUB):
        x3 = x_ref[:, j * SUB:(j + 1) * SUB, :]
        ms = jnp.mean(x3 * x3, axis=-1, keepdims=True)
        hn = (x3 * lax.rsqrt(ms + EPS) * gnorm) * scale1 + shift_ref[...]
        hb = hn.reshape(SUB * BATCH, D_MODEL).astype(BF16)
        h_tb[j * MXU_DIM:(j + 1) * MXU_DIM, :] = _dot(perm, hb).astype(BF16)

    def proj(col0, width):
        return _dot(h_tb[...], win_ref[:, col0:col0 + width])

    def mixer_a_dots(j):
        return tuple(proj(o + j * CB, CB) for o in (O_SCB, O_SCC, O_SCV, O_SCG))

    def mixer_a_block(j, z):
        zb, zc, zv, hzg = z
        u = zc * zv
        ubuf[j, SC_HIST:SC_HIST + ROWS, :] = u
        cw = sccw_ref[j]
        conv = sccb_ref[j] + ubuf[j, 0:ROWS, :] * cw[0:1, :]
        conv = conv + ubuf[j, BATCH:BATCH + ROWS, :] * cw[1:2, :]
        conv = conv + u * cw[2:3, :]
        ubuf[j, 0:SC_HIST, :] = u[ROWS - SC_HIST:ROWS, :]
        s_scr[j] = (zb * conv * _half_silu_from_half(hzg)).astype(BF16)

    z = mixer_a_dots(0)
    for j in range(NA):
        z_next = mixer_a_dots(j + 1) if j + 1 < NA else None
        mixer_a_block(j, z)
        z = z_next

    vz = proj(O_RGV, RG_W)
    s_all = jnp.concatenate([s_scr[j] for j in range(NA)], axis=1)
    ya_half = _dot(s_all, scwout_ref[...])
    vbuf[RG_HIST:RG_HIST + ROWS, :] = vz
    rcw = rgcw_ref[...]
    hv = rgcb_ref[...] + vbuf[0:ROWS, :] * rcw[0:1, :]
    hv = hv + vbuf[BATCH:BATCH + ROWS, :] * rcw[1:2, :]
    hv = hv + vbuf[2 * BATCH:2 * BATCH + ROWS, :] * rcw[2:3, :]
    hv = hv + vz * rcw[3:4, :]
    vbuf[0:RG_HIST, :] = vz[ROWS - RG_HIST:ROWS, :]
    for cidx in range(RG_W // LANES):
        vbf[cidx] = hv[:, cidx * LANES:(cidx + 1) * LANES].astype(BF16)
    for k in range(NB):
        hvf[k] = hv[:, k * CB:(k + 1) * CB]

    def gate_dots(k):
        s0 = _gate_win_start_slab(k)
        lhs = jnp.concatenate([vbf[s0 + i] for i in range(GATE_WIN_SLABS)], axis=1)
        return _dot(lhs, wg_ref[k]), proj(O_RGG + k * CB, CB)

    def mixer_b_block(k, hg, hzg):
        t_r = jnp.tanh(hg[:, 0:CB] + gba_ref[k])
        t_i = jnp.tanh(hg[:, CB:2 * CB] + gbx_ref[k])
        half_k = (-0.5 * LRU_C) * jax.nn.softplus(-lam_ref[k])
        log_a = half_k + half_k * t_r
        a = jnp.exp(log_a)
        a_scr[...] = a
        one_minus_a2 = jnp.tanh(log_a) * (-1.0 - a * a)
        root = jnp.where(one_minus_a2 > 0.0, one_minus_a2 * lax.rsqrt(one_minus_a2), 0.0)
        hvk = hvf[k]
        b_scr[...] = root * (hvk + hvk * t_i)
        h = hstate[k]
        for t in range(TS):
            rows = slice(t * BATCH, (t + 1) * BATCH)
            h = a_scr[rows, :] * h + b_scr[rows, :]
            yrg_scr[rows, :] = h
        hstate[k] = h
        s2_scr[k] = (yrg_scr[...] * _half_silu_from_half(hzg)).astype(BF16)

    hg, hzg = gate_dots(0)
    m = None
    for k in range(NB):
        if k + 1 < NB:
            hg_next, hzg_next = gate_dots(k + 1)
        else:
            m = proj(O_M, 2 * D_MODEL)
            hg_next = hzg_next = None
        mixer_b_block(k, hg, hzg)
        hg, hzg = hg_next, hzg_next

    s2_all = jnp.concatenate([s2_scr[k] for k in range(NB)], axis=1)
    yb_half = _dot(s2_all, rgwout_ref[...])

    t_a = jnp.tanh(m[:, 0:D_MODEL] + bm_ref[0:1, :])
    t_b = jnp.tanh(m[:, D_MODEL:2 * D_MODEL] + bm_ref[1:2, :])
    merged = ((ya_half + ya_half * t_a) + (yb_half + yb_half * t_b)).astype(BF16)
    for j in range(NSUB):
        mbt_scr[j * MXU_DIM:(j + 1) * MXU_DIM, :] = _dot(
            perm, merged[j * MXU_DIM:(j + 1) * MXU_DIM, :]).astype(BF16)
    y = _dot(mbt_scr[...], wout_ref[...])
    gfinal = gfinal_ref[...].reshape(1, 1, D_MODEL)
    for j in range(NSUB):
        y3 = y[j * MXU_DIM:(j + 1) * MXU_DIM, :].reshape(BATCH, SUB, D_MODEL)
        xo = x_ref[:, j * SUB:(j + 1) * SUB, :] + gate_ref[...] * y3
        ms = jnp.mean(xo * xo, axis=-1, keepdims=True)
        out_ref[:, j * SUB:(j + 1) * SUB, :] = xo * lax.rsqrt(ms + EPS) * gfinal


def _const_spec(shape):
    nd = len(shape)
    return pl.BlockSpec(shape, lambda i, _nd=nd: (0,) * _nd, pipeline_mode=pl.Buffered(1))


def _row_perm():
    r = jnp.arange(SUB * BATCH)
    src = (r % BATCH) * SUB + r // BATCH
    return (src[:, None] == r[None, :]).astype(BF16)


def _block_diag(w):
    h, d, _ = w.shape
    eye = jnp.eye(h, dtype=w.dtype)
    return (eye[:, None, :, None] * w[:, :, None, :]).reshape(h * d, h * d)


def _gate_weights(w_a, w_x):
    da = _block_diag(w_a)
    dx = _block_diag(w_x)
    blocks = []
    for k in range(NB):
        s = _gate_win_start_slab(k) * LANES
        cols = slice(k * CB, (k + 1) * CB)
        blocks.append(jnp.concatenate([da[s:s + GATE_WIN, cols], dx[s:s + GATE_WIN, cols]], axis=1))
    return jnp.stack(blocks).astype(BF16)


def kernel(x, c, w_ada, b_ada, g_norm, w_in, sc_conv_w, sc_conv_b, sc_w_out, rg_conv_w, rg_conv_b,
           rg_w_a, rg_b_a, rg_w_x, rg_b_x, rg_lambda, rg_w_out, b_merge, w_out, g_final):
    assert x.shape == (BATCH, SEQ, D_MODEL) and w_ada.shape[0] == 1
    l = 0
    mod = _adaln(c, w_ada[l], b_ada[l])
    shift = mod[:, 0:D_MODEL].reshape(BATCH, 1, D_MODEL)
    scale = mod[:, D_MODEL:2 * D_MODEL].reshape(BATCH, 1, D_MODEL)
    gate = mod[:, 2 * D_MODEL:3 * D_MODEL].reshape(BATCH, 1, D_MODEL)

    col = jnp.arange(P_IN)
    halved = ((col >= O_SCG) & (col < O_RGV)) | (col >= O_RGG)
    win = (w_in[l] * jnp.where(halved, 0.5, 1.0).astype(F32)[None, :]).astype(BF16)

    sccw = sc_conv_w[l].reshape(SC_K, NA, CB).transpose(1, 0, 2)
    sccb = sc_conv_b[l].reshape(NA, 1, CB)
    wg = _gate_weights(rg_w_a[l], rg_w_x[l])
    gba = (0.5 * rg_b_a[l]).reshape(NB, 1, CB)
    gbx = (0.5 * rg_b_x[l]).reshape(NB, 1, CB)
    lam = rg_lambda[l].reshape(NB, 1, CB)

    operands = (
        x, shift, scale, gate, g_norm[l].reshape(1, D_MODEL), g_final.reshape(1, D_MODEL), _row_perm(),
        win, sccw, sccb, (0.5 * sc_w_out[l]).astype(BF16),
        0.5 * rg_conv_w[l], (0.5 * rg_conv_b[l]).reshape(1, RG_W), wg, gba, gbx, lam,
        (0.5 * rg_w_out[l]).astype(BF16), 0.5 * b_merge[l], w_out[l].astype(BF16),
    )
    x_spec = pl.BlockSpec((BATCH, TS, D_MODEL), lambda i: (0, i, 0))
    in_specs = [x_spec] + [_const_spec(op.shape) for op in operands[1:]]
    scratch = [
        pltpu.VMEM((ROWS, D_MODEL), BF16),
        pltpu.VMEM((NA, SC_HIST + ROWS, CB), F32),
        pltpu.VMEM((NA, ROWS, CB), BF16),
        pltpu.VMEM((RG_HIST + ROWS, RG_W), F32),
        pltpu.VMEM((NB, ROWS, CB), F32),
        pltpu.VMEM((RG_W // LANES, ROWS, LANES), BF16),
        pltpu.VMEM((ROWS, CB), F32),
        pltpu.VMEM((ROWS, CB), F32),
        pltpu.VMEM((ROWS, CB), F32),
        pltpu.VMEM((NB, ROWS, CB), BF16),
        pltpu.VMEM((NB, BATCH, CB), F32),
        pltpu.VMEM((ROWS, D_MODEL), BF16),
    ]
    return pl.pallas_call(
        _block_kernel,
        grid=(SEQ // TS,),
        in_specs=in_specs,
        out_specs=pl.BlockSpec((BATCH, TS, D_MODEL), lambda i: (0, i, 0)),
        out_shape=jax.ShapeDtypeStruct((BATCH, SEQ, D_MODEL), F32),
        scratch_shapes=scratch,
        compiler_params=pltpu.CompilerParams(
            dimension_semantics=("arbitrary",), vmem_limit_bytes=VMEM_LIMIT_BYTES),
        name="hybrid_block",
    )(*operands)
```

```python
import jax
import jax.numpy as jnp
from jax import lax
from jax.experimental import pallas as pl
from jax.experimental.pallas import tpu as pltpu

D_MODEL = 1024
BATCH = 16
SEQ = 2048
SC_K = 3
RG_W = 1280
RG_HEADS = 16
RG_HEAD_DIM = RG_W // RG_HEADS
RG_K = 4
LRU_C = 8.0
EPS = 1e-6

LANES = 128
MXU_DIM = 256
V7X_VMEM_BYTES = 64 * 1024 * 1024
TS = 32
NT = SEQ // TS
ROWS = TS * BATCH
CB = MXU_DIM
NA = D_MODEL // CB
NB = RG_W // CB
GATE_WIN = 2 * CB
GATE_WIN_SLABS = GATE_WIN // LANES
SC_HIST = (SC_K - 1) * BATCH
RG_HIST = (RG_K - 1) * BATCH
VMEM_LIMIT_BYTES = V7X_VMEM_BYTES - 6 * 1024 * 1024

O_SCB, O_SCC, O_SCV, O_SCG = 0, D_MODEL, 2 * D_MODEL, 3 * D_MODEL
O_RGV = 4 * D_MODEL
O_RGG = O_RGV + RG_W
O_M = O_RGG + RG_W
P_IN = O_M + 2 * D_MODEL

F32 = jnp.float32
BF16 = jnp.bfloat16


def _gate_win_start_slab(k):
    return min(max(2 * k - 1, 0), RG_W // LANES - GATE_WIN_SLABS)


def _adaln_kernel(c_ref, w_ref, b_ref, o_ref):
    c = c_ref[...]
    o_ref[...] = jnp.dot(c * jax.nn.sigmoid(c), w_ref[...], preferred_element_type=F32,
                         precision=lax.Precision.HIGHEST) + b_ref[...]


def _adaln(c, w_ada, b_ada):
    n = w_ada.shape[1]
    nblk = n // D_MODEL
    return pl.pallas_call(
        _adaln_kernel,
        grid=(nblk,),
        in_specs=[
            pl.BlockSpec((BATCH, D_MODEL), lambda i: (0, 0)),
            pl.BlockSpec((D_MODEL, D_MODEL), lambda i: (0, i)),
            pl.BlockSpec((1, D_MODEL), lambda i: (0, i)),
        ],
        out_specs=pl.BlockSpec((BATCH, D_MODEL), lambda i: (0, i)),
        out_shape=jax.ShapeDtypeStruct((BATCH, n), F32),
        name="adaln",
    )(c, w_ada, b_ada.reshape(1, n))


def _dot(a, b):
    return jnp.dot(a, b, preferred_element_type=F32)


def _half_silu_from_half(hz):
    return hz + hz * jnp.tanh(hz)


def _tile_copies(hbm_ref, buf_ref, sem_ref, tile, slot, to_hbm):
    copies = []
    for b in range(BATCH):
        hbm = hbm_ref.at[b, pl.ds(tile * TS, TS), :]
        vmem = buf_ref.at[slot, :, b, :]
        src, dst = (vmem, hbm) if to_hbm else (hbm, vmem)
        copies.append(pltpu.make_async_copy(src, dst, sem_ref.at[slot]))
    return copies


def _block_kernel(x_hbm, mod_ref, gnorm_ref, gfinal_ref,
                  win_ref, sccw_ref, sccb_ref, scwout_ref,
                  rgcw_ref, rgcb_ref, wg_ref, gba_ref, gbx_ref, lam_ref,
                  rgwout_ref, bm_ref, wout_ref,
                  out_hbm,
                  xbuf, obuf, in_sem, out_sem,
                  h_tb, ubuf, s_scr, vbuf, hvf, vbf, a_scr, b_scr, yrg_scr, s2_scr, hstate):
    step = pl.program_id(0)
    slot = lax.rem(step, 2)

    @pl.when(step == 0)
    def _():
        ubuf[:, 0:SC_HIST, :] = jnp.zeros((NA, SC_HIST, CB), F32)
        vbuf[0:RG_HIST, :] = jnp.zeros((RG_HIST, RG_W), F32)
        hstate[...] = jnp.zeros_like(hstate)
        for cp in _tile_copies(x_hbm, xbuf, in_sem, 0, 0, to_hbm=False):
            cp.start()

    @pl.when(step + 1 < NT)
    def _():
        for cp in _tile_copies(x_hbm, xbuf, in_sem, step + 1, 1 - slot, to_hbm=False):
            cp.start()

    for cp in _tile_copies(x_hbm, xbuf, in_sem, step, slot, to_hbm=False):
        cp.wait()

    shift = mod_ref[:, 0:D_MODEL]
    scale1 = 1.0 + mod_ref[:, D_MODEL:2 * D_MODEL]
    gate = mod_ref[:, 2 * D_MODEL:3 * D_MODEL]

    x3 = xbuf[slot]
    ms = jnp.mean(x3 * x3, axis=-1, keepdims=True)
    hn = (x3 * lax.rsqrt(ms + EPS) * gnorm_ref[...]) * scale1 + shift
    h_tb[...] = hn.reshape(ROWS, D_MODEL).astype(BF16)

    def proj(col0, width):
        return _dot(h_tb[...], win_ref[:, col0:col0 + width])

    def mixer_a_dots(j):
        return tuple(proj(o + j * CB, CB) for o in (O_SCB, O_SCC, O_SCV, O_SCG))

    def mixer_a_block(j, z):
        zb, zc, zv, hzg = z
        u = zc * zv
        ubuf[j, SC_HIST:SC_HIST + ROWS, :] = u
        cw = sccw_ref[j]
        conv = sccb_ref[j] + ubuf[j, 0:ROWS, :] * cw[0:1, :]
        conv = conv + ubuf[j, BATCH:BATCH + ROWS, :] * cw[1:2, :]
        conv = conv + u * cw[2:3, :]
        ubuf[j, 0:SC_HIST, :] = u[ROWS - SC_HIST:ROWS, :]
        s_scr[j] = (zb * conv * _half_silu_from_half(hzg)).astype(BF16)

    z = mixer_a_dots(0)
    for j in range(NA):
        z_next = mixer_a_dots(j + 1) if j + 1 < NA else None
        mixer_a_block(j, z)
        z = z_next

    vz = proj(O_RGV, RG_W)
    s_all = jnp.concatenate([s_scr[j] for j in range(NA)], axis=1)
    ya_half = _dot(s_all, scwout_ref[...])
    vbuf[RG_HIST:RG_HIST + ROWS, :] = vz
    rcw = rgcw_ref[...]
    hv = rgcb_ref[...] + vbuf[0:ROWS, :] * rcw[0:1, :]
    hv = hv + vbuf[BATCH:BATCH + ROWS, :] * rcw[1:2, :]
    hv = hv + vbuf[2 * BATCH:2 * BATCH + ROWS, :] * rcw[2:3, :]
    hv = hv + vz * rcw[3:4, :]
    vbuf[0:RG_HIST, :] = vz[ROWS - RG_HIST:ROWS, :]
    for cidx in range(RG_W // LANES):
        vbf[cidx] = hv[:, cidx * LANES:(cidx + 1) * LANES].astype(BF16)
    for k in range(NB):
        hvf[k] = hv[:, k * CB:(k + 1) * CB]

    def gate_dots(k):
        s0 = _gate_win_start_slab(k)
        lhs = jnp.concatenate([vbf[s0 + i] for i in range(GATE_WIN_SLABS)], axis=1)
        return _dot(lhs, wg_ref[k]), proj(O_RGG + k * CB, CB)

    def mixer_b_block(k, hg, hzg):
        t_r = jnp.tanh(hg[:, 0:CB] + gba_ref[k])
        t_i = jnp.tanh(hg[:, CB:2 * CB] + gbx_ref[k])
        half_k = (-0.5 * LRU_C) * jax.nn.softplus(-lam_ref[k])
        log_a = half_k + half_k * t_r
        a = jnp.exp(log_a)
        a_scr[...] = a
        one_minus_a2 = jnp.tanh(log_a) * (-1.0 - a * a)
        root = jnp.where(one_minus_a2 > 0.0, one_minus_a2 * lax.rsqrt(one_minus_a2), 0.0)
        hvk = hvf[k]
        b_scr[...] = root * (hvk + hvk * t_i)
        h = hstate[k]
        for t in range(TS):
            rows = slice(t * BATCH, (t + 1) * BATCH)
            h = a_scr[rows, :] * h + b_scr[rows, :]
            yrg_scr[rows, :] = h
        hstate[k] = h
        s2_scr[k] = (yrg_scr[...] * _half_silu_from_half(hzg)).astype(BF16)

    hg, hzg = gate_dots(0)
    m = None
    for k in range(NB):
        if k + 1 < NB:
            hg_next, hzg_next = gate_dots(k + 1)
        else:
            m = proj(O_M, 2 * D_MODEL)
            hg_next = hzg_next = None
        mixer_b_block(k, hg, hzg)
        hg, hzg = hg_next, hzg_next

    s2_all = jnp.concatenate([s2_scr[k] for k in range(NB)], axis=1)
    yb_half = _dot(s2_all, rgwout_ref[...])

    t_a = jnp.tanh(m[:, 0:D_MODEL] + bm_ref[0:1, :])
    t_b = jnp.tanh(m[:, D_MODEL:2 * D_MODEL] + bm_ref[1:2, :])
    merged = ((ya_half + ya_half * t_a) + (yb_half + yb_half * t_b)).astype(BF16)
    y3 = _dot(merged, wout_ref[...]).reshape(TS, BATCH, D_MODEL)
    xo = xbuf[slot] + gate * y3
    ms = jnp.mean(xo * xo, axis=-1, keepdims=True)
    res = xo * lax.rsqrt(ms + EPS) * gfinal_ref[...]

    @pl.when(step >= 2)
    def _():
        for cp in _tile_copies(out_hbm, obuf, out_sem, step - 2, slot, to_hbm=True):
            cp.wait()

    obuf[slot] = res
    for cp in _tile_copies(out_hbm, obuf, out_sem, step, slot, to_hbm=True):
        cp.start()

    @pl.when(step == NT - 1)
    def _():
        for cp in _tile_copies(out_hbm, obuf, out_sem, step - 1, 1 - slot, to_hbm=True):
            cp.wait()
        for cp in _tile_copies(out_hbm, obuf, out_sem, step, slot, to_hbm=True):
            cp.wait()


def _const_spec(shape):
    nd = len(shape)
    return pl.BlockSpec(shape, lambda i, _nd=nd: (0,) * _nd, pipeline_mode=pl.Buffered(1))


def _block_diag(w):
    h, d, _ = w.shape
    eye = jnp.eye(h, dtype=w.dtype)
    return (eye[:, None, :, None] * w[:, :, None, :]).reshape(h * d, h * d)


def _gate_weights(w_a, w_x):
    da = _block_diag(w_a)
    dx = _block_diag(w_x)
    blocks = []
    for k in range(NB):
        s = _gate_win_start_slab(k) * LANES
        cols = slice(k * CB, (k + 1) * CB)
        blocks.append(jnp.concatenate([da[s:s + GATE_WIN, cols], dx[s:s + GATE_WIN, cols]], axis=1))
    return jnp.stack(blocks).astype(BF16)


def kernel(x, c, w_ada, b_ada, g_norm, w_in, sc_conv_w, sc_conv_b, sc_w_out, rg_conv_w, rg_conv_b,
           rg_w_a, rg_b_a, rg_w_x, rg_b_x, rg_lambda, rg_w_out, b_merge, w_out, g_final):
    assert x.shape == (BATCH, SEQ, D_MODEL) and w_ada.shape[0] == 1
    l = 0
    mod = _adaln(c, w_ada[l], b_ada[l])

    col = jnp.arange(P_IN)
    halved = ((col >= O_SCG) & (col < O_RGV)) | (col >= O_RGG)
    win = (w_in[l] * jnp.where(halved, 0.5, 1.0).astype(F32)[None, :]).astype(BF16)

    sccw = sc_conv_w[l].reshape(SC_K, NA, CB).transpose(1, 0, 2)
    sccb = sc_conv_b[l].reshape(NA, 1, CB)
    wg = _gate_weights(rg_w_a[l], rg_w_x[l])
    gba = (0.5 * rg_b_a[l]).reshape(NB, 1, CB)
    gbx = (0.5 * rg_b_x[l]).reshape(NB, 1, CB)
    lam = rg_lambda[l].reshape(NB, 1, CB)

    operands = (
        x, mod, g_norm[l].reshape(1, D_MODEL), g_final.reshape(1, D_MODEL),
        win, sccw, sccb, (0.5 * sc_w_out[l]).astype(BF16),
        0.5 * rg_conv_w[l], (0.5 * rg_conv_b[l]).reshape(1, RG_W), wg, gba, gbx, lam,
        (0.5 * rg_w_out[l]).astype(BF16), 0.5 * b_merge[l], w_out[l].astype(BF16),
    )
    any_spec = pl.BlockSpec(memory_space=pl.ANY)
    in_specs = [any_spec] + [_const_spec(op.shape) for op in operands[1:]]
    scratch = [
        pltpu.VMEM((2, TS, BATCH, D_MODEL), F32),
        pltpu.VMEM((2, TS, BATCH, D_MODEL), F32),
        pltpu.SemaphoreType.DMA((2,)),
        pltpu.SemaphoreType.DMA((2,)),
        pltpu.VMEM((ROWS, D_MODEL), BF16),
        pltpu.VMEM((NA, SC_HIST + ROWS, CB), F32),
        pltpu.VMEM((NA, ROWS, CB), BF16),
        pltpu.VMEM((RG_HIST + ROWS, RG_W), F32),
        pltpu.VMEM((NB, ROWS, CB), F32),
        pltpu.VMEM((RG_W // LANES, ROWS, LANES), BF16),
        pltpu.VMEM((ROWS, CB), F32),
        pltpu.VMEM((ROWS, CB), F32),
        pltpu.VMEM((ROWS, CB), F32),
        pltpu.VMEM((NB, ROWS, CB), BF16),
        pltpu.VMEM((NB, BATCH, CB), F32),
    ]
    return pl.pallas_call(
        _block_kernel,
        grid=(NT,),
        in_specs=in_specs,
        out_specs=any_spec,
        out_shape=jax.ShapeDtypeStruct((BATCH, SEQ, D_MODEL), F32),
        scratch_shapes=scratch,
        compiler_params=pltpu.CompilerParams(
            dimension_semantics=("arbitrary",), vmem_limit_bytes=VMEM_LIMIT_BYTES),
        name="hybrid_block",
    )(*operands)
```

```python
import jax
import jax.numpy as jnp
from jax import lax
from jax.experimental import pallas as pl
from jax.experimental.pallas import tpu as pltpu

D_MODEL = 1024
BATCH = 16
SEQ = 2048
SC_K = 3
RG_W = 1280
RG_HEADS = 16
RG_HEAD_DIM = RG_W // RG_HEADS
RG_K = 4
LRU_C = 8.0
EPS = 1e-6

LANES = 128
MXU_DIM = 256
V7X_VMEM_BYTES = 64 * 1024 * 1024
TS = 32
NT = SEQ // TS
ROWS = TS * BATCH
XBUFS = 3
CB = MXU_DIM
NA = D_MODEL // CB
NB = RG_W // CB
GATE_WIN = 2 * CB
GATE_WIN_SLABS = GATE_WIN // LANES
SC_HIST = (SC_K - 1) * BATCH
RG_HIST = (RG_K - 1) * BATCH
VMEM_LIMIT_BYTES = V7X_VMEM_BYTES - 6 * 1024 * 1024

O_SCB, O_SCC, O_SCV, O_SCG = 0, D_MODEL, 2 * D_MODEL, 3 * D_MODEL
O_RGV = 4 * D_MODEL
O_RGG = O_RGV + RG_W
O_M = O_RGG + RG_W
P_IN = O_M + 2 * D_MODEL

F32 = jnp.float32
BF16 = jnp.bfloat16


def _gate_win_start_slab(k):
    return min(max(2 * k - 1, 0), RG_W // LANES - GATE_WIN_SLABS)


def _adaln_kernel(c_ref, w_ref, b_ref, o_ref):
    c = c_ref[...]
    o_ref[...] = jnp.dot(c * jax.nn.sigmoid(c), w_ref[...], preferred_element_type=F32,
                         precision=lax.Precision.HIGHEST) + b_ref[...]


def _adaln(c, w_ada, b_ada):
    n = w_ada.shape[1]
    nblk = n // D_MODEL
    return pl.pallas_call(
        _adaln_kernel,
        grid=(nblk,),
        in_specs=[
            pl.BlockSpec((BATCH, D_MODEL), lambda i: (0, 0)),
            pl.BlockSpec((D_MODEL, D_MODEL), lambda i: (0, i)),
            pl.BlockSpec((1, D_MODEL), lambda i: (0, i)),
        ],
        out_specs=pl.BlockSpec((BATCH, D_MODEL), lambda i: (0, i)),
        out_shape=jax.ShapeDtypeStruct((BATCH, n), F32),
        name="adaln",
    )(c, w_ada, b_ada.reshape(1, n))


def _dot(a, b):
    return jnp.dot(a, b, preferred_element_type=F32)


def _half_silu_from_half(hz):
    return hz + hz * jnp.tanh(hz)


def _tile_copies(hbm_ref, buf_ref, sem_ref, tile, slot, to_hbm):
    copies = []
    for b in range(BATCH):
        hbm = hbm_ref.at[b, pl.ds(tile * TS, TS), :]
        vmem = buf_ref.at[slot, :, b, :]
        src, dst = (vmem, hbm) if to_hbm else (hbm, vmem)
        copies.append(pltpu.make_async_copy(src, dst, sem_ref.at[slot]))
    return copies


def _block_kernel(x_hbm, mod_ref, gnorm_ref, gfinal_ref,
                  win_ref, sccw_ref, sccb_ref, scwout_ref,
                  rgcw_ref, rgcb_ref, wg_ref, gba_ref, gbx_ref, lam_ref,
                  rgwout_ref, bm_ref, wout_ref,
                  out_hbm,
                  xbuf, obuf, in_sem, out_sem,
                  h_tb, z0_scr, ubuf, s_scr, vbuf, hvf, vbf, a_scr, b_scr, yrg_scr, s2_scr, hstate):
    step = pl.program_id(0)
    xs_cur = lax.rem(step, XBUFS)
    xs_next = lax.rem(step + 1, XBUFS)
    xs_next2 = lax.rem(step + 2, XBUFS)
    hs_cur = lax.rem(step, 2)
    hs_next = 1 - hs_cur
    os_cur = hs_cur

    shift = mod_ref[:, 0:D_MODEL]
    scale1 = 1.0 + mod_ref[:, D_MODEL:2 * D_MODEL]
    gate = mod_ref[:, 2 * D_MODEL:3 * D_MODEL]

    def norm_in(x_slot, h_slot):
        x3 = xbuf[x_slot]
        ms = jnp.mean(x3 * x3, axis=-1, keepdims=True)
        hn = (x3 * lax.rsqrt(ms + EPS) * gnorm_ref[...]) * scale1 + shift
        h_tb[h_slot] = hn.reshape(ROWS, D_MODEL).astype(BF16)

    def proj(h_slot, col0, width):
        return _dot(h_tb[h_slot], win_ref[:, col0:col0 + width])

    def mixer_a_dots(h_slot, j):
        return tuple(proj(h_slot, o + j * CB, CB) for o in (O_SCB, O_SCC, O_SCV, O_SCG))

    def block0_dots_to_scratch(h_slot):
        for n, zn in enumerate(mixer_a_dots(h_slot, 0)):
            z0_scr[n] = zn

    @pl.when(step == 0)
    def _():
        ubuf[:, 0:SC_HIST, :] = jnp.zeros((NA, SC_HIST, CB), F32)
        vbuf[0:RG_HIST, :] = jnp.zeros((RG_HIST, RG_W), F32)
        hstate[...] = jnp.zeros_like(hstate)
        first = _tile_copies(x_hbm, xbuf, in_sem, 0, 0, to_hbm=False)
        for cp in first:
            cp.start()
        for cp in _tile_copies(x_hbm, xbuf, in_sem, 1, 1, to_hbm=False):
            cp.start()
        for cp in first:
            cp.wait()
        norm_in(0, 0)
        block0_dots_to_scratch(0)

    @pl.when(step >= 2)
    def _():
        for cp in _tile_copies(out_hbm, obuf, out_sem, step - 2, os_cur, to_hbm=True):
            cp.wait()

    for cp in _tile_copies(x_hbm, xbuf, in_sem, jnp.minimum(step + 2, NT - 1), xs_next2, to_hbm=False):
        cp.start()
    for cp in _tile_copies(x_hbm, xbuf, in_sem, jnp.minimum(step + 1, NT - 1), xs_next, to_hbm=False):
        cp.wait()

    def mixer_a_block(j, z):
        zb, zc, zv, hzg = z
        u = zc * zv
        ubuf[j, SC_HIST:SC_HIST + ROWS, :] = u
        cw = sccw_ref[j]
        conv = sccb_ref[j] + ubuf[j, 0:ROWS, :] * cw[0:1, :]
        conv = conv + ubuf[j, BATCH:BATCH + ROWS, :] * cw[1:2, :]
        conv = conv + u * cw[2:3, :]
        ubuf[j, 0:SC_HIST, :] = u[ROWS - SC_HIST:ROWS, :]
        s_scr[j] = (zb * conv * _half_silu_from_half(hzg)).astype(BF16)

    z = tuple(z0_scr[n] for n in range(4))
    for j in range(NA):
        z_next = mixer_a_dots(hs_cur, j + 1) if j + 1 < NA else None
        mixer_a_block(j, z)
        z = z_next

    norm_in(xs_next, hs_next)

    vz = proj(hs_cur, O_RGV, RG_W)
    s_all = jnp.concatenate([s_scr[j] for j in range(NA)], axis=1)
    ya_half = _dot(s_all, scwout_ref[...])
    vbuf[RG_HIST:RG_HIST + ROWS, :] = vz
    rcw = rgcw_ref[...]
    hv = rgcb_ref[...] + vbuf[0:ROWS, :] * rcw[0:1, :]
    hv = hv + vbuf[BATCH:BATCH + ROWS, :] * rcw[1:2, :]
    hv = hv + vbuf[2 * BATCH:2 * BATCH + ROWS, :] * rcw[2:3, :]
    hv = hv + vz * rcw[3:4, :]
    vbuf[0:RG_HIST, :] = vz[ROWS - RG_HIST:ROWS, :]
    for cidx in range(RG_W // LANES):
        vbf[cidx] = hv[:, cidx * LANES:(cidx + 1) * LANES].astype(BF16)
    for k in range(NB):
        hvf[k] = hv[:, k * CB:(k + 1) * CB]

    def gate_dots(k):
        s0 = _gate_win_start_slab(k)
        lhs = jnp.concatenate([vbf[s0 + i] for i in range(GATE_WIN_SLABS)], axis=1)
        return _dot(lhs, wg_ref[k]), proj(hs_cur, O_RGG + k * CB, CB)

    def mixer_b_block(k, hg, hzg):
        t_r = jnp.tanh(hg[:, 0:CB] + gba_ref[k])
        t_i = jnp.tanh(hg[:, CB:2 * CB] + gbx_ref[k])
        half_k = (-0.5 * LRU_C) * jax.nn.softplus(-lam_ref[k])
        log_a = half_k + half_k * t_r
        a = jnp.exp(log_a)
        a_scr[...] = a
        one_minus_a2 = jnp.tanh(log_a) * (-1.0 - a * a)
        root = jnp.where(one_minus_a2 > 0.0, one_minus_a2 * lax.rsqrt(one_minus_a2), 0.0)
        hvk = hvf[k]
        b_scr[...] = root * (hvk + hvk * t_i)
        h = hstate[k]
        for t in range(TS):
            rows = slice(t * BATCH, (t + 1) * BATCH)
            h = a_scr[rows, :] * h + b_scr[rows, :]
            yrg_scr[rows, :] = h
        hstate[k] = h
        s2_scr[k] = (yrg_scr[...] * _half_silu_from_half(hzg)).astype(BF16)

    hg, hzg = gate_dots(0)
    m = None
    for k in range(NB):
        if k + 1 < NB:
            hg_next, hzg_next = gate_dots(k + 1)
        else:
            m = proj(hs_cur, O_M, 2 * D_MODEL)
            hg_next = hzg_next = None
        mixer_b_block(k, hg, hzg)
        hg, hzg = hg_next, hzg_next

    s2_all = jnp.concatenate([s2_scr[k] for k in range(NB)], axis=1)
    yb_half = _dot(s2_all, rgwout_ref[...])

    t_a = jnp.tanh(m[:, 0:D_MODEL] + bm_ref[0:1, :])
    t_b = jnp.tanh(m[:, D_MODEL:2 * D_MODEL] + bm_ref[1:2, :])
    merged = ((ya_half + ya_half * t_a) + (yb_half + yb_half * t_b)).astype(BF16)
    y3 = _dot(merged, wout_ref[...]).reshape(TS, BATCH, D_MODEL)
    block0_dots_to_scratch(hs_next)
    xo = xbuf[xs_cur] + gate * y3
    ms = jnp.mean(xo * xo, axis=-1, keepdims=True)
    obuf[os_cur] = xo * lax.rsqrt(ms + EPS) * gfinal_ref[...]
    for cp in _tile_copies(out_hbm, obuf, out_sem, step, os_cur, to_hbm=True):
        cp.start()

    @pl.when(step == NT - 1)
    def _():
        for cp in _tile_copies(x_hbm, xbuf, in_sem, NT - 1, xs_next2, to_hbm=False):
            cp.wait()
        for cp in _tile_copies(out_hbm, obuf, out_sem, step - 1, 1 - os_cur, to_hbm=True):
            cp.wait()
        for cp in _tile_copies(out_hbm, obuf, out_sem, step, os_cur, to_hbm=True):
            cp.wait()


def _const_spec(shape):
    nd = len(shape)
    return pl.BlockSpec(shape, lambda i, _nd=nd: (0,) * _nd, pipeline_mode=pl.Buffered(1))


def _block_diag(w):
    h, d, _ = w.shape
    eye = jnp.eye(h, dtype=w.dtype)
    return (eye[:, None, :, None] * w[:, :, None, :]).reshape(h * d, h * d)


def _gate_weights(w_a, w_x):
    da = _block_diag(w_a)
    dx = _block_diag(w_x)
    blocks = []
    for k in range(NB):
        s = _gate_win_start_slab(k) * LANES
        cols = slice(k * CB, (k + 1) * CB)
        blocks.append(jnp.concatenate([da[s:s + GATE_WIN, cols], dx[s:s + GATE_WIN, cols]], axis=1))
    return jnp.stack(blocks).astype(BF16)


def kernel(x, c, w_ada, b_ada, g_norm, w_in, sc_conv_w, sc_conv_b, sc_w_out, rg_conv_w, rg_conv_b,
           rg_w_a, rg_b_a, rg_w_x, rg_b_x, rg_lambda, rg_w_out, b_merge, w_out, g_final):
    assert x.shape == (BATCH, SEQ, D_MODEL) and w_ada.shape[0] == 1
    l = 0
    mod = _adaln(c, w_ada[l], b_ada[l])

    col = jnp.arange(P_IN)
    halved = ((col >= O_SCG) & (col < O_RGV)) | (col >= O_RGG)
    win = (w_in[l] * jnp.where(halved, 0.5, 1.0).astype(F32)[None, :]).astype(BF16)

    sccw = sc_conv_w[l].reshape(SC_K, NA, CB).transpose(1, 0, 2)
    sccb = sc_conv_b[l].reshape(NA, 1, CB)
    wg = _gate_weights(rg_w_a[l], rg_w_x[l])
    gba = (0.5 * rg_b_a[l]).reshape(NB, 1, CB)
    gbx = (0.5 * rg_b_x[l]).reshape(NB, 1, CB)
    lam = rg_lambda[l].reshape(NB, 1, CB)

    operands = (
        x, mod, g_norm[l].reshape(1, D_MODEL), g_final.reshape(1, D_MODEL),
        win, sccw, sccb, (0.5 * sc_w_out[l]).astype(BF16),
        0.5 * rg_conv_w[l], (0.5 * rg_conv_b[l]).reshape(1, RG_W), wg, gba, gbx, lam,
        (0.5 * rg_w_out[l]).astype(BF16), 0.5 * b_merge[l], w_out[l].astype(BF16),
    )
    any_spec = pl.BlockSpec(memory_space=pl.ANY)
    in_specs = [any_spec] + [_const_spec(op.shape) for op in operands[1:]]
    scratch = [
        pltpu.VMEM((XBUFS, TS, BATCH, D_MODEL), F32),
        pltpu.VMEM((2, TS, BATCH, D_MODEL), F32),
        pltpu.SemaphoreType.DMA((XBUFS,)),
        pltpu.SemaphoreType.DMA((2,)),
        pltpu.VMEM((2, ROWS, D_MODEL), BF16),
        pltpu.VMEM((4, ROWS, CB), F32),
        pltpu.VMEM((NA, SC_HIST + ROWS, CB), F32),
        pltpu.VMEM((NA, ROWS, CB), BF16),
        pltpu.VMEM((RG_HIST + ROWS, RG_W), F32),
        pltpu.VMEM((NB, ROWS, CB), F32),
        pltpu.VMEM((RG_W // LANES, ROWS, LANES), BF16),
        pltpu.VMEM((ROWS, CB), F32),
        pltpu.VMEM((ROWS, CB), F32),
        pltpu.VMEM((ROWS, CB), F32),
        pltpu.VMEM((NB, ROWS, CB), BF16),
        pltpu.VMEM((NB, BATCH, CB), F32),
    ]
    return pl.pallas_call(
        _block_kernel,
        grid=(NT,),
        in_specs=in_specs,
        out_specs=any_spec,
        out_shape=jax.ShapeDtypeStruct((BATCH, SEQ, D_MODEL), F32),
        scratch_shapes=scratch,
        compiler_params=pltpu.CompilerParams(
            dimension_semantics=("arbitrary",), vmem_limit_bytes=VMEM_LIMIT_BYTES),
        name="hybrid_block",
    )(*operands)
```

```python
import jax
import jax.numpy as jnp
from jax import lax
from jax.experimental import pallas as pl
from jax.experimental.pallas import tpu as pltpu

D_MODEL = 1024
BATCH = 16
SEQ = 2048
SC_K = 3
RG_W = 1280
RG_HEADS = 16
RG_HEAD_DIM = RG_W // RG_HEADS
RG_K = 4
LRU_C = 8.0
EPS = 1e-6

LANES = 128
MXU_DIM = 256
V7X_VMEM_BYTES = 64 * 1024 * 1024
TS = 16
NT = SEQ // TS
ROWS = TS * BATCH
XBUFS = 3
CB = MXU_DIM
NA = D_MODEL // CB
NB = RG_W // CB
GATE_WIN = 2 * CB
GATE_WIN_SLABS = GATE_WIN // LANES
SC_HIST = (SC_K - 1) * BATCH
RG_HIST = (RG_K - 1) * BATCH
VMEM_LIMIT_BYTES = V7X_VMEM_BYTES - 6 * 1024 * 1024

O_SCB, O_SCC, O_SCV, O_SCG = 0, D_MODEL, 2 * D_MODEL, 3 * D_MODEL
O_RGV = 4 * D_MODEL
O_RGG = O_RGV + RG_W
O_M = O_RGG + RG_W
P_IN = O_M + 2 * D_MODEL
M_PIECE = 2 * D_MODEL // (NB - 1)

WP_SC = 0
WP_RG = WP_SC + D_MODEL
WP_OUT = WP_RG + RG_W
WP_ROWS = WP_OUT + D_MODEL

F32 = jnp.float32
BF16 = jnp.bfloat16


def _gate_win_start_slab(k):
    return min(max(2 * k - 1, 0), RG_W // LANES - GATE_WIN_SLABS)


def _adaln_kernel(c_ref, w_ref, b_ref, o_ref):
    c = c_ref[...]
    c_act = (c * jax.nn.sigmoid(c)).astype(BF16)
    o_ref[...] = jnp.dot(c_act, w_ref[...].astype(BF16), preferred_element_type=F32) + b_ref[...]


def _adaln(c, w_ada, b_ada):
    n = w_ada.shape[1]
    nblk = n // D_MODEL
    return pl.pallas_call(
        _adaln_kernel,
        grid=(nblk,),
        in_specs=[
            pl.BlockSpec((BATCH, D_MODEL), lambda i: (0, 0)),
            pl.BlockSpec((D_MODEL, D_MODEL), lambda i: (0, i)),
            pl.BlockSpec((1, D_MODEL), lambda i: (0, i)),
        ],
        out_specs=pl.BlockSpec((BATCH, D_MODEL), lambda i: (0, i)),
        out_shape=jax.ShapeDtypeStruct((BATCH, n), F32),
        name="adaln",
    )(c, w_ada, b_ada.reshape(1, n))


def _dot(a, b):
    return jnp.dot(a, b, preferred_element_type=F32)


def _half_silu_from_half(hz):
    return hz + hz * jnp.tanh(hz)


def _tile_copies(hbm_ref, buf_ref, sem_ref, tile, slot, to_hbm):
    copies = []
    for b in range(BATCH):
        hbm = hbm_ref.at[b, pl.ds(tile * TS, TS), :]
        vmem = buf_ref.at[slot, :, b, :]
        src, dst = (vmem, hbm) if to_hbm else (hbm, vmem)
        copies.append(pltpu.make_async_copy(src, dst, sem_ref.at[slot]))
    return copies


def _block_kernel(x_hbm, mod_ref, gnorm_ref, gfinal_ref, sccw_ref, sccb_ref, rgcw_ref, rgcb_ref,
                  gba_ref, gbx_ref, lam_ref, bm_ref, win_ref, wg_ref, wproj_ref,
                  out_hbm,
                  xbuf, obuf, in_sem, out_sem,
                  h_tb, z0_scr, ubuf, s_scr, vbuf, hvf, vbf, a_scr, b_scr, yrg_scr, s2_scr, hstate):
    step = pl.program_id(0)
    xs_cur = lax.rem(step, XBUFS)
    xs_next = lax.rem(step + 1, XBUFS)
    xs_next2 = lax.rem(step + 2, XBUFS)
    os_cur = lax.rem(step, 2)
    h_cur = h_tb.at[os_cur]
    h_next = h_tb.at[1 - os_cur]

    shift = mod_ref[:, 0:D_MODEL]
    scale1 = 1.0 + mod_ref[:, D_MODEL:2 * D_MODEL]
    gate = mod_ref[:, 2 * D_MODEL:3 * D_MODEL]

    def norm_in(x_slot, h_ref):
        x3 = xbuf[x_slot]
        ms = jnp.mean(x3 * x3, axis=-1, keepdims=True)
        hn = (x3 * lax.rsqrt(ms + EPS) * gnorm_ref[...]) * scale1 + shift
        h_ref[...] = hn.reshape(ROWS, D_MODEL).astype(BF16)

    def proj(h_ref, col0, width):
        return _dot(h_ref[...], win_ref[:, col0:col0 + width])

    def mixer_a_dots(h_ref, j):
        return tuple(proj(h_ref, o + j * CB, CB) for o in (O_SCB, O_SCC, O_SCV, O_SCG))

    def block0_dots_to_scratch(h_ref):
        for n, zn in enumerate(mixer_a_dots(h_ref, 0)):
            z0_scr[n] = zn

    @pl.when(step == 0)
    def _():
        ubuf[:, 0:SC_HIST, :] = jnp.zeros((NA, SC_HIST, CB), F32)
        vbuf[0:RG_HIST, :] = jnp.zeros((RG_HIST, RG_W), F32)
        hstate[...] = jnp.zeros_like(hstate)
        first = _tile_copies(x_hbm, xbuf, in_sem, 0, 0, to_hbm=False)
        for cp in first:
            cp.start()
        for cp in _tile_copies(x_hbm, xbuf, in_sem, 1, 1, to_hbm=False):
            cp.start()
        for cp in first:
            cp.wait()
        norm_in(0, h_cur)
        block0_dots_to_scratch(h_cur)

    @pl.when(step >= 2)
    def _():
        for cp in _tile_copies(out_hbm, obuf, out_sem, step - 2, os_cur, to_hbm=True):
            cp.wait()

    for cp in _tile_copies(x_hbm, xbuf, in_sem, jnp.minimum(step + 2, NT - 1), xs_next2, to_hbm=False):
        cp.start()
    for cp in _tile_copies(x_hbm, xbuf, in_sem, jnp.minimum(step + 1, NT - 1), xs_next, to_hbm=False):
        cp.wait()

    def mixer_a_block(j, z):
        zb, zc, zv, hzg = z
        u = zc * zv
        ubuf[j, SC_HIST:SC_HIST + ROWS, :] = u
        cw = sccw_ref[:, j * CB:(j + 1) * CB]
        conv = sccb_ref[:, j * CB:(j + 1) * CB] + ubuf[j, 0:ROWS, :] * cw[0:1, :]
        conv = conv + ubuf[j, BATCH:BATCH + ROWS, :] * cw[1:2, :]
        conv = conv + u * cw[2:3, :]
        ubuf[j, 0:SC_HIST, :] = u[ROWS - SC_HIST:ROWS, :]
        s_scr[j] = (zb * conv * _half_silu_from_half(hzg)).astype(BF16)

    z = tuple(z0_scr[n] for n in range(4))
    for j in range(NA):
        z_next = mixer_a_dots(h_cur, j + 1) if j + 1 < NA else None
        if j == 0:
            norm_in(xs_next, h_next)
        mixer_a_block(j, z)
        z = z_next

    vz = proj(h_cur, O_RGV, RG_W)
    m_pieces = [proj(h_cur, O_M, M_PIECE)]
    s_all = jnp.concatenate([s_scr[j] for j in range(NA)], axis=1)
    ya_half = _dot(s_all, wproj_ref[WP_SC:WP_SC + D_MODEL, :])
    vbuf[RG_HIST:RG_HIST + ROWS, :] = vz
    rcw = 0.5 * rgcw_ref[...]
    hv = 0.5 * rgcb_ref[...] + vbuf[0:ROWS, :] * rcw[0:1, :]
    hv = hv + vbuf[BATCH:BATCH + ROWS, :] * rcw[1:2, :]
    hv = hv + vbuf[2 * BATCH:2 * BATCH + ROWS, :] * rcw[2:3, :]
    hv = hv + vz * rcw[3:4, :]
    vbuf[0:RG_HIST, :] = vz[ROWS - RG_HIST:ROWS, :]
    for cidx in range(RG_W // LANES):
        vbf[cidx] = hv[:, cidx * LANES:(cidx + 1) * LANES].astype(BF16)
    for k in range(NB):
        hvf[k] = hv[:, k * CB:(k + 1) * CB]

    def gate_dots(k):
        s0 = _gate_win_start_slab(k)
        lhs = jnp.concatenate([vbf[s0 + i] for i in range(GATE_WIN_SLABS)], axis=1)
        return _dot(lhs, wg_ref[k]), proj(h_cur, O_RGG + k * CB, CB)

    def mixer_b_block(k, hg, hzg):
        cols = slice(k * CB, (k + 1) * CB)
        t_r = jnp.tanh(hg[:, 0:CB] + 0.5 * gba_ref[:, cols])
        t_i = jnp.tanh(hg[:, CB:2 * CB] + 0.5 * gbx_ref[:, cols])
        half_k = (-0.5 * LRU_C) * jax.nn.softplus(-lam_ref[:, cols])
        log_a = half_k + half_k * t_r
        a = jnp.exp(log_a)
        a_scr[...] = a
        one_minus_a2 = jnp.tanh(log_a) * (-1.0 - a * a)
        root = jnp.where(one_minus_a2 > 0.0, one_minus_a2 * lax.rsqrt(one_minus_a2), 0.0)
        hvk = hvf[k]
        b_scr[...] = root * (hvk + hvk * t_i)
        h = hstate[k]
        for t in range(TS):
            rows = slice(t * BATCH, (t + 1) * BATCH)
            h = a_scr[rows, :] * h + b_scr[rows, :]
            yrg_scr[rows, :] = h
        hstate[k] = h
        s2_scr[k] = (yrg_scr[...] * _half_silu_from_half(hzg)).astype(BF16)

    hg, hzg = gate_dots(0)
    for k in range(NB):
        hg_next, hzg_next = gate_dots(k + 1) if k + 1 < NB else (None, None)
        if 1 <= k < NB - 1:
            m_pieces.append(proj(h_cur, O_M + k * M_PIECE, M_PIECE))
        mixer_b_block(k, hg, hzg)
        hg, hzg = hg_next, hzg_next
    m = jnp.concatenate(m_pieces, axis=1)

    s2_all = jnp.concatenate([s2_scr[k] for k in range(NB)], axis=1)
    yb_half = _dot(s2_all, wproj_ref[WP_RG:WP_RG + RG_W, :])

    t_a = jnp.tanh(m[:, 0:D_MODEL] + 0.5 * bm_ref[0:1, :])
    t_b = jnp.tanh(m[:, D_MODEL:2 * D_MODEL] + 0.5 * bm_ref[1:2, :])
    merged = ((ya_half + ya_half * t_a) + (yb_half + yb_half * t_b)).astype(BF16)
    y3 = _dot(merged, wproj_ref[WP_OUT:WP_OUT + D_MODEL, :]).reshape(TS, BATCH, D_MODEL)
    block0_dots_to_scratch(h_next)
    xo = xbuf[xs_cur] + gate * y3
    ms = jnp.mean(xo * xo, axis=-1, keepdims=True)
    obuf[os_cur] = xo * lax.rsqrt(ms + EPS) * gfinal_ref[...]
    for cp in _tile_copies(out_hbm, obuf, out_sem, step, os_cur, to_hbm=True):
        cp.start()

    @pl.when(step == NT - 1)
    def _():
        for cp in _tile_copies(x_hbm, xbuf, in_sem, NT - 1, xs_next2, to_hbm=False):
            cp.wait()
        for cp in _tile_copies(out_hbm, obuf, out_sem, step - 1, 1 - os_cur, to_hbm=True):
            cp.wait()
        for cp in _tile_copies(out_hbm, obuf, out_sem, step, os_cur, to_hbm=True):
            cp.wait()


def _const_spec(shape):
    nd = len(shape)
    return pl.BlockSpec(shape, lambda i, _nd=nd: (0,) * _nd, pipeline_mode=pl.Buffered(1))


def _gate_weights(w_a, w_x):
    win_start = jnp.array([_gate_win_start_slab(k) * LANES for k in range(NB)])
    row_ch = win_start[:, None] + jnp.arange(GATE_WIN)[None, :]
    col_ch = jnp.arange(NB)[:, None] * CB + jnp.arange(CB)[None, :]
    same_head = (row_ch // RG_HEAD_DIM)[:, :, None] == (col_ch // RG_HEAD_DIM)[:, None, :]
    spread = (jnp.arange(RG_HEAD_DIM)[None, :, None] == (col_ch % RG_HEAD_DIM)[:, None, :]).astype(BF16)

    def expand(w):
        rows = w.reshape(RG_W, RG_HEAD_DIM).astype(BF16)
        win = jnp.stack([rows[_gate_win_start_slab(k) * LANES:_gate_win_start_slab(k) * LANES + GATE_WIN]
                         for k in range(NB)])
        dense = jnp.einsum("kre,kec->krc", win, spread, preferred_element_type=F32)
        return jnp.where(same_head, dense, 0.0).astype(BF16)

    return jnp.concatenate([expand(w_a), expand(w_x)], axis=2)


def kernel(x, c, w_ada, b_ada, g_norm, w_in, sc_conv_w, sc_conv_b, sc_w_out, rg_conv_w, rg_conv_b,
           rg_w_a, rg_b_a, rg_w_x, rg_b_x, rg_lambda, rg_w_out, b_merge, w_out, g_final):
    assert x.shape == (BATCH, SEQ, D_MODEL) and w_ada.shape[0] == 1
    l = 0
    mod = _adaln(c, w_ada[l], b_ada[l])

    col = jnp.arange(P_IN)
    halved = ((col >= O_SCG) & (col < O_RGV)) | (col >= O_RGG)
    win = (w_in[l] * jnp.where(halved, 0.5, 1.0).astype(F32)[None, :]).astype(BF16)

    wg = _gate_weights(rg_w_a[l], rg_w_x[l])
    wproj = jnp.concatenate([0.5 * sc_w_out[l], 0.5 * rg_w_out[l], w_out[l]], axis=0).astype(BF16)

    assert wproj.shape == (WP_ROWS, D_MODEL)

    operands = (
        x, mod, g_norm[l].reshape(1, D_MODEL), g_final.reshape(1, D_MODEL),
        sc_conv_w[l], sc_conv_b[l].reshape(1, D_MODEL), rg_conv_w[l], rg_conv_b[l].reshape(1, RG_W),
        rg_b_a[l].reshape(1, RG_W), rg_b_x[l].reshape(1, RG_W), rg_lambda[l].reshape(1, RG_W), b_merge[l],
        win, wg, wproj)
    any_spec = pl.BlockSpec(memory_space=pl.ANY)
    in_specs = [any_spec] + [_const_spec(op.shape) for op in operands[1:]]
    scratch = [
        pltpu.VMEM((XBUFS, TS, BATCH, D_MODEL), F32),
        pltpu.VMEM((2, TS, BATCH, D_MODEL), F32),
        pltpu.SemaphoreType.DMA((XBUFS,)),
        pltpu.SemaphoreType.DMA((2,)),
        pltpu.VMEM((2, ROWS, D_MODEL), BF16),
        pltpu.VMEM((4, ROWS, CB), F32),
        pltpu.VMEM((NA, SC_HIST + ROWS, CB), F32),
        pltpu.VMEM((NA, ROWS, CB), BF16),
        pltpu.VMEM((RG_HIST + ROWS, RG_W), F32),
        pltpu.VMEM((NB, ROWS, CB), F32),
        pltpu.VMEM((RG_W // LANES, ROWS, LANES), BF16),
        pltpu.VMEM((ROWS, CB), F32),
        pltpu.VMEM((ROWS, CB), F32),
        pltpu.VMEM((ROWS, CB), F32),
        pltpu.VMEM((NB, ROWS, CB), BF16),
        pltpu.VMEM((NB, BATCH, CB), F32),
    ]
    return pl.pallas_call(
        _block_kernel,
        grid=(NT,),
        in_specs=in_specs,
        out_specs=any_spec,
        out_shape=jax.ShapeDtypeStruct((BATCH, SEQ, D_MODEL), F32),
        scratch_shapes=scratch,
        compiler_params=pltpu.CompilerParams(
            dimension_semantics=("arbitrary",), vmem_limit_bytes=VMEM_LIMIT_BYTES),
        name="hybrid_block",
    )(*operands)
```

```python
import jax
import jax.numpy as jnp
from jax import lax
from jax.experimental import pallas as pl
from jax.experimental.pallas import tpu as pltpu

D_MODEL = 1024
BATCH = 16
SEQ = 2048
SC_K = 3
RG_W = 1280
RG_HEADS = 16
RG_HEAD_DIM = RG_W // RG_HEADS
RG_K = 4
LRU_C = 8.0
EPS = 1e-6

LANES = 128
MXU_DIM = 256
V7X_VMEM_BYTES = 64 * 1024 * 1024
TS = 16
TPS = 2
NTILES = SEQ // TS
NSTEPS = NTILES // TPS
assert TPS % 2 == 0
ROWS = TS * BATCH
XBUFS = 2 * TPS + 1
OBUFS = 2 * TPS
CB = MXU_DIM
NA = D_MODEL // CB
NB = RG_W // CB
GB = LANES
NG = RG_W // GB
GATE_WIN = MXU_DIM
GATE_SHIFT = LANES // 2
SC_HIST = (SC_K - 1) * BATCH
RG_HIST = (RG_K - 1) * BATCH
VMEM_LIMIT_BYTES = V7X_VMEM_BYTES - 6 * 1024 * 1024

O_SCB, O_SCC, O_SCV, O_SCG = 0, D_MODEL, 2 * D_MODEL, 3 * D_MODEL
O_RGV = 4 * D_MODEL
O_RGG = O_RGV + RG_W
O_M = O_RGG + RG_W
P_IN = O_M + 2 * D_MODEL
M_PIECE = 2 * D_MODEL // (NB - 1)

WP_SC = 0
WP_RG = WP_SC + D_MODEL
WP_OUT = WP_RG + RG_W
WP_ROWS = WP_OUT + D_MODEL

WIN_CHUNK = 64
WP_CHUNK = 256
WIN_COL_SCALES = ((0, O_SCG, 1.0), (O_SCG, O_RGV, 0.5), (O_RGV, O_RGG, 1.0), (O_RGG, P_IN, 0.5))

F32 = jnp.float32
BF16 = jnp.bfloat16


def _gate_win_start(c):
    lo = (c * GB) // RG_HEAD_DIM * RG_HEAD_DIM
    hi = -(-(c * GB + GB) // RG_HEAD_DIM) * RG_HEAD_DIM
    start = min(lo // GATE_SHIFT * GATE_SHIFT, RG_W - GATE_WIN)
    assert start <= lo and hi <= start + GATE_WIN
    return start


def _adaln_kernel(c_ref, w_ref, b_ref, o_ref):
    c = c_ref[...]
    c_act = (c * jax.nn.sigmoid(c)).astype(BF16)
    o_ref[...] = jnp.dot(c_act, w_ref[...].astype(BF16), preferred_element_type=F32) + b_ref[...]


def _adaln(c, w_ada, b_ada):
    n = w_ada.shape[1]
    nblk = n // D_MODEL
    return pl.pallas_call(
        _adaln_kernel,
        grid=(nblk,),
        in_specs=[
            pl.BlockSpec((BATCH, D_MODEL), lambda i: (0, 0)),
            pl.BlockSpec((D_MODEL, D_MODEL), lambda i: (0, i)),
            pl.BlockSpec((1, D_MODEL), lambda i: (0, i)),
        ],
        out_specs=pl.BlockSpec((BATCH, D_MODEL), lambda i: (0, i)),
        out_shape=jax.ShapeDtypeStruct((BATCH, n), F32),
        name="adaln",
    )(c, w_ada, b_ada.reshape(1, n))


def _dot(a, b):
    return jnp.dot(a, b, preferred_element_type=F32)


def _half_silu_from_half(hz):
    return hz + hz * jnp.tanh(hz)


def _tile_copies(hbm_ref, buf_ref, sem_ref, tile, slot, to_hbm):
    copies = []
    for b in range(BATCH):
        hbm = hbm_ref.at[b, pl.ds(tile * TS, TS), :]
        vmem = buf_ref.at[slot, :, b, :]
        src, dst = (vmem, hbm) if to_hbm else (hbm, vmem)
        copies.append(pltpu.make_async_copy(src, dst, sem_ref.at[slot]))
    return copies


def _block_kernel(x_hbm, mod_ref, gnorm_ref, gfinal_ref, sccw_ref, sccb_ref, rgcw_ref, rgcb_ref,
                  gba_ref, gbx_ref, lam_ref, bm_ref, wg_ref, win_hbm, scw_hbm, rgw_hbm, ow_hbm,
                  out_hbm,
                  win_ref, wproj_ref, stage_in, stage_proj, w_sem,
                  xbuf, obuf, in_sem, out_sem,
                  h_tb, z0_scr, ubuf, s_scr, vbuf, hvf, vbf, vbs, a_scr, b_scr, yrg_scr, s2_scr, hstate):
    step = pl.program_id(0)
    tile0 = step * TPS
    oslot0 = lax.rem(step, 2) * TPS

    def in_copies(tile):
        if isinstance(tile, int):
            return _tile_copies(x_hbm, xbuf, in_sem, min(tile, NTILES - 1), tile % XBUFS, to_hbm=False)
        return _tile_copies(x_hbm, xbuf, in_sem, jnp.minimum(tile, NTILES - 1),
                            lax.rem(tile, XBUFS), to_hbm=False)

    def out_copies(tile, oslot):
        return _tile_copies(out_hbm, obuf, out_sem, tile, oslot, to_hbm=True)

    shift = mod_ref[:, 0:D_MODEL]
    scale1 = 1.0 + mod_ref[:, D_MODEL:2 * D_MODEL]
    gate = mod_ref[:, 2 * D_MODEL:3 * D_MODEL]

    def norm_in(x_slot, h_ref):
        x3 = xbuf[x_slot]
        ms = jnp.mean(x3 * x3, axis=-1, keepdims=True)
        hn = (x3 * lax.rsqrt(ms + EPS) * gnorm_ref[...]) * scale1 + shift
        h_ref[...] = hn.reshape(ROWS, D_MODEL).astype(BF16)

    def proj(h_ref, col0, width):
        return _dot(h_ref[...], win_ref[:, col0:col0 + width])

    def mixer_a_dots(h_ref, j):
        return tuple(proj(h_ref, o + j * CB, CB) for o in (O_SCB, O_SCC, O_SCV, O_SCG))

    def block0_dots_to_scratch(h_ref):
        for n, zn in enumerate(mixer_a_dots(h_ref, 0)):
            z0_scr[n] = zn

    def load_weights():
        def win_copy(i, slot):
            return pltpu.make_async_copy(win_hbm.at[pl.ds(i * WIN_CHUNK, WIN_CHUNK), :],
                                         stage_in.at[slot], w_sem.at[slot])

        win_copy(0, 0).start()

        def win_body(i, carry):
            slot = lax.rem(i, 2)

            @pl.when(i + 1 < D_MODEL // WIN_CHUNK)
            def _():
                win_copy(i + 1, 1 - slot).start()

            win_copy(i, slot).wait()
            rows = pl.ds(pl.multiple_of(i * WIN_CHUNK, WIN_CHUNK), WIN_CHUNK)
            for c0, c1, scale in WIN_COL_SCALES:
                v = stage_in[slot, :, c0:c1]
                win_ref[rows, c0:c1] = (v if scale == 1.0 else scale * v).astype(BF16)
            return carry

        lax.fori_loop(0, D_MODEL // WIN_CHUNK, win_body, 0)

        chunks = []
        for src, dst0, nrows, scale in ((scw_hbm, WP_SC, D_MODEL, 0.5), (rgw_hbm, WP_RG, RG_W, 0.5),
                                        (ow_hbm, WP_OUT, D_MODEL, 1.0)):
            chunks += [(src, r, dst0 + r, scale) for r in range(0, nrows, WP_CHUNK)]

        def proj_copy(n):
            src, r, _, _ = chunks[n]
            return pltpu.make_async_copy(src.at[r:r + WP_CHUNK, :], stage_proj.at[n % 2], w_sem.at[n % 2])

        proj_copy(0).start()
        for n, (_, _, dst, scale) in enumerate(chunks):
            if n + 1 < len(chunks):
                proj_copy(n + 1).start()
            proj_copy(n).wait()
            v = stage_proj[n % 2]
            wproj_ref[dst:dst + WP_CHUNK, :] = (v if scale == 1.0 else scale * v).astype(BF16)

    @pl.when(step == 0)
    def _():
        ubuf[:, 0:SC_HIST, :] = jnp.zeros((NA, SC_HIST, CB), F32)
        vbuf[0:RG_HIST, :] = jnp.zeros((RG_HIST, RG_W), F32)
        hstate[...] = jnp.zeros_like(hstate)
        for t in range(TPS + 1):
            for cp in in_copies(t):
                cp.start()
        load_weights()
        for cp in in_copies(0):
            cp.wait()
        norm_in(0, h_tb.at[0])
        block0_dots_to_scratch(h_tb.at[0])

    @pl.when(step >= 2)
    def _():
        for c in range(TPS):
            for cp in out_copies(tile0 - 2 * TPS + c, oslot0 + c):
                cp.wait()

    for c in range(TPS):
        for cp in in_copies(tile0 + TPS + 1 + c):
            cp.start()
    for c in range(TPS):
        for cp in in_copies(tile0 + 1 + c):
            cp.wait()

    def mixer_a_block(j, z):
        zb, zc, zv, hzg = z
        u = zc * zv
        ubuf[j, SC_HIST:SC_HIST + ROWS, :] = u
        cw = sccw_ref[:, j * CB:(j + 1) * CB]
        conv = sccb_ref[:, j * CB:(j + 1) * CB] + ubuf[j, 0:ROWS, :] * cw[0:1, :]
        conv = conv + ubuf[j, BATCH:BATCH + ROWS, :] * cw[1:2, :]
        conv = conv + u * cw[2:3, :]
        ubuf[j, 0:SC_HIST, :] = u[ROWS - SC_HIST:ROWS, :]
        s_scr[j] = (zb * conv * _half_silu_from_half(hzg)).astype(BF16)

    def gate_dot(c):
        start = _gate_win_start(c)
        slabs = vbf if start % LANES == 0 else vbs
        s0 = start // LANES
        lhs = jnp.concatenate([slabs[s0 + i] for i in range(GATE_WIN // LANES)], axis=1)
        return _dot(lhs, wg_ref[c])

    def gate_dots(h_cur, k):
        pieces = [gate_dot(c) for c in range(k * CB // GB, (k + 1) * CB // GB)]
        hg_a = jnp.concatenate([p[:, 0:GB] for p in pieces], axis=1)
        hg_x = jnp.concatenate([p[:, GB:2 * GB] for p in pieces], axis=1)
        return (hg_a, hg_x), proj(h_cur, O_RGG + k * CB, CB)

    def mixer_b_block(k, hg, hzg):
        hg_a, hg_x = hg
        cols = slice(k * CB, (k + 1) * CB)
        t_r = jnp.tanh(hg_a + 0.5 * gba_ref[:, cols])
        t_i = jnp.tanh(hg_x + 0.5 * gbx_ref[:, cols])
        half_k = (-0.5 * LRU_C) * jax.nn.softplus(-lam_ref[:, cols])
        log_a = half_k + half_k * t_r
        a = jnp.exp(log_a)
        a_scr[...] = a
        one_minus_a2 = jnp.tanh(log_a) * (-1.0 - a * a)
        root = jnp.where(one_minus_a2 > 0.0, one_minus_a2 * lax.rsqrt(one_minus_a2), 0.0)
        hvk = hvf[k]
        b_scr[...] = root * (hvk + hvk * t_i)
        h = hstate[k]
        for t in range(TS):
            rows = slice(t * BATCH, (t + 1) * BATCH)
            h = a_scr[rows, :] * h + b_scr[rows, :]
            yrg_scr[rows, :] = h
        hstate[k] = h
        s2_scr[k] = (yrg_scr[...] * _half_silu_from_half(hzg)).astype(BF16)

    def process_tile(c):
        tile = tile0 + c
        xs_cur = lax.rem(tile, XBUFS)
        xs_next = lax.rem(tile + 1, XBUFS)
        h_cur = h_tb.at[c % 2]
        h_next = h_tb.at[(c + 1) % 2]

        z = tuple(z0_scr[n] for n in range(4))
        for j in range(NA):
            z_next = mixer_a_dots(h_cur, j + 1) if j + 1 < NA else None
            if j == 0:
                norm_in(xs_next, h_next)
            mixer_a_block(j, z)
            z = z_next

        vz = proj(h_cur, O_RGV, RG_W)
        m_pieces = [proj(h_cur, O_M, M_PIECE)]
        s_all = jnp.concatenate([s_scr[j] for j in range(NA)], axis=1)
        ya_half = _dot(s_all, wproj_ref[WP_SC:WP_SC + D_MODEL, :])
        vbuf[RG_HIST:RG_HIST + ROWS, :] = vz
        rcw = 0.5 * rgcw_ref[...]
        hv = 0.5 * rgcb_ref[...] + vbuf[0:ROWS, :] * rcw[0:1, :]
        hv = hv + vbuf[BATCH:BATCH + ROWS, :] * rcw[1:2, :]
        hv = hv + vbuf[2 * BATCH:2 * BATCH + ROWS, :] * rcw[2:3, :]
        hv = hv + vz * rcw[3:4, :]
        vbuf[0:RG_HIST, :] = vz[ROWS - RG_HIST:ROWS, :]
        for cidx in range(NG):
            vbf[cidx] = hv[:, cidx * LANES:(cidx + 1) * LANES].astype(BF16)
        for cidx in range(NG - 1):
            lo = GATE_SHIFT + cidx * LANES
            vbs[cidx] = hv[:, lo:lo + LANES].astype(BF16)
        for k in range(NB):
            hvf[k] = hv[:, k * CB:(k + 1) * CB]

        hg, hzg = gate_dots(h_cur, 0)
        for k in range(NB):
            hg_next, hzg_next = gate_dots(h_cur, k + 1) if k + 1 < NB else (None, None)
            if 1 <= k < NB - 1:
                m_pieces.append(proj(h_cur, O_M + k * M_PIECE, M_PIECE))
            mixer_b_block(k, hg, hzg)
            hg, hzg = hg_next, hzg_next
        m = jnp.concatenate(m_pieces, axis=1)

        s2_all = jnp.concatenate([s2_scr[k] for k in range(NB)], axis=1)
        yb_half = _dot(s2_all, wproj_ref[WP_RG:WP_RG + RG_W, :])

        t_a = jnp.tanh(m[:, 0:D_MODEL] + 0.5 * bm_ref[0:1, :])
        t_b = jnp.tanh(m[:, D_MODEL:2 * D_MODEL] + 0.5 * bm_ref[1:2, :])
        merged = ((ya_half + ya_half * t_a) + (yb_half + yb_half * t_b)).astype(BF16)
        y3 = _dot(merged, wproj_ref[WP_OUT:WP_OUT + D_MODEL, :]).reshape(TS, BATCH, D_MODEL)
        block0_dots_to_scratch(h_next)
        xo = xbuf[xs_cur] + gate * y3
        ms = jnp.mean(xo * xo, axis=-1, keepdims=True)
        obuf[oslot0 + c] = xo * lax.rsqrt(ms + EPS) * gfinal_ref[...]

    for c in range(TPS):
        process_tile(c)

    for c in range(TPS):
        for cp in out_copies(tile0 + c, oslot0 + c):
            cp.start()

    @pl.when(step == NSTEPS - 1)
    def _():
        for c in range(TPS):
            for cp in in_copies(tile0 + TPS + 1 + c):
                cp.wait()
            for cp in out_copies(tile0 - TPS + c, (TPS - oslot0) + c):
                cp.wait()
            for cp in out_copies(tile0 + c, oslot0 + c):
                cp.wait()


def _const_spec(shape):
    nd = len(shape)
    return pl.BlockSpec(shape, lambda i, _nd=nd: (0,) * _nd, pipeline_mode=pl.Buffered(1))


def _gate_weights(w_a, w_x):
    starts = [_gate_win_start(c) for c in range(NG)]
    row_ch = jnp.array(starts)[:, None] + jnp.arange(GATE_WIN)[None, :]
    col_ch = jnp.arange(NG)[:, None] * GB + jnp.arange(GB)[None, :]
    same_head = (row_ch // RG_HEAD_DIM)[:, :, None] == (col_ch // RG_HEAD_DIM)[:, None, :]
    spread = (jnp.arange(RG_HEAD_DIM)[None, :, None] == (col_ch % RG_HEAD_DIM)[:, None, :]).astype(BF16)

    def expand(w):
        rows = w.reshape(RG_W, RG_HEAD_DIM).astype(BF16)
        win = jnp.stack([rows[s:s + GATE_WIN] for s in starts])
        dense = jnp.einsum("kre,kec->krc", win, spread, preferred_element_type=F32)
        return jnp.where(same_head, dense, 0.0).astype(BF16)

    return jnp.concatenate([expand(w_a), expand(w_x)], axis=2)


def kernel(x, c, w_ada, b_ada, g_norm, w_in, sc_conv_w, sc_conv_b, sc_w_out, rg_conv_w, rg_conv_b,
           rg_w_a, rg_b_a, rg_w_x, rg_b_x, rg_lambda, rg_w_out, b_merge, w_out, g_final):
    assert x.shape == (BATCH, SEQ, D_MODEL) and w_ada.shape[0] == 1
    l = 0
    mod = _adaln(c, w_ada[l], b_ada[l])

    wg = _gate_weights(rg_w_a[l], rg_w_x[l])

    vmem_operands = (
        mod, g_norm[l].reshape(1, D_MODEL), g_final.reshape(1, D_MODEL),
        sc_conv_w[l], sc_conv_b[l].reshape(1, D_MODEL), rg_conv_w[l], rg_conv_b[l].reshape(1, RG_W),
        rg_b_a[l].reshape(1, RG_W), rg_b_x[l].reshape(1, RG_W), rg_lambda[l].reshape(1, RG_W), b_merge[l],
        wg)
    hbm_weights = (w_in[l], sc_w_out[l], rg_w_out[l], w_out[l])
    operands = (x,) + vmem_operands + hbm_weights
    any_spec = pl.BlockSpec(memory_space=pl.ANY)
    in_specs = [any_spec] + [_const_spec(op.shape) for op in vmem_operands] + [any_spec] * len(hbm_weights)
    scratch = [
        pltpu.VMEM((D_MODEL, P_IN), BF16),
        pltpu.VMEM((WP_ROWS, D_MODEL), BF16),
        pltpu.VMEM((2, WIN_CHUNK, P_IN), F32),
        pltpu.VMEM((2, WP_CHUNK, D_MODEL), F32),
        pltpu.SemaphoreType.DMA((2,)),
        pltpu.VMEM((XBUFS, TS, BATCH, D_MODEL), F32),
        pltpu.VMEM((OBUFS, TS, BATCH, D_MODEL), F32),
        pltpu.SemaphoreType.DMA((XBUFS,)),
        pltpu.SemaphoreType.DMA((OBUFS,)),
        pltpu.VMEM((2, ROWS, D_MODEL), BF16),
        pltpu.VMEM((4, ROWS, CB), F32),
        pltpu.VMEM((NA, SC_HIST + ROWS, CB), F32),
        pltpu.VMEM((NA, ROWS, CB), BF16),
        pltpu.VMEM((RG_HIST + ROWS, RG_W), F32),
        pltpu.VMEM((NB, ROWS, CB), F32),
        pltpu.VMEM((NG, ROWS, LANES), BF16),
        pltpu.VMEM((NG - 1, ROWS, LANES), BF16),
        pltpu.VMEM((ROWS, CB), F32),
        pltpu.VMEM((ROWS, CB), F32),
        pltpu.VMEM((ROWS, CB), F32),
        pltpu.VMEM((NB, ROWS, CB), BF16),
        pltpu.VMEM((NB, BATCH, CB), F32),
    ]
    return pl.pallas_call(
        _block_kernel,
        grid=(NSTEPS,),
        in_specs=in_specs,
        out_specs=any_spec,
        out_shape=jax.ShapeDtypeStruct((BATCH, SEQ, D_MODEL), F32),
        scratch_shapes=scratch,
        compiler_params=pltpu.CompilerParams(
            dimension_semantics=("arbitrary",), vmem_limit_bytes=VMEM_LIMIT_BYTES),
        name="hybrid_block",
    )(*operands)
```

```python
import jax
import jax.numpy as jnp
from jax import lax
from jax.experimental import pallas as pl
from jax.experimental.pallas import tpu as pltpu

D_MODEL = 1024
BATCH = 16
SEQ = 2048
SC_K = 3
RG_W = 1280
RG_HEADS = 16
RG_HEAD_DIM = RG_W // RG_HEADS
RG_K = 4
LRU_C = 8.0
EPS = 1e-6

LANES = 128
MXU_DIM = 256
V7X_VMEM_BYTES = 64 * 1024 * 1024
TS = 16
TPS = 2
NTILES = SEQ // TS
NSTEPS = NTILES // TPS
assert TPS % 2 == 0
ROWS = TS * BATCH
XBUFS = 2 * TPS + 1
OBUFS = 2 * TPS
CB = MXU_DIM
NA = D_MODEL // CB
NB = RG_W // CB
GB = LANES
NG = RG_W // GB
GATE_WIN = MXU_DIM
GATE_SHIFT = LANES // 2
SC_HIST = (SC_K - 1) * BATCH
RG_HIST = (RG_K - 1) * BATCH
VMEM_LIMIT_BYTES = V7X_VMEM_BYTES - 6 * 1024 * 1024

O_SCB, O_SCC, O_SCV, O_SCG = 0, D_MODEL, 2 * D_MODEL, 3 * D_MODEL
O_RGV = 4 * D_MODEL
O_RGG = O_RGV + RG_W
O_M = O_RGG + RG_W
P_IN = O_M + 2 * D_MODEL
M_PIECE = 2 * D_MODEL // (NB - 1)

WP_SC = 0
WP_RG = WP_SC + D_MODEL
WP_OUT = WP_RG + RG_W
WP_ROWS = WP_OUT + D_MODEL

WIN_CHUNK = 64
WP_CHUNK = 256
ADA_CHUNK = 128
WIN_COL_SCALES = ((0, O_SCG, 1.0), (O_SCG, O_RGV, 0.5), (O_RGV, O_RGG, 1.0), (O_RGG, P_IN, 0.5))

F32 = jnp.float32
BF16 = jnp.bfloat16


def _gate_win_start(c):
    lo = (c * GB) // RG_HEAD_DIM * RG_HEAD_DIM
    hi = -(-(c * GB + GB) // RG_HEAD_DIM) * RG_HEAD_DIM
    start = min(lo // GATE_SHIFT * GATE_SHIFT, RG_W - GATE_WIN)
    assert start <= lo and hi <= start + GATE_WIN
    return start


def _dot(a, b):
    return jnp.dot(a, b, preferred_element_type=F32)


def _half_silu_from_half(hz):
    return hz + hz * jnp.tanh(hz)


def _tile_copies(hbm_ref, buf_ref, sem_ref, tile, slot, to_hbm):
    copies = []
    for b in range(BATCH):
        hbm = hbm_ref.at[b, pl.ds(tile * TS, TS), :]
        vmem = buf_ref.at[slot, :, b, :]
        src, dst = (vmem, hbm) if to_hbm else (hbm, vmem)
        copies.append(pltpu.make_async_copy(src, dst, sem_ref.at[slot]))
    return copies


def _block_kernel(x_hbm, c_ref, bada_ref, gnorm_ref, gfinal_ref, sccw_ref, sccb_ref, rgcw_ref, rgcb_ref,
                  gba_ref, gbx_ref, lam_ref, bm_ref, wg_ref, wada_hbm, win_hbm, scw_hbm, rgw_hbm, ow_hbm,
                  out_hbm,
                  mod_scr, win_ref, wproj_ref, stage_ada, stage_in, stage_proj, w_sem,
                  xbuf, obuf, in_sem, out_sem,
                  h_tb, z0_scr, ubuf, s_scr, vbuf, hvf, vbf, vbs, a_scr, b_scr, yrg_scr, s2_scr, hstate):
    step = pl.program_id(0)
    tile0 = step * TPS
    oslot0 = lax.rem(step, 2) * TPS

    def in_copies(tile):
        if isinstance(tile, int):
            return _tile_copies(x_hbm, xbuf, in_sem, min(tile, NTILES - 1), tile % XBUFS, to_hbm=False)
        return _tile_copies(x_hbm, xbuf, in_sem, jnp.minimum(tile, NTILES - 1),
                            lax.rem(tile, XBUFS), to_hbm=False)

    def out_copies(tile, oslot):
        return _tile_copies(out_hbm, obuf, out_sem, tile, oslot, to_hbm=True)

    def norm_in(x_slot, h_ref):
        x3 = xbuf[x_slot]
        ms = jnp.mean(x3 * x3, axis=-1, keepdims=True)
        shift = mod_scr[:, 0:D_MODEL]
        scale1 = 1.0 + mod_scr[:, D_MODEL:2 * D_MODEL]
        hn = (x3 * lax.rsqrt(ms + EPS) * gnorm_ref[...]) * scale1 + shift
        h_ref[...] = hn.reshape(ROWS, D_MODEL).astype(BF16)

    def proj(h_ref, col0, width):
        return _dot(h_ref[...], win_ref[:, col0:col0 + width])

    def mixer_a_dots(h_ref, j):
        return tuple(proj(h_ref, o + j * CB, CB) for o in (O_SCB, O_SCC, O_SCV, O_SCG))

    def block0_dots_to_scratch(h_ref):
        for n, zn in enumerate(mixer_a_dots(h_ref, 0)):
            z0_scr[n] = zn

    def load_weights():
        def win_copy(i, slot):
            return pltpu.make_async_copy(win_hbm.at[pl.ds(i * WIN_CHUNK, WIN_CHUNK), :],
                                         stage_in.at[slot], w_sem.at[slot])

        win_copy(0, 0).start()

        def win_body(i, carry):
            slot = lax.rem(i, 2)

            @pl.when(i + 1 < D_MODEL // WIN_CHUNK)
            def _():
                win_copy(i + 1, 1 - slot).start()

            win_copy(i, slot).wait()
            rows = pl.ds(pl.multiple_of(i * WIN_CHUNK, WIN_CHUNK), WIN_CHUNK)
            for c0, c1, scale in WIN_COL_SCALES:
                v = stage_in[slot, :, c0:c1]
                win_ref[rows, c0:c1] = (v if scale == 1.0 else scale * v).astype(BF16)
            return carry

        lax.fori_loop(0, D_MODEL // WIN_CHUNK, win_body, 0)

        chunks = []
        for src, dst0, nrows, scale in ((scw_hbm, WP_SC, D_MODEL, 0.5), (rgw_hbm, WP_RG, RG_W, 0.5),
                                        (ow_hbm, WP_OUT, D_MODEL, 1.0)):
            chunks += [(src, r, dst0 + r, scale) for r in range(0, nrows, WP_CHUNK)]

        def proj_copy(n):
            src, r, _, _ = chunks[n]
            return pltpu.make_async_copy(src.at[r:r + WP_CHUNK, :], stage_proj.at[n % 2], w_sem.at[n % 2])

        proj_copy(0).start()
        for n, (_, _, dst, scale) in enumerate(chunks):
            if n + 1 < len(chunks):
                proj_copy(n + 1).start()
            proj_copy(n).wait()
            v = stage_proj[n % 2]
            wproj_ref[dst:dst + WP_CHUNK, :] = (v if scale == 1.0 else scale * v).astype(BF16)

        def ada_copy(n):
            return pltpu.make_async_copy(wada_hbm.at[n * ADA_CHUNK:(n + 1) * ADA_CHUNK, :],
                                         stage_ada.at[n % 2], w_sem.at[n % 2])

        c = c_ref[...]
        c_act = (c * jax.nn.sigmoid(c)).astype(BF16)
        mod = bada_ref[...]
        ada_copy(0).start()
        for n in range(D_MODEL // ADA_CHUNK):
            if n + 1 < D_MODEL // ADA_CHUNK:
                ada_copy(n + 1).start()
            ada_copy(n).wait()
            mod = mod + _dot(c_act[:, n * ADA_CHUNK:(n + 1) * ADA_CHUNK], stage_ada[n % 2].astype(BF16))
        mod_scr[...] = mod

    @pl.when(step == 0)
    def _():
        ubuf[:, 0:SC_HIST, :] = jnp.zeros((NA, SC_HIST, CB), F32)
        vbuf[0:RG_HIST, :] = jnp.zeros((RG_HIST, RG_W), F32)
        hstate[...] = jnp.zeros_like(hstate)
        for t in range(TPS + 1):
            for cp in in_copies(t):
                cp.start()
        load_weights()
        for cp in in_copies(0):
            cp.wait()
        norm_in(0, h_tb.at[0])
        block0_dots_to_scratch(h_tb.at[0])

    @pl.when(step >= 2)
    def _():
        for c in range(TPS):
            for cp in out_copies(tile0 - 2 * TPS + c, oslot0 + c):
                cp.wait()

    for c in range(TPS):
        for cp in in_copies(tile0 + TPS + 1 + c):
            cp.start()
    for c in range(TPS):
        for cp in in_copies(tile0 + 1 + c):
            cp.wait()

    def mixer_a_block(j, z):
        zb, zc, zv, hzg = z
        u = zc * zv
        ubuf[j, SC_HIST:SC_HIST + ROWS, :] = u
        cw = sccw_ref[:, j * CB:(j + 1) * CB]
        conv = sccb_ref[:, j * CB:(j + 1) * CB] + ubuf[j, 0:ROWS, :] * cw[0:1, :]
        conv = conv + ubuf[j, BATCH:BATCH + ROWS, :] * cw[1:2, :]
        conv = conv + u * cw[2:3, :]
        ubuf[j, 0:SC_HIST, :] = u[ROWS - SC_HIST:ROWS, :]
        s_scr[j] = (zb * conv * _half_silu_from_half(hzg)).astype(BF16)

    def gate_dot(c):
        start = _gate_win_start(c)
        slabs = vbf if start % LANES == 0 else vbs
        s0 = start // LANES
        lhs = jnp.concatenate([slabs[s0 + i] for i in range(GATE_WIN // LANES)], axis=1)
        return _dot(lhs, wg_ref[c])

    def gate_dots(h_cur, k):
        pieces = [gate_dot(c) for c in range(k * CB // GB, (k + 1) * CB // GB)]
        hg_a = jnp.concatenate([p[:, 0:GB] for p in pieces], axis=1)
        hg_x = jnp.concatenate([p[:, GB:2 * GB] for p in pieces], axis=1)
        return (hg_a, hg_x), proj(h_cur, O_RGG + k * CB, CB)

    def mixer_b_block(k, hg, hzg):
        hg_a, hg_x = hg
        cols = slice(k * CB, (k + 1) * CB)
        t_r = jnp.tanh(hg_a + 0.5 * gba_ref[:, cols])
        t_i = jnp.tanh(hg_x + 0.5 * gbx_ref[:, cols])
        half_k = (-0.5 * LRU_C) * jax.nn.softplus(-lam_ref[:, cols])
        log_a = half_k + half_k * t_r
        a = jnp.exp(log_a)
        a_scr[...] = a
        one_minus_a2 = jnp.tanh(log_a) * (-1.0 - a * a)
        root = jnp.where(one_minus_a2 > 0.0, one_minus_a2 * lax.rsqrt(one_minus_a2), 0.0)
        hvk = hvf[k]
        b_scr[...] = root * (hvk + hvk * t_i)
        h = hstate[k]
        for t in range(TS):
            rows = slice(t * BATCH, (t + 1) * BATCH)
            h = a_scr[rows, :] * h + b_scr[rows, :]
            yrg_scr[rows, :] = h
        hstate[k] = h
        s2_scr[k] = (yrg_scr[...] * _half_silu_from_half(hzg)).astype(BF16)

    def process_tile(c):
        tile = tile0 + c
        xs_cur = lax.rem(tile, XBUFS)
        xs_next = lax.rem(tile + 1, XBUFS)
        h_cur = h_tb.at[c % 2]
        h_next = h_tb.at[(c + 1) % 2]

        z = tuple(z0_scr[n] for n in range(4))
        for j in range(NA):
            z_next = mixer_a_dots(h_cur, j + 1) if j + 1 < NA else None
            if j == 0:
                norm_in(xs_next, h_next)
            mixer_a_block(j, z)
            z = z_next

        vz = proj(h_cur, O_RGV, RG_W)
        m_pieces = [proj(h_cur, O_M, M_PIECE)]
        s_all = jnp.concatenate([s_scr[j] for j in range(NA)], axis=1)
        ya_half = _dot(s_all, wproj_ref[WP_SC:WP_SC + D_MODEL, :])
        vbuf[RG_HIST:RG_HIST + ROWS, :] = vz
        rcw = 0.5 * rgcw_ref[...]
        hv = 0.5 * rgcb_ref[...] + vbuf[0:ROWS, :] * rcw[0:1, :]
        hv = hv + vbuf[BATCH:BATCH + ROWS, :] * rcw[1:2, :]
        hv = hv + vbuf[2 * BATCH:2 * BATCH + ROWS, :] * rcw[2:3, :]
        hv = hv + vz * rcw[3:4, :]
        vbuf[0:RG_HIST, :] = vz[ROWS - RG_HIST:ROWS, :]
        for cidx in range(NG):
            vbf[cidx] = hv[:, cidx * LANES:(cidx + 1) * LANES].astype(BF16)
        for cidx in range(NG - 1):
            lo = GATE_SHIFT + cidx * LANES
            vbs[cidx] = hv[:, lo:lo + LANES].astype(BF16)
        for k in range(NB):
            hvf[k] = hv[:, k * CB:(k + 1) * CB]

        hg, hzg = gate_dots(h_cur, 0)
        for k in range(NB):
            hg_next, hzg_next = gate_dots(h_cur, k + 1) if k + 1 < NB else (None, None)
            if 1 <= k < NB - 1:
                m_pieces.append(proj(h_cur, O_M + k * M_PIECE, M_PIECE))
            mixer_b_block(k, hg, hzg)
            hg, hzg = hg_next, hzg_next
        m = jnp.concatenate(m_pieces, axis=1)

        s2_all = jnp.concatenate([s2_scr[k] for k in range(NB)], axis=1)
        yb_half = _dot(s2_all, wproj_ref[WP_RG:WP_RG + RG_W, :])

        t_a = jnp.tanh(m[:, 0:D_MODEL] + 0.5 * bm_ref[0:1, :])
        t_b = jnp.tanh(m[:, D_MODEL:2 * D_MODEL] + 0.5 * bm_ref[1:2, :])
        merged = ((ya_half + ya_half * t_a) + (yb_half + yb_half * t_b)).astype(BF16)
        y3 = _dot(merged, wproj_ref[WP_OUT:WP_OUT + D_MODEL, :]).reshape(TS, BATCH, D_MODEL)
        block0_dots_to_scratch(h_next)
        xo = xbuf[xs_cur] + mod_scr[:, 2 * D_MODEL:3 * D_MODEL] * y3
        ms = jnp.mean(xo * xo, axis=-1, keepdims=True)
        obuf[oslot0 + c] = xo * lax.rsqrt(ms + EPS) * gfinal_ref[...]

    for c in range(TPS):
        process_tile(c)

    for c in range(TPS):
        for cp in out_copies(tile0 + c, oslot0 + c):
            cp.start()

    @pl.when(step == NSTEPS - 1)
    def _():
        for c in range(TPS):
            for cp in in_copies(tile0 + TPS + 1 + c):
                cp.wait()
            for cp in out_copies(tile0 - TPS + c, (TPS - oslot0) + c):
                cp.wait()
            for cp in out_copies(tile0 + c, oslot0 + c):
                cp.wait()


def _const_spec(shape):
    nd = len(shape)
    return pl.BlockSpec(shape, lambda i, _nd=nd: (0,) * _nd, pipeline_mode=pl.Buffered(1))


def _gate_weights(w_a, w_x):
    starts = [_gate_win_start(c) for c in range(NG)]
    row_ch = jnp.array(starts)[:, None] + jnp.arange(GATE_WIN)[None, :]
    col_ch = jnp.arange(NG)[:, None] * GB + jnp.arange(GB)[None, :]
    same_head = (row_ch // RG_HEAD_DIM)[:, :, None] == (col_ch // RG_HEAD_DIM)[:, None, :]
    spread = (jnp.arange(RG_HEAD_DIM)[None, :, None] == (col_ch % RG_HEAD_DIM)[:, None, :]).astype(BF16)

    def expand(w):
        rows = w.reshape(RG_W, RG_HEAD_DIM).astype(BF16)
        win = jnp.stack([rows[s:s + GATE_WIN] for s in starts])
        dense = jnp.einsum("kre,kec->krc", win, spread, preferred_element_type=F32)
        return jnp.where(same_head, dense, 0.0).astype(BF16)

    return jnp.concatenate([expand(w_a), expand(w_x)], axis=2)


def kernel(x, c, w_ada, b_ada, g_norm, w_in, sc_conv_w, sc_conv_b, sc_w_out, rg_conv_w, rg_conv_b,
           rg_w_a, rg_b_a, rg_w_x, rg_b_x, rg_lambda, rg_w_out, b_merge, w_out, g_final):
    assert x.shape == (BATCH, SEQ, D_MODEL) and w_ada.shape[0] == 1
    l = 0
    wg = _gate_weights(rg_w_a[l], rg_w_x[l])

    vmem_operands = (
        c, b_ada[l].reshape(1, 3 * D_MODEL), g_norm[l].reshape(1, D_MODEL), g_final.reshape(1, D_MODEL),
        sc_conv_w[l], sc_conv_b[l].reshape(1, D_MODEL), rg_conv_w[l], rg_conv_b[l].reshape(1, RG_W),
        rg_b_a[l].reshape(1, RG_W), rg_b_x[l].reshape(1, RG_W), rg_lambda[l].reshape(1, RG_W), b_merge[l],
        wg)
    hbm_weights = (w_ada[l], w_in[l], sc_w_out[l], rg_w_out[l], w_out[l])
    operands = (x,) + vmem_operands + hbm_weights
    any_spec = pl.BlockSpec(memory_space=pl.ANY)
    in_specs = [any_spec] + [_const_spec(op.shape) for op in vmem_operands] + [any_spec] * len(hbm_weights)
    scratch = [
        pltpu.VMEM((BATCH, 3 * D_MODEL), F32),
        pltpu.VMEM((D_MODEL, P_IN), BF16),
        pltpu.VMEM((WP_ROWS, D_MODEL), BF16),
        pltpu.VMEM((2, ADA_CHUNK, 3 * D_MODEL), F32),
        pltpu.VMEM((2, WIN_CHUNK, P_IN), F32),
        pltpu.VMEM((2, WP_CHUNK, D_MODEL), F32),
        pltpu.SemaphoreType.DMA((2,)),
        pltpu.VMEM((XBUFS, TS, BATCH, D_MODEL), F32),
        pltpu.VMEM((OBUFS, TS, BATCH, D_MODEL), F32),
        pltpu.SemaphoreType.DMA((XBUFS,)),
        pltpu.SemaphoreType.DMA((OBUFS,)),
        pltpu.VMEM((2, ROWS, D_MODEL), BF16),
        pltpu.VMEM((4, ROWS, CB), F32),
        pltpu.VMEM((NA, SC_HIST + ROWS, CB), F32),
        pltpu.VMEM((NA, ROWS, CB), BF16),
        pltpu.VMEM((RG_HIST + ROWS, RG_W), F32),
        pltpu.VMEM((NB, ROWS, CB), F32),
        pltpu.VMEM((NG, ROWS, LANES), BF16),
        pltpu.VMEM((NG - 1, ROWS, LANES), BF16),
        pltpu.VMEM((ROWS, CB), F32),
        pltpu.VMEM((ROWS, CB), F32),
        pltpu.VMEM((ROWS, CB), F32),
        pltpu.VMEM((NB, ROWS, CB), BF16),
        pltpu.VMEM((NB, BATCH, CB), F32),
    ]
    return pl.pallas_call(
        _block_kernel,
        grid=(NSTEPS,),
        in_specs=in_specs,
        out_specs=any_spec,
        out_shape=jax.ShapeDtypeStruct((BATCH, SEQ, D_MODEL), F32),
        scratch_shapes=scratch,
        compiler_params=pltpu.CompilerParams(
            dimension_semantics=("arbitrary",), vmem_limit_bytes=VMEM_LIMIT_BYTES),
        name="hybrid_block",
    )(*operands)
```

```python
import jax
import jax.numpy as jnp
from jax import lax
from jax.experimental import pallas as pl
from jax.experimental.pallas import tpu as pltpu

D_MODEL = 1024
BATCH = 16
SEQ = 2048
SC_K = 3
RG_W = 1280
RG_HEADS = 16
RG_HEAD_DIM = RG_W // RG_HEADS
RG_K = 4
LRU_C = 8.0
EPS = 1e-6

LANES = 128
MXU_DIM = 256
V7X_VMEM_BYTES = 64 * 1024 * 1024
TS = 16
TPS = 2
NTILES = SEQ // TS
NSTEPS = NTILES // TPS
assert TPS % 2 == 0
ROWS = TS * BATCH
XBUFS = 2 * TPS + 1
OBUFS = 2 * TPS
CB = MXU_DIM
NA = D_MODEL // CB
NB = RG_W // CB
GB = LANES
NG = RG_W // GB
GATE_WIN = MXU_DIM
GATE_SHIFT = LANES // 2
SC_HIST = (SC_K - 1) * BATCH
RG_HIST = (RG_K - 1) * BATCH
VMEM_LIMIT_BYTES = V7X_VMEM_BYTES - 6 * 1024 * 1024

O_SCB, O_SCC, O_SCV, O_SCG = 0, D_MODEL, 2 * D_MODEL, 3 * D_MODEL
O_RGV = 4 * D_MODEL
O_RGG = O_RGV + RG_W
O_M = O_RGG + RG_W
P_IN = O_M + 2 * D_MODEL
M_PIECE = 2 * D_MODEL // (NB - 1)

WP_SC = 0
WP_RG = WP_SC + D_MODEL
WP_OUT = WP_RG + RG_W
WP_ROWS = WP_OUT + D_MODEL

WIN_CHUNK = 64
WP_CHUNK = 256
WIN_COL_SCALES = ((0, O_SCG, 1.0), (O_SCG, O_RGV, 0.5), (O_RGV, O_RGG, 1.0), (O_RGG, P_IN, 0.5))

F32 = jnp.float32
BF16 = jnp.bfloat16


def _gate_win_start(c):
    lo = (c * GB) // RG_HEAD_DIM * RG_HEAD_DIM
    hi = -(-(c * GB + GB) // RG_HEAD_DIM) * RG_HEAD_DIM
    start = min(lo // GATE_SHIFT * GATE_SHIFT, RG_W - GATE_WIN)
    assert start <= lo and hi <= start + GATE_WIN
    return start


def _adaln_kernel(c_ref, w_ref, b_ref, o_ref):
    c = c_ref[...]
    c_act = (c * jax.nn.sigmoid(c)).astype(BF16)
    o_ref[...] = jnp.dot(c_act, w_ref[...].astype(BF16), preferred_element_type=F32) + b_ref[...]


def _adaln(c, w_ada, b_ada):
    n = w_ada.shape[1]
    nblk = n // D_MODEL
    return pl.pallas_call(
        _adaln_kernel,
        grid=(nblk,),
        in_specs=[
            pl.BlockSpec((BATCH, D_MODEL), lambda i: (0, 0)),
            pl.BlockSpec((D_MODEL, D_MODEL), lambda i: (0, i)),
            pl.BlockSpec((1, D_MODEL), lambda i: (0, i)),
        ],
        out_specs=pl.BlockSpec((BATCH, D_MODEL), lambda i: (0, i)),
        out_shape=jax.ShapeDtypeStruct((BATCH, n), F32),
        name="adaln",
    )(c, w_ada, b_ada.reshape(1, n))


def _dot(a, b):
    return jnp.dot(a, b, preferred_element_type=F32)


def _half_silu_from_half(hz):
    return hz + hz * jnp.tanh(hz)


def _tile_copies(hbm_ref, buf_ref, sem_ref, tile, slot, to_hbm):
    copies = []
    for b in range(BATCH):
        hbm = hbm_ref.at[b, pl.ds(tile * TS, TS), :]
        vmem = buf_ref.at[slot, :, b, :]
        src, dst = (vmem, hbm) if to_hbm else (hbm, vmem)
        copies.append(pltpu.make_async_copy(src, dst, sem_ref.at[slot]))
    return copies


def _block_kernel(x_hbm, mod_ref, gnorm_ref, gfinal_ref, sccw_ref, sccb_ref, rgcw_ref, rgcb_ref,
                  gba_ref, gbx_ref, lam_ref, bm_ref, wg_ref, win_hbm, scw_hbm, rgw_hbm, ow_hbm,
                  out_hbm,
                  win_ref, wproj_ref, stage_in, stage_proj, w_sem,
                  xbuf, obuf, in_sem, out_sem,
                  h_tb, z0_scr, ubuf, s_scr, vbuf, hvf, vbf, vbs, s2_scr, hstate):
    step = pl.program_id(0)
    tile0 = step * TPS
    oslot0 = lax.rem(step, 2) * TPS

    def in_copies(tile):
        if isinstance(tile, int):
            return _tile_copies(x_hbm, xbuf, in_sem, min(tile, NTILES - 1), tile % XBUFS, to_hbm=False)
        return _tile_copies(x_hbm, xbuf, in_sem, jnp.minimum(tile, NTILES - 1),
                            lax.rem(tile, XBUFS), to_hbm=False)

    def out_copies(tile, oslot):
        return _tile_copies(out_hbm, obuf, out_sem, tile, oslot, to_hbm=True)

    shift = mod_ref[:, 0:D_MODEL]
    scale1 = 1.0 + mod_ref[:, D_MODEL:2 * D_MODEL]
    gate = mod_ref[:, 2 * D_MODEL:3 * D_MODEL]

    def norm_in(x_slot, h_ref):
        x3 = xbuf[x_slot]
        ms = jnp.mean(x3 * x3, axis=-1, keepdims=True)
        hn = (x3 * lax.rsqrt(ms + EPS) * gnorm_ref[...]) * scale1 + shift
        h_ref[...] = hn.reshape(ROWS, D_MODEL).astype(BF16)

    def proj(h_ref, col0, width):
        return _dot(h_ref[...], win_ref[:, col0:col0 + width])

    def mixer_a_dots(h_ref, j):
        return tuple(proj(h_ref, o + j * CB, CB) for o in (O_SCB, O_SCC, O_SCV, O_SCG))

    def block0_dots_to_scratch(h_ref):
        for n, zn in enumerate(mixer_a_dots(h_ref, 0)):
            z0_scr[n] = zn

    def load_weights():
        def win_copy(i, slot):
            return pltpu.make_async_copy(win_hbm.at[pl.ds(i * WIN_CHUNK, WIN_CHUNK), :],
                                         stage_in.at[slot], w_sem.at[slot])

        win_copy(0, 0).start()

        def win_body(i, carry):
            slot = lax.rem(i, 2)

            @pl.when(i + 1 < D_MODEL // WIN_CHUNK)
            def _():
                win_copy(i + 1, 1 - slot).start()

            win_copy(i, slot).wait()
            rows = pl.ds(pl.multiple_of(i * WIN_CHUNK, WIN_CHUNK), WIN_CHUNK)
            for c0, c1, scale in WIN_COL_SCALES:
                v = stage_in[slot, :, c0:c1]
                win_ref[rows, c0:c1] = (v if scale == 1.0 else scale * v).astype(BF16)
            return carry

        lax.fori_loop(0, D_MODEL // WIN_CHUNK, win_body, 0)

        chunks = []
        for src, dst0, nrows, scale in ((scw_hbm, WP_SC, D_MODEL, 0.5), (rgw_hbm, WP_RG, RG_W, 0.5),
                                        (ow_hbm, WP_OUT, D_MODEL, 1.0)):
            chunks += [(src, r, dst0 + r, scale) for r in range(0, nrows, WP_CHUNK)]

        def proj_copy(n):
            src, r, _, _ = chunks[n]
            return pltpu.make_async_copy(src.at[r:r + WP_CHUNK, :], stage_proj.at[n % 2], w_sem.at[n % 2])

        proj_copy(0).start()
        for n, (_, _, dst, scale) in enumerate(chunks):
            if n + 1 < len(chunks):
                proj_copy(n + 1).start()
            proj_copy(n).wait()
            v = stage_proj[n % 2]
            wproj_ref[dst:dst + WP_CHUNK, :] = (v if scale == 1.0 else scale * v).astype(BF16)

    @pl.when(step == 0)
    def _():
        ubuf[:, 0:SC_HIST, :] = jnp.zeros((NA, SC_HIST, CB), F32)
        vbuf[0:RG_HIST, :] = jnp.zeros((RG_HIST, RG_W), F32)
        hstate[...] = jnp.zeros_like(hstate)
        for t in range(TPS + 1):
            for cp in in_copies(t):
                cp.start()
        load_weights()
        for cp in in_copies(0):
            cp.wait()
        norm_in(0, h_tb.at[0])
        block0_dots_to_scratch(h_tb.at[0])

    @pl.when(step >= 2)
    def _():
        for c in range(TPS):
            for cp in out_copies(tile0 - 2 * TPS + c, oslot0 + c):
                cp.wait()

    for c in range(TPS):
        for cp in in_copies(tile0 + TPS + 1 + c):
            cp.start()
    for c in range(TPS):
        for cp in in_copies(tile0 + 1 + c):
            cp.wait()

    def mixer_a_block(j, z):
        zb, zc, zv, hzg = z
        u = zc * zv
        ubuf[j, SC_HIST:SC_HIST + ROWS, :] = u
        cw = sccw_ref[:, j * CB:(j + 1) * CB]
        conv = sccb_ref[:, j * CB:(j + 1) * CB] + ubuf[j, 0:ROWS, :] * cw[0:1, :]
        conv = conv + ubuf[j, BATCH:BATCH + ROWS, :] * cw[1:2, :]
        conv = conv + u * cw[2:3, :]
        ubuf[j, 0:SC_HIST, :] = u[ROWS - SC_HIST:ROWS, :]
        s_scr[j] = (zb * conv * _half_silu_from_half(hzg)).astype(BF16)

    def gate_dot(c):
        start = _gate_win_start(c)
        slabs = vbf if start % LANES == 0 else vbs
        s0 = start // LANES
        lhs = jnp.concatenate([slabs[s0 + i] for i in range(GATE_WIN // LANES)], axis=1)
        return _dot(lhs, wg_ref[c])

    def gate_dots(h_cur, k):
        pieces = [gate_dot(c) for c in range(k * CB // GB, (k + 1) * CB // GB)]
        hg_a = jnp.concatenate([p[:, 0:GB] for p in pieces], axis=1)
        hg_x = jnp.concatenate([p[:, GB:2 * GB] for p in pieces], axis=1)
        return (hg_a, hg_x), proj(h_cur, O_RGG + k * CB, CB)

    def mixer_b_block(k, hg, hzg):
        hg_a, hg_x = hg
        cols = slice(k * CB, (k + 1) * CB)
        half_ba = 0.5 * gba_ref[:, cols]
        half_bx = 0.5 * gbx_ref[:, cols]
        half_k = (-0.5 * LRU_C) * jax.nn.softplus(-lam_ref[:, cols])
        h = hstate[k]
        for t in range(TS):
            rows = slice(t * BATCH, (t + 1) * BATCH)
            t_r = jnp.tanh(hg_a[rows, :] + half_ba)
            t_i = jnp.tanh(hg_x[rows, :] + half_bx)
            log_a = half_k + half_k * t_r
            a = jnp.exp(log_a)
            one_minus_a2 = jnp.tanh(log_a) * (-1.0 - a * a)
            root = jnp.where(one_minus_a2 > 0.0, one_minus_a2 * lax.rsqrt(one_minus_a2), 0.0)
            hvk = hvf[k, rows, :]
            h = a * h + root * (hvk + hvk * t_i)
            s2_scr[k, rows, :] = (h * _half_silu_from_half(hzg[rows, :])).astype(BF16)
        hstate[k] = h

    def process_tile(c):
        tile = tile0 + c
        xs_cur = lax.rem(tile, XBUFS)
        xs_next = lax.rem(tile + 1, XBUFS)
        h_cur = h_tb.at[c % 2]
        h_next = h_tb.at[(c + 1) % 2]

        z = tuple(z0_scr[n] for n in range(4))
        for j in range(NA):
            z_next = mixer_a_dots(h_cur, j + 1) if j + 1 < NA else None
            if j == 0:
                norm_in(xs_next, h_next)
            mixer_a_block(j, z)
            z = z_next

        vz = proj(h_cur, O_RGV, RG_W)
        m_pieces = [proj(h_cur, O_M, M_PIECE)]
        s_all = jnp.concatenate([s_scr[j] for j in range(NA)], axis=1)
        ya_half = _dot(s_all, wproj_ref[WP_SC:WP_SC + D_MODEL, :])
        vbuf[RG_HIST:RG_HIST + ROWS, :] = vz
        rcw = 0.5 * rgcw_ref[...]
        hv = 0.5 * rgcb_ref[...] + vbuf[0:ROWS, :] * rcw[0:1, :]
        hv = hv + vbuf[BATCH:BATCH + ROWS, :] * rcw[1:2, :]
        hv = hv + vbuf[2 * BATCH:2 * BATCH + ROWS, :] * rcw[2:3, :]
        hv = hv + vz * rcw[3:4, :]
        vbuf[0:RG_HIST, :] = vz[ROWS - RG_HIST:ROWS, :]
        for cidx in range(NG):
            vbf[cidx] = hv[:, cidx * LANES:(cidx + 1) * LANES].astype(BF16)
        for cidx in range(NG - 1):
            lo = GATE_SHIFT + cidx * LANES
            vbs[cidx] = hv[:, lo:lo + LANES].astype(BF16)
        for k in range(NB):
            hvf[k] = hv[:, k * CB:(k + 1) * CB]

        hg, hzg = gate_dots(h_cur, 0)
        for k in range(NB):
            hg_next, hzg_next = gate_dots(h_cur, k + 1) if k + 1 < NB else (None, None)
            if 1 <= k < NB - 1:
                m_pieces.append(proj(h_cur, O_M + k * M_PIECE, M_PIECE))
            mixer_b_block(k, hg, hzg)
            hg, hzg = hg_next, hzg_next
        m = jnp.concatenate(m_pieces, axis=1)

        s2_all = jnp.concatenate([s2_scr[k] for k in range(NB)], axis=1)
        yb_half = _dot(s2_all, wproj_ref[WP_RG:WP_RG + RG_W, :])

        t_a = jnp.tanh(m[:, 0:D_MODEL] + 0.5 * bm_ref[0:1, :])
        t_b = jnp.tanh(m[:, D_MODEL:2 * D_MODEL] + 0.5 * bm_ref[1:2, :])
        merged = ((ya_half + ya_half * t_a) + (yb_half + yb_half * t_b)).astype(BF16)
        y3 = _dot(merged, wproj_ref[WP_OUT:WP_OUT + D_MODEL, :]).reshape(TS, BATCH, D_MODEL)
        block0_dots_to_scratch(h_next)
        xo = xbuf[xs_cur] + gate * y3
        ms = jnp.mean(xo * xo, axis=-1, keepdims=True)
        obuf[oslot0 + c] = xo * lax.rsqrt(ms + EPS) * gfinal_ref[...]

    for c in range(TPS):
        process_tile(c)

    for c in range(TPS):
        for cp in out_copies(tile0 + c, oslot0 + c):
            cp.start()

    @pl.when(step == NSTEPS - 1)
    def _():
        for c in range(TPS):
            for cp in in_copies(tile0 + TPS + 1 + c):
                cp.wait()
            for cp in out_copies(tile0 - TPS + c, (TPS - oslot0) + c):
                cp.wait()
            for cp in out_copies(tile0 + c, oslot0 + c):
                cp.wait()


def _const_spec(shape):
    nd = len(shape)
    return pl.BlockSpec(shape, lambda i, _nd=nd: (0,) * _nd, pipeline_mode=pl.Buffered(1))


def _gate_weights(w_a, w_x):
    starts = [_gate_win_start(c) for c in range(NG)]
    row_ch = jnp.array(starts)[:, None] + jnp.arange(GATE_WIN)[None, :]
    col_ch = jnp.arange(NG)[:, None] * GB + jnp.arange(GB)[None, :]
    same_head = (row_ch // RG_HEAD_DIM)[:, :, None] == (col_ch // RG_HEAD_DIM)[:, None, :]
    spread = (jnp.arange(RG_HEAD_DIM)[None, :, None] == (col_ch % RG_HEAD_DIM)[:, None, :]).astype(BF16)

    def expand(w):
        rows = w.reshape(RG_W, RG_HEAD_DIM).astype(BF16)
        win = jnp.stack([rows[s:s + GATE_WIN] for s in starts])
        dense = jnp.einsum("kre,kec->krc", win, spread, preferred_element_type=F32)
        return jnp.where(same_head, dense, 0.0).astype(BF16)

    return jnp.concatenate([expand(w_a), expand(w_x)], axis=2)


def kernel(x, c, w_ada, b_ada, g_norm, w_in, sc_conv_w, sc_conv_b, sc_w_out, rg_conv_w, rg_conv_b,
           rg_w_a, rg_b_a, rg_w_x, rg_b_x, rg_lambda, rg_w_out, b_merge, w_out, g_final):
    assert x.shape == (BATCH, SEQ, D_MODEL) and w_ada.shape[0] == 1
    l = 0
    mod = _adaln(c, w_ada[l], b_ada[l])

    wg = _gate_weights(rg_w_a[l], rg_w_x[l])

    vmem_operands = (
        mod, g_norm[l].reshape(1, D_MODEL), g_final.reshape(1, D_MODEL),
        sc_conv_w[l], sc_conv_b[l].reshape(1, D_MODEL), rg_conv_w[l], rg_conv_b[l].reshape(1, RG_W),
        rg_b_a[l].reshape(1, RG_W), rg_b_x[l].reshape(1, RG_W), rg_lambda[l].reshape(1, RG_W), b_merge[l],
        wg)
    hbm_weights = (w_in[l], sc_w_out[l], rg_w_out[l], w_out[l])
    operands = (x,) + vmem_operands + hbm_weights
    any_spec = pl.BlockSpec(memory_space=pl.ANY)
    in_specs = [any_spec] + [_const_spec(op.shape) for op in vmem_operands] + [any_spec] * len(hbm_weights)
    scratch = [
        pltpu.VMEM((D_MODEL, P_IN), BF16),
        pltpu.VMEM((WP_ROWS, D_MODEL), BF16),
        pltpu.VMEM((2, WIN_CHUNK, P_IN), F32),
        pltpu.VMEM((2, WP_CHUNK, D_MODEL), F32),
        pltpu.SemaphoreType.DMA((2,)),
        pltpu.VMEM((XBUFS, TS, BATCH, D_MODEL), F32),
        pltpu.VMEM((OBUFS, TS, BATCH, D_MODEL), F32),
        pltpu.SemaphoreType.DMA((XBUFS,)),
        pltpu.SemaphoreType.DMA((OBUFS,)),
        pltpu.VMEM((2, ROWS, D_MODEL), BF16),
        pltpu.VMEM((4, ROWS, CB), F32),
        pltpu.VMEM((NA, SC_HIST + ROWS, CB), F32),
        pltpu.VMEM((NA, ROWS, CB), BF16),
        pltpu.VMEM((RG_HIST + ROWS, RG_W), F32),
        pltpu.VMEM((NB, ROWS, CB), F32),
        pltpu.VMEM((NG, ROWS, LANES), BF16),
        pltpu.VMEM((NG - 1, ROWS, LANES), BF16),
        pltpu.VMEM((NB, ROWS, CB), BF16),
        pltpu.VMEM((NB, BATCH, CB), F32),
    ]
    return pl.pallas_call(
        _block_kernel,
        grid=(NSTEPS,),
        in_specs=in_specs,
        out_specs=any_spec,
        out_shape=jax.ShapeDtypeStruct((BATCH, SEQ, D_MODEL), F32),
        scratch_shapes=scratch,
        compiler_params=pltpu.CompilerParams(
            dimension_semantics=("arbitrary",), vmem_limit_bytes=VMEM_LIMIT_BYTES),
        name="hybrid_block",
    )(*operands)
```

```python
import jax
import jax.numpy as jnp
from jax import lax
from jax.experimental import pallas as pl
from jax.experimental.pallas import tpu as pltpu

D_MODEL = 1024
BATCH = 16
SEQ = 2048
SC_K = 3
RG_W = 1280
RG_HEADS = 16
RG_HEAD_DIM = RG_W // RG_HEADS
RG_K = 4
LRU_C = 8.0
EPS = 1e-6

LANES = 128
MXU_DIM = 256
V7X_VMEM_BYTES = 64 * 1024 * 1024
TS = 16
TPS = 2
NTILES = SEQ // TS
NSTEPS = NTILES // TPS
assert TPS % 2 == 0
ROWS = TS * BATCH
GROUP_T = TPS * TS
XBUFS = 3
OBUFS = 2
CB = MXU_DIM
NA = D_MODEL // CB
NB = RG_W // CB
GB = LANES
NG = RG_W // GB
GATE_WIN = MXU_DIM
GATE_SHIFT = LANES // 2
SC_HIST = (SC_K - 1) * BATCH
RG_HIST = (RG_K - 1) * BATCH
VMEM_LIMIT_BYTES = V7X_VMEM_BYTES - 6 * 1024 * 1024

O_SCB, O_SCC, O_SCV, O_SCG = 0, D_MODEL, 2 * D_MODEL, 3 * D_MODEL
O_RGV = 4 * D_MODEL
O_RGG = O_RGV + RG_W
O_M = O_RGG + RG_W
P_IN = O_M + 2 * D_MODEL
M_PIECE = 2 * D_MODEL // (NB - 1)

WP_SC = 0
WP_RG = WP_SC + D_MODEL
WP_OUT = WP_RG + RG_W
WP_ROWS = WP_OUT + D_MODEL

WIN_CHUNK = 64
WP_CHUNK = 256
WIN_COL_SCALES = ((0, O_SCG, 1.0), (O_SCG, O_RGV, 0.5), (O_RGV, O_RGG, 1.0), (O_RGG, P_IN, 0.5))

F32 = jnp.float32
BF16 = jnp.bfloat16


def _gate_win_start(c):
    lo = (c * GB) // RG_HEAD_DIM * RG_HEAD_DIM
    hi = -(-(c * GB + GB) // RG_HEAD_DIM) * RG_HEAD_DIM
    start = min(lo // GATE_SHIFT * GATE_SHIFT, RG_W - GATE_WIN)
    assert start <= lo and hi <= start + GATE_WIN
    return start


def _adaln_kernel(c_ref, w_ref, b_ref, o_ref):
    c = c_ref[...]
    c_act = (c * jax.nn.sigmoid(c)).astype(BF16)
    o_ref[...] = jnp.dot(c_act, w_ref[...].astype(BF16), preferred_element_type=F32) + b_ref[...]


def _adaln(c, w_ada, b_ada):
    n = w_ada.shape[1]
    nblk = n // D_MODEL
    return pl.pallas_call(
        _adaln_kernel,
        grid=(nblk,),
        in_specs=[
            pl.BlockSpec((BATCH, D_MODEL), lambda i: (0, 0)),
            pl.BlockSpec((D_MODEL, D_MODEL), lambda i: (0, i)),
            pl.BlockSpec((1, D_MODEL), lambda i: (0, i)),
        ],
        out_specs=pl.BlockSpec((BATCH, D_MODEL), lambda i: (0, i)),
        out_shape=jax.ShapeDtypeStruct((BATCH, n), F32),
        name="adaln",
    )(c, w_ada, b_ada.reshape(1, n))


def _dot(a, b):
    return jnp.dot(a, b, preferred_element_type=F32)


def _half_silu_from_half(hz):
    return hz + hz * jnp.tanh(hz)


def _group_copies(hbm_ref, buf_ref, sem_ref, group, slot, to_hbm):
    copies = []
    for b in range(BATCH):
        hbm = hbm_ref.at[b, pl.ds(group * GROUP_T, GROUP_T), :]
        vmem = buf_ref.at[slot, :, b, :]
        src, dst = (vmem, hbm) if to_hbm else (hbm, vmem)
        copies.append(pltpu.make_async_copy(src, dst, sem_ref.at[slot]))
    return copies


def _block_kernel(x_hbm, mod_ref, gnorm_ref, gfinal_ref, sccw_ref, sccb_ref, rgcw_ref, rgcb_ref,
                  gba_ref, gbx_ref, lam_ref, bm_ref, wg_ref, win_hbm, scw_hbm, rgw_hbm, ow_hbm,
                  out_hbm,
                  win_ref, wproj_ref, stage_in, stage_proj, w_sem,
                  xbuf, obuf, in_sem, out_sem,
                  h_tb, z0_scr, ubuf, s_scr, vbuf, hvf, vbf, vbs, a_scr, b_scr, yrg_scr, s2_scr, hstate):
    step = pl.program_id(0)
    xg_cur = lax.rem(step, XBUFS)
    xg_next = lax.rem(step + 1, XBUFS)
    og_cur = lax.rem(step, OBUFS)

    def in_copies(group):
        if isinstance(group, int):
            return _group_copies(x_hbm, xbuf, in_sem, min(group, NSTEPS - 1), group % XBUFS, to_hbm=False)
        return _group_copies(x_hbm, xbuf, in_sem, jnp.minimum(group, NSTEPS - 1),
                             lax.rem(group, XBUFS), to_hbm=False)

    def out_copies(group):
        return _group_copies(out_hbm, obuf, out_sem, group, lax.rem(group, OBUFS), to_hbm=True)

    shift = mod_ref[:, 0:D_MODEL]
    scale1 = 1.0 + mod_ref[:, D_MODEL:2 * D_MODEL]
    gate = mod_ref[:, 2 * D_MODEL:3 * D_MODEL]

    def norm_in(x_slot, c, h_ref):
        x3 = xbuf[x_slot, c * TS:(c + 1) * TS]
        ms = jnp.mean(x3 * x3, axis=-1, keepdims=True)
        hn = (x3 * lax.rsqrt(ms + EPS) * gnorm_ref[...]) * scale1 + shift
        h_ref[...] = hn.reshape(ROWS, D_MODEL).astype(BF16)

    def proj(h_ref, col0, width):
        return _dot(h_ref[...], win_ref[:, col0:col0 + width])

    def mixer_a_dots(h_ref, j):
        return tuple(proj(h_ref, o + j * CB, CB) for o in (O_SCB, O_SCC, O_SCV, O_SCG))

    def block0_dots_to_scratch(h_ref):
        for n, zn in enumerate(mixer_a_dots(h_ref, 0)):
            z0_scr[n] = zn

    def load_weights():
        def win_copy(i, slot):
            return pltpu.make_async_copy(win_hbm.at[pl.ds(i * WIN_CHUNK, WIN_CHUNK), :],
                                         stage_in.at[slot], w_sem.at[slot])

        win_copy(0, 0).start()

        def win_body(i, carry):
            slot = lax.rem(i, 2)

            @pl.when(i + 1 < D_MODEL // WIN_CHUNK)
            def _():
                win_copy(i + 1, 1 - slot).start()

            win_copy(i, slot).wait()
            rows = pl.ds(pl.multiple_of(i * WIN_CHUNK, WIN_CHUNK), WIN_CHUNK)
            for c0, c1, scale in WIN_COL_SCALES:
                v = stage_in[slot, :, c0:c1]
                win_ref[rows, c0:c1] = (v if scale == 1.0 else scale * v).astype(BF16)
            return carry

        lax.fori_loop(0, D_MODEL // WIN_CHUNK, win_body, 0)

        chunks = []
        for src, dst0, nrows, scale in ((scw_hbm, WP_SC, D_MODEL, 0.5), (rgw_hbm, WP_RG, RG_W, 0.5),
                                        (ow_hbm, WP_OUT, D_MODEL, 1.0)):
            chunks += [(src, r, dst0 + r, scale) for r in range(0, nrows, WP_CHUNK)]

        def proj_copy(n):
            src, r, _, _ = chunks[n]
            return pltpu.make_async_copy(src.at[r:r + WP_CHUNK, :], stage_proj.at[n % 2], w_sem.at[n % 2])

        proj_copy(0).start()
        for n, (_, _, dst, scale) in enumerate(chunks):
            if n + 1 < len(chunks):
                proj_copy(n + 1).start()
            proj_copy(n).wait()
            v = stage_proj[n % 2]
            wproj_ref[dst:dst + WP_CHUNK, :] = (v if scale == 1.0 else scale * v).astype(BF16)

    @pl.when(step == 0)
    def _():
        ubuf[:, 0:SC_HIST, :] = jnp.zeros((NA, SC_HIST, CB), F32)
        vbuf[0:RG_HIST, :] = jnp.zeros((RG_HIST, RG_W), F32)
        hstate[...] = jnp.zeros_like(hstate)
        for g in range(2):
            for cp in in_copies(g):
                cp.start()
        load_weights()
        for cp in in_copies(0):
            cp.wait()
        norm_in(0, 0, h_tb.at[0])
        block0_dots_to_scratch(h_tb.at[0])

    @pl.when(step >= OBUFS)
    def _():
        for cp in out_copies(step - OBUFS):
            cp.wait()

    for cp in in_copies(step + 2):
        cp.start()
    for cp in in_copies(step + 1):
        cp.wait()

    def mixer_a_block(j, z):
        zb, zc, zv, hzg = z
        u = zc * zv
        ubuf[j, SC_HIST:SC_HIST + ROWS, :] = u
        cw = sccw_ref[:, j * CB:(j + 1) * CB]
        conv = sccb_ref[:, j * CB:(j + 1) * CB] + ubuf[j, 0:ROWS, :] * cw[0:1, :]
        conv = conv + ubuf[j, BATCH:BATCH + ROWS, :] * cw[1:2, :]
        conv = conv + u * cw[2:3, :]
        ubuf[j, 0:SC_HIST, :] = u[ROWS - SC_HIST:ROWS, :]
        s_scr[j] = (zb * conv * _half_silu_from_half(hzg)).astype(BF16)

    def gate_dot(c):
        start = _gate_win_start(c)
        slabs = vbf if start % LANES == 0 else vbs
        s0 = start // LANES
        lhs = jnp.concatenate([slabs[s0 + i] for i in range(GATE_WIN // LANES)], axis=1)
        return _dot(lhs, wg_ref[c])

    def gate_dots(h_cur, k):
        pieces = [gate_dot(c) for c in range(k * CB // GB, (k + 1) * CB // GB)]
        hg_a = jnp.concatenate([p[:, 0:GB] for p in pieces], axis=1)
        hg_x = jnp.concatenate([p[:, GB:2 * GB] for p in pieces], axis=1)
        return (hg_a, hg_x), proj(h_cur, O_RGG + k * CB, CB)

    def mixer_b_block(k, hg, hzg):
        hg_a, hg_x = hg
        cols = slice(k * CB, (k + 1) * CB)
        t_r = jnp.tanh(hg_a + 0.5 * gba_ref[:, cols])
        t_i = jnp.tanh(hg_x + 0.5 * gbx_ref[:, cols])
        half_k = (-0.5 * LRU_C) * jax.nn.softplus(-lam_ref[:, cols])
        log_a = half_k + half_k * t_r
        a = jnp.exp(log_a)
        a_scr[...] = a
        one_minus_a2 = jnp.tanh(log_a) * (-1.0 - a * a)
        root = jnp.where(one_minus_a2 > 0.0, one_minus_a2 * lax.rsqrt(one_minus_a2), 0.0)
        hvk = hvf[k]
        b_scr[...] = root * (hvk + hvk * t_i)
        h = hstate[k]
        for t in range(TS):
            rows = slice(t * BATCH, (t + 1) * BATCH)
            h = a_scr[rows, :] * h + b_scr[rows, :]
            yrg_scr[rows, :] = h
        hstate[k] = h
        s2_scr[k] = (yrg_scr[...] * _half_silu_from_half(hzg)).astype(BF16)

    def process_tile(c):
        h_cur = h_tb.at[c % 2]
        h_next = h_tb.at[(c + 1) % 2]

        z = tuple(z0_scr[n] for n in range(4))
        for j in range(NA):
            z_next = mixer_a_dots(h_cur, j + 1) if j + 1 < NA else None
            if j == 0:
                if c + 1 < TPS:
                    norm_in(xg_cur, c + 1, h_next)
                else:
                    norm_in(xg_next, 0, h_next)
            mixer_a_block(j, z)
            z = z_next

        vz = proj(h_cur, O_RGV, RG_W)
        m_pieces = [proj(h_cur, O_M, M_PIECE)]
        s_all = jnp.concatenate([s_scr[j] for j in range(NA)], axis=1)
        ya_half = _dot(s_all, wproj_ref[WP_SC:WP_SC + D_MODEL, :])
        vbuf[RG_HIST:RG_HIST + ROWS, :] = vz
        rcw = 0.5 * rgcw_ref[...]
        hv = 0.5 * rgcb_ref[...] + vbuf[0:ROWS, :] * rcw[0:1, :]
        hv = hv + vbuf[BATCH:BATCH + ROWS, :] * rcw[1:2, :]
        hv = hv + vbuf[2 * BATCH:2 * BATCH + ROWS, :] * rcw[2:3, :]
        hv = hv + vz * rcw[3:4, :]
        vbuf[0:RG_HIST, :] = vz[ROWS - RG_HIST:ROWS, :]
        for cidx in range(NG):
            vbf[cidx] = hv[:, cidx * LANES:(cidx + 1) * LANES].astype(BF16)
        for cidx in range(NG - 1):
            lo = GATE_SHIFT + cidx * LANES
            vbs[cidx] = hv[:, lo:lo + LANES].astype(BF16)
        for k in range(NB):
            hvf[k] = hv[:, k * CB:(k + 1) * CB]

        hg, hzg = gate_dots(h_cur, 0)
        for k in range(NB):
            hg_next, hzg_next = gate_dots(h_cur, k + 1) if k + 1 < NB else (None, None)
            if 1 <= k < NB - 1:
                m_pieces.append(proj(h_cur, O_M + k * M_PIECE, M_PIECE))
            mixer_b_block(k, hg, hzg)
            hg, hzg = hg_next, hzg_next
        m = jnp.concatenate(m_pieces, axis=1)

        s2_all = jnp.concatenate([s2_scr[k] for k in range(NB)], axis=1)
        yb_half = _dot(s2_all, wproj_ref[WP_RG:WP_RG + RG_W, :])

        t_a = jnp.tanh(m[:, 0:D_MODEL] + 0.5 * bm_ref[0:1, :])
        t_b = jnp.tanh(m[:, D_MODEL:2 * D_MODEL] + 0.5 * bm_ref[1:2, :])
        merged = ((ya_half + ya_half * t_a) + (yb_half + yb_half * t_b)).astype(BF16)
        y3 = _dot(merged, wproj_ref[WP_OUT:WP_OUT + D_MODEL, :]).reshape(TS, BATCH, D_MODEL)
        block0_dots_to_scratch(h_next)
        xo = xbuf[xg_cur, c * TS:(c + 1) * TS] + gate * y3
        ms = jnp.mean(xo * xo, axis=-1, keepdims=True)
        obuf[og_cur, c * TS:(c + 1) * TS] = xo * lax.rsqrt(ms + EPS) * gfinal_ref[...]

    for c in range(TPS):
        process_tile(c)

    for cp in out_copies(step):
        cp.start()

    @pl.when(step == NSTEPS - 1)
    def _():
        for cp in in_copies(step + 2):
            cp.wait()
        for cp in out_copies(step - 1):
            cp.wait()
        for cp in out_copies(step):
            cp.wait()


def _const_spec(shape):
    nd = len(shape)
    return pl.BlockSpec(shape, lambda i, _nd=nd: (0,) * _nd, pipeline_mode=pl.Buffered(1))


def _gate_weights(w_a, w_x):
    starts = [_gate_win_start(c) for c in range(NG)]
    row_ch = jnp.array(starts)[:, None] + jnp.arange(GATE_WIN)[None, :]
    col_ch = jnp.arange(NG)[:, None] * GB + jnp.arange(GB)[None, :]
    same_head = (row_ch // RG_HEAD_DIM)[:, :, None] == (col_ch // RG_HEAD_DIM)[:, None, :]
    spread = (jnp.arange(RG_HEAD_DIM)[None, :, None] == (col_ch % RG_HEAD_DIM)[:, None, :]).astype(BF16)

    def expand(w):
        rows = w.reshape(RG_W, RG_HEAD_DIM).astype(BF16)
        win = jnp.stack([rows[s:s + GATE_WIN] for s in starts])
        dense = jnp.einsum("kre,kec->krc", win, spread, preferred_element_type=F32)
        return jnp.where(same_head, dense, 0.0).astype(BF16)

    return jnp.concatenate([expand(w_a), expand(w_x)], axis=2)


def kernel(x, c, w_ada, b_ada, g_norm, w_in, sc_conv_w, sc_conv_b, sc_w_out, rg_conv_w, rg_conv_b,
           rg_w_a, rg_b_a, rg_w_x, rg_b_x, rg_lambda, rg_w_out, b_merge, w_out, g_final):
    assert x.shape == (BATCH, SEQ, D_MODEL) and w_ada.shape[0] == 1
    l = 0
    mod = _adaln(c, w_ada[l], b_ada[l])

    wg = _gate_weights(rg_w_a[l], rg_w_x[l])

    vmem_operands = (
        mod, g_norm[l].reshape(1, D_MODEL), g_final.reshape(1, D_MODEL),
        sc_conv_w[l], sc_conv_b[l].reshape(1, D_MODEL), rg_conv_w[l], rg_conv_b[l].reshape(1, RG_W),
        rg_b_a[l].reshape(1, RG_W), rg_b_x[l].reshape(1, RG_W), rg_lambda[l].reshape(1, RG_W), b_merge[l],
        wg)
    hbm_weights = (w_in[l], sc_w_out[l], rg_w_out[l], w_out[l])
    operands = (x,) + vmem_operands + hbm_weights
    any_spec = pl.BlockSpec(memory_space=pl.ANY)
    in_specs = [any_spec] + [_const_spec(op.shape) for op in vmem_operands] + [any_spec] * len(hbm_weights)
    scratch = [
        pltpu.VMEM((D_MODEL, P_IN), BF16),
        pltpu.VMEM((WP_ROWS, D_MODEL), BF16),
        pltpu.VMEM((2, WIN_CHUNK, P_IN), F32),
        pltpu.VMEM((2, WP_CHUNK, D_MODEL), F32),
        pltpu.SemaphoreType.DMA((2,)),
        pltpu.VMEM((XBUFS, GROUP_T, BATCH, D_MODEL), F32),
        pltpu.VMEM((OBUFS, GROUP_T, BATCH, D_MODEL), F32),
        pltpu.SemaphoreType.DMA((XBUFS,)),
        pltpu.SemaphoreType.DMA((OBUFS,)),
        pltpu.VMEM((2, ROWS, D_MODEL), BF16),
        pltpu.VMEM((4, ROWS, CB), F32),
        pltpu.VMEM((NA, SC_HIST + ROWS, CB), F32),
        pltpu.VMEM((NA, ROWS, CB), BF16),
        pltpu.VMEM((RG_HIST + ROWS, RG_W), F32),
        pltpu.VMEM((NB, ROWS, CB), F32),
        pltpu.VMEM((NG, ROWS, LANES), BF16),
        pltpu.VMEM((NG - 1, ROWS, LANES), BF16),
        pltpu.VMEM((ROWS, CB), F32),
        pltpu.VMEM((ROWS, CB), F32),
        pltpu.VMEM((ROWS, CB), F32),
        pltpu.VMEM((NB, ROWS, CB), BF16),
        pltpu.VMEM((NB, BATCH, CB), F32),
    ]
    return pl.pallas_call(
        _block_kernel,
        grid=(NSTEPS,),
        in_specs=in_specs,
        out_specs=any_spec,
        out_shape=jax.ShapeDtypeStruct((BATCH, SEQ, D_MODEL), F32),
        scratch_shapes=scratch,
        compiler_params=pltpu.CompilerParams(
            dimension_semantics=("arbitrary",), vmem_limit_bytes=VMEM_LIMIT_BYTES),
        name="hybrid_block",
    )(*operands)
```

```python
import jax
import jax.numpy as jnp
from jax import lax
from jax.experimental import pallas as pl
from jax.experimental.pallas import tpu as pltpu

D_MODEL = 1024
BATCH = 16
SEQ = 2048
SC_K = 3
RG_W = 1280
RG_HEADS = 16
RG_HEAD_DIM = RG_W // RG_HEADS
RG_K = 4
LRU_C = 8.0
EPS = 1e-6

LANES = 128
MXU_DIM = 256
V7X_VMEM_BYTES = 64 * 1024 * 1024
TS = 16
TPS = 2
NTILES = SEQ // TS
NSTEPS = NTILES // TPS
assert TPS % 2 == 0
ROWS = TS * BATCH
GROUP_T = TPS * TS
XBUFS = 3
OBUFS = 2
CB = MXU_DIM
NA = D_MODEL // CB
NB = RG_W // CB
GB = LANES
NG = RG_W // GB
GATE_WIN = MXU_DIM
GATE_SHIFT = LANES // 2
SC_HIST = (SC_K - 1) * BATCH
RG_HIST = (RG_K - 1) * BATCH
VMEM_LIMIT_BYTES = V7X_VMEM_BYTES - 6 * 1024 * 1024

O_SCB, O_SCC, O_SCV, O_SCG = 0, D_MODEL, 2 * D_MODEL, 3 * D_MODEL
O_RGV = 4 * D_MODEL
O_RGG = O_RGV + RG_W
O_M = O_RGG + RG_W
P_IN = O_M + 2 * D_MODEL
M_PIECE = 2 * D_MODEL // (NB - 1)

WP_SC = 0
WP_RG = WP_SC + D_MODEL
WP_OUT = WP_RG + RG_W
WP_ROWS = WP_OUT + D_MODEL

WIN_CHUNK = 64
WIN_SLOTS = 4
WP_CHUNK = 256
WIN_COL_SCALES = ((0, O_SCG, 1.0), (O_SCG, O_RGV, 0.5), (O_RGV, O_RGG, 1.0), (O_RGG, P_IN, 0.5))

F32 = jnp.float32
BF16 = jnp.bfloat16


def _gate_win_start(c):
    lo = (c * GB) // RG_HEAD_DIM * RG_HEAD_DIM
    hi = -(-(c * GB + GB) // RG_HEAD_DIM) * RG_HEAD_DIM
    start = min(lo // GATE_SHIFT * GATE_SHIFT, RG_W - GATE_WIN)
    assert start <= lo and hi <= start + GATE_WIN
    return start


def _adaln_kernel(c_ref, w_ref, b_ref, o_ref):
    c = c_ref[...]
    c_act = (c * jax.nn.sigmoid(c)).astype(BF16)
    o_ref[...] = jnp.dot(c_act, w_ref[...].astype(BF16), preferred_element_type=F32) + b_ref[...]


def _adaln(c, w_ada, b_ada):
    n = w_ada.shape[1]
    nblk = n // D_MODEL
    return pl.pallas_call(
        _adaln_kernel,
        grid=(nblk,),
        in_specs=[
            pl.BlockSpec((BATCH, D_MODEL), lambda i: (0, 0)),
            pl.BlockSpec((D_MODEL, D_MODEL), lambda i: (0, i)),
            pl.BlockSpec((1, D_MODEL), lambda i: (0, i)),
        ],
        out_specs=pl.BlockSpec((BATCH, D_MODEL), lambda i: (0, i)),
        out_shape=jax.ShapeDtypeStruct((BATCH, n), F32),
        name="adaln",
    )(c, w_ada, b_ada.reshape(1, n))


def _dot(a, b):
    return jnp.dot(a, b, preferred_element_type=F32)


def _half_silu_from_half(hz):
    return hz + hz * jnp.tanh(hz)


def _group_copies(hbm_ref, buf_ref, sem_ref, group, slot, to_hbm):
    copies = []
    for b in range(BATCH):
        hbm = hbm_ref.at[b, pl.ds(group * GROUP_T, GROUP_T), :]
        vmem = buf_ref.at[slot, :, b, :]
        src, dst = (vmem, hbm) if to_hbm else (hbm, vmem)
        copies.append(pltpu.make_async_copy(src, dst, sem_ref.at[slot]))
    return copies


def _block_kernel(x_hbm, mod_ref, gnorm_ref, gfinal_ref, sccw_ref, sccb_ref, rgcw_ref, rgcb_ref,
                  gba_ref, gbx_ref, lam_ref, bm_ref, wg_ref, win_hbm, scw_hbm, rgw_hbm, ow_hbm,
                  out_hbm,
                  win_ref, wproj_ref, stage_in, stage_proj, win_sem, w_sem,
                  xbuf, obuf, in_sem, out_sem,
                  h_tb, z0_scr, ubuf, s_scr, vbuf, hvf, vbf, vbs, a_scr, b_scr, yrg_scr, s2_scr, hstate):
    step = pl.program_id(0)
    xg_cur = lax.rem(step, XBUFS)
    xg_next = lax.rem(step + 1, XBUFS)
    og_cur = lax.rem(step, OBUFS)

    def in_copies(group):
        if isinstance(group, int):
            return _group_copies(x_hbm, xbuf, in_sem, min(group, NSTEPS - 1), group % XBUFS, to_hbm=False)
        return _group_copies(x_hbm, xbuf, in_sem, jnp.minimum(group, NSTEPS - 1),
                             lax.rem(group, XBUFS), to_hbm=False)

    def out_copies(group):
        return _group_copies(out_hbm, obuf, out_sem, group, lax.rem(group, OBUFS), to_hbm=True)

    shift = mod_ref[:, 0:D_MODEL]
    scale1 = 1.0 + mod_ref[:, D_MODEL:2 * D_MODEL]
    gate = mod_ref[:, 2 * D_MODEL:3 * D_MODEL]

    def norm_in(x_slot, c, h_ref):
        x3 = xbuf[x_slot, c * TS:(c + 1) * TS]
        ms = jnp.mean(x3 * x3, axis=-1, keepdims=True)
        hn = (x3 * lax.rsqrt(ms + EPS) * gnorm_ref[...]) * scale1 + shift
        h_ref[...] = hn.reshape(ROWS, D_MODEL).astype(BF16)

    def proj(h_ref, col0, width):
        return _dot(h_ref[...], win_ref[:, col0:col0 + width])

    def mixer_a_dots(h_ref, j):
        return tuple(proj(h_ref, o + j * CB, CB) for o in (O_SCB, O_SCC, O_SCV, O_SCG))

    def block0_dots_to_scratch(h_ref):
        for n, zn in enumerate(mixer_a_dots(h_ref, 0)):
            z0_scr[n] = zn

    def load_weights():
        n_win = D_MODEL // WIN_CHUNK

        def win_copy(i):
            slot = i % WIN_SLOTS if isinstance(i, int) else lax.rem(i, WIN_SLOTS)
            return pltpu.make_async_copy(win_hbm.at[pl.ds(i * WIN_CHUNK, WIN_CHUNK), :],
                                         stage_in.at[slot], win_sem.at[slot])

        for i in range(WIN_SLOTS - 1):
            win_copy(i).start()

        def win_body(i, carry):
            slot = lax.rem(i, WIN_SLOTS)

            @pl.when(i + WIN_SLOTS - 1 < n_win)
            def _():
                win_copy(i + WIN_SLOTS - 1).start()

            win_copy(i).wait()
            rows = pl.ds(pl.multiple_of(i * WIN_CHUNK, WIN_CHUNK), WIN_CHUNK)
            for c0, c1, scale in WIN_COL_SCALES:
                v = stage_in[slot, :, c0:c1]
                win_ref[rows, c0:c1] = (v if scale == 1.0 else scale * v).astype(BF16)
            return carry

        chunks = []
        for src, dst0, nrows, scale in ((scw_hbm, WP_SC, D_MODEL, 0.5), (rgw_hbm, WP_RG, RG_W, 0.5),
                                        (ow_hbm, WP_OUT, D_MODEL, 1.0)):
            chunks += [(src, r, dst0 + r, scale) for r in range(0, nrows, WP_CHUNK)]

        def proj_copy(n):
            src, r, _, _ = chunks[n]
            return pltpu.make_async_copy(src.at[r:r + WP_CHUNK, :], stage_proj.at[n % 2], w_sem.at[n % 2])

        proj_copy(0).start()
        lax.fori_loop(0, n_win, win_body, 0)

        for n, (_, _, dst, scale) in enumerate(chunks):
            if n + 1 < len(chunks):
                proj_copy(n + 1).start()
            proj_copy(n).wait()
            v = stage_proj[n % 2]
            wproj_ref[dst:dst + WP_CHUNK, :] = (v if scale == 1.0 else scale * v).astype(BF16)

    @pl.when(step == 0)
    def _():
        ubuf[:, 0:SC_HIST, :] = jnp.zeros((NA, SC_HIST, CB), F32)
        vbuf[0:RG_HIST, :] = jnp.zeros((RG_HIST, RG_W), F32)
        hstate[...] = jnp.zeros_like(hstate)
        for g in range(2):
            for cp in in_copies(g):
                cp.start()
        load_weights()
        for cp in in_copies(0):
            cp.wait()
        norm_in(0, 0, h_tb.at[0])
        block0_dots_to_scratch(h_tb.at[0])

    @pl.when(step >= OBUFS)
    def _():
        for cp in out_copies(step - OBUFS):
            cp.wait()

    for cp in in_copies(step + 2):
        cp.start()
    for cp in in_copies(step + 1):
        cp.wait()

    def mixer_a_block(j, z):
        zb, zc, zv, hzg = z
        u = zc * zv
        ubuf[j, SC_HIST:SC_HIST + ROWS, :] = u
        cw = sccw_ref[:, j * CB:(j + 1) * CB]
        conv = sccb_ref[:, j * CB:(j + 1) * CB] + ubuf[j, 0:ROWS, :] * cw[0:1, :]
        conv = conv + ubuf[j, BATCH:BATCH + ROWS, :] * cw[1:2, :]
        conv = conv + u * cw[2:3, :]
        ubuf[j, 0:SC_HIST, :] = u[ROWS - SC_HIST:ROWS, :]
        s_scr[j] = (zb * conv * _half_silu_from_half(hzg)).astype(BF16)

    def gate_dot(c):
        start = _gate_win_start(c)
        slabs = vbf if start % LANES == 0 else vbs
        s0 = start // LANES
        lhs = jnp.concatenate([slabs[s0 + i] for i in range(GATE_WIN // LANES)], axis=1)
        return _dot(lhs, wg_ref[c])

    def gate_dots(h_cur, k):
        pieces = [gate_dot(c) for c in range(k * CB // GB, (k + 1) * CB // GB)]
        hg_a = jnp.concatenate([p[:, 0:GB] for p in pieces], axis=1)
        hg_x = jnp.concatenate([p[:, GB:2 * GB] for p in pieces], axis=1)
        return (hg_a, hg_x), proj(h_cur, O_RGG + k * CB, CB)

    def mixer_b_block(k, hg, hzg):
        hg_a, hg_x = hg
        cols = slice(k * CB, (k + 1) * CB)
        t_r = jnp.tanh(hg_a + 0.5 * gba_ref[:, cols])
        t_i = jnp.tanh(hg_x + 0.5 * gbx_ref[:, cols])
        half_k = (-0.5 * LRU_C) * jax.nn.softplus(-lam_ref[:, cols])
        log_a = half_k + half_k * t_r
        a = jnp.exp(log_a)
        a_scr[...] = a
        one_minus_a2 = jnp.tanh(log_a) * (-1.0 - a * a)
        root = jnp.where(one_minus_a2 > 0.0, one_minus_a2 * lax.rsqrt(one_minus_a2), 0.0)
        hvk = hvf[k]
        b_scr[...] = root * (hvk + hvk * t_i)
        h = hstate[k]
        for t in range(TS):
            rows = slice(t * BATCH, (t + 1) * BATCH)
            h = a_scr[rows, :] * h + b_scr[rows, :]
            yrg_scr[rows, :] = h
        hstate[k] = h
        s2_scr[k] = (yrg_scr[...] * _half_silu_from_half(hzg)).astype(BF16)

    def process_tile(c):
        h_cur = h_tb.at[c % 2]
        h_next = h_tb.at[(c + 1) % 2]

        z = tuple(z0_scr[n] for n in range(4))
        for j in range(NA):
            z_next = mixer_a_dots(h_cur, j + 1) if j + 1 < NA else None
            if j == 0:
                if c + 1 < TPS:
                    norm_in(xg_cur, c + 1, h_next)
                else:
                    norm_in(xg_next, 0, h_next)
            mixer_a_block(j, z)
            z = z_next

        vz = proj(h_cur, O_RGV, RG_W)
        m_pieces = [proj(h_cur, O_M, M_PIECE)]
        s_all = jnp.concatenate([s_scr[j] for j in range(NA)], axis=1)
        ya_half = _dot(s_all, wproj_ref[WP_SC:WP_SC + D_MODEL, :])
        vbuf[RG_HIST:RG_HIST + ROWS, :] = vz
        rcw = 0.5 * rgcw_ref[...]
        hv = 0.5 * rgcb_ref[...] + vbuf[0:ROWS, :] * rcw[0:1, :]
        hv = hv + vbuf[BATCH:BATCH + ROWS, :] * rcw[1:2, :]
        hv = hv + vbuf[2 * BATCH:2 * BATCH + ROWS, :] * rcw[2:3, :]
        hv = hv + vz * rcw[3:4, :]
        vbuf[0:RG_HIST, :] = vz[ROWS - RG_HIST:ROWS, :]
        for cidx in range(NG):
            vbf[cidx] = hv[:, cidx * LANES:(cidx + 1) * LANES].astype(BF16)
        for cidx in range(NG - 1):
            lo = GATE_SHIFT + cidx * LANES
            vbs[cidx] = hv[:, lo:lo + LANES].astype(BF16)
        for k in range(NB):
            hvf[k] = hv[:, k * CB:(k + 1) * CB]

        hg, hzg = gate_dots(h_cur, 0)
        for k in range(NB):
            hg_next, hzg_next = gate_dots(h_cur, k + 1) if k + 1 < NB else (None, None)
            if 1 <= k < NB - 1:
                m_pieces.append(proj(h_cur, O_M + k * M_PIECE, M_PIECE))
            mixer_b_block(k, hg, hzg)
            hg, hzg = hg_next, hzg_next
        m = jnp.concatenate(m_pieces, axis=1)

        s2_all = jnp.concatenate([s2_scr[k] for k in range(NB)], axis=1)
        yb_half = _dot(s2_all, wproj_ref[WP_RG:WP_RG + RG_W, :])

        t_a = jnp.tanh(m[:, 0:D_MODEL] + 0.5 * bm_ref[0:1, :])
        t_b = jnp.tanh(m[:, D_MODEL:2 * D_MODEL] + 0.5 * bm_ref[1:2, :])
        merged = ((ya_half + ya_half * t_a) + (yb_half + yb_half * t_b)).astype(BF16)
        y3 = _dot(merged, wproj_ref[WP_OUT:WP_OUT + D_MODEL, :]).reshape(TS, BATCH, D_MODEL)
        block0_dots_to_scratch(h_next)
        xo = xbuf[xg_cur, c * TS:(c + 1) * TS] + gate * y3
        ms = jnp.mean(xo * xo, axis=-1, keepdims=True)
        obuf[og_cur, c * TS:(c + 1) * TS] = xo * lax.rsqrt(ms + EPS) * gfinal_ref[...]

    for c in range(TPS):
        process_tile(c)

    for cp in out_copies(step):
        cp.start()

    @pl.when(step == NSTEPS - 1)
    def _():
        for cp in in_copies(step + 2):
            cp.wait()
        for cp in out_copies(step - 1):
            cp.wait()
        for cp in out_copies(step):
            cp.wait()


def _const_spec(shape):
    nd = len(shape)
    return pl.BlockSpec(shape, lambda i, _nd=nd: (0,) * _nd, pipeline_mode=pl.Buffered(1))


def _gate_weights(w_a, w_x):
    starts = [_gate_win_start(c) for c in range(NG)]
    row_ch = jnp.array(starts)[:, None] + jnp.arange(GATE_WIN)[None, :]
    col_ch = jnp.arange(NG)[:, None] * GB + jnp.arange(GB)[None, :]
    same_head = (row_ch // RG_HEAD_DIM)[:, :, None] == (col_ch // RG_HEAD_DIM)[:, None, :]
    spread = (jnp.arange(RG_HEAD_DIM)[None, :, None] == (col_ch % RG_HEAD_DIM)[:, None, :]).astype(BF16)

    def expand(w):
        rows = w.reshape(RG_W, RG_HEAD_DIM).astype(BF16)
        win = jnp.stack([rows[s:s + GATE_WIN] for s in starts])
        dense = jnp.einsum("kre,kec->krc", win, spread, preferred_element_type=F32)
        return jnp.where(same_head, dense, 0.0).astype(BF16)

    return jnp.concatenate([expand(w_a), expand(w_x)], axis=2)


def kernel(x, c, w_ada, b_ada, g_norm, w_in, sc_conv_w, sc_conv_b, sc_w_out, rg_conv_w, rg_conv_b,
           rg_w_a, rg_b_a, rg_w_x, rg_b_x, rg_lambda, rg_w_out, b_merge, w_out, g_final):
    assert x.shape == (BATCH, SEQ, D_MODEL) and w_ada.shape[0] == 1
    l = 0
    mod = _adaln(c, w_ada[l], b_ada[l])

    wg = _gate_weights(rg_w_a[l], rg_w_x[l])

    vmem_operands = (
        mod, g_norm[l].reshape(1, D_MODEL), g_final.reshape(1, D_MODEL),
        sc_conv_w[l], sc_conv_b[l].reshape(1, D_MODEL), rg_conv_w[l], rg_conv_b[l].reshape(1, RG_W),
        rg_b_a[l].reshape(1, RG_W), rg_b_x[l].reshape(1, RG_W), rg_lambda[l].reshape(1, RG_W), b_merge[l],
        wg)
    hbm_weights = (w_in[l], sc_w_out[l], rg_w_out[l], w_out[l])
    operands = (x,) + vmem_operands + hbm_weights
    any_spec = pl.BlockSpec(memory_space=pl.ANY)
    in_specs = [any_spec] + [_const_spec(op.shape) for op in vmem_operands] + [any_spec] * len(hbm_weights)
    scratch = [
        pltpu.VMEM((D_MODEL, P_IN), BF16),
        pltpu.VMEM((WP_ROWS, D_MODEL), BF16),
        pltpu.VMEM((WIN_SLOTS, WIN_CHUNK, P_IN), F32),
        pltpu.VMEM((2, WP_CHUNK, D_MODEL), F32),
        pltpu.SemaphoreType.DMA((WIN_SLOTS,)),
        pltpu.SemaphoreType.DMA((2,)),
        pltpu.VMEM((XBUFS, GROUP_T, BATCH, D_MODEL), F32),
        pltpu.VMEM((OBUFS, GROUP_T, BATCH, D_MODEL), F32),
        pltpu.SemaphoreType.DMA((XBUFS,)),
        pltpu.SemaphoreType.DMA((OBUFS,)),
        pltpu.VMEM((2, ROWS, D_MODEL), BF16),
        pltpu.VMEM((4, ROWS, CB), F32),
        pltpu.VMEM((NA, SC_HIST + ROWS, CB), F32),
        pltpu.VMEM((NA, ROWS, CB), BF16),
        pltpu.VMEM((RG_HIST + ROWS, RG_W), F32),
        pltpu.VMEM((NB, ROWS, CB), F32),
        pltpu.VMEM((NG, ROWS, LANES), BF16),
        pltpu.VMEM((NG - 1, ROWS, LANES), BF16),
        pltpu.VMEM((ROWS, CB), F32),
        pltpu.VMEM((ROWS, CB), F32),
        pltpu.VMEM((ROWS, CB), F32),
        pltpu.VMEM((NB, ROWS, CB), BF16),
        pltpu.VMEM((NB, BATCH, CB), F32),
    ]
    return pl.pallas_call(
        _block_kernel,
        grid=(NSTEPS,),
        in_specs=in_specs,
        out_specs=any_spec,
        out_shape=jax.ShapeDtypeStruct((BATCH, SEQ, D_MODEL), F32),
        scratch_shapes=scratch,
        compiler_params=pltpu.CompilerParams(
            dimension_semantics=("arbitrary",), vmem_limit_bytes=VMEM_LIMIT_BYTES),
        name="hybrid_block",
    )(*operands)
```

```python
import jax
import jax.numpy as jnp
from jax import lax
from jax.experimental import pallas as pl
from jax.experimental.pallas import tpu as pltpu

D_MODEL = 1024
BATCH = 16
SEQ = 2048
SC_K = 3
RG_W = 1280
RG_HEADS = 16
RG_HEAD_DIM = RG_W // RG_HEADS
RG_K = 4
LRU_C = 8.0
EPS = 1e-6

LANES = 128
MXU_DIM = 256
V7X_VMEM_BYTES = 64 * 1024 * 1024
TS = 16
TPS = 2
NTILES = SEQ // TS
NSTEPS = NTILES // TPS
assert TPS % 2 == 0
ROWS = TS * BATCH
GROUP_T = TPS * TS
XBUFS = 3
OBUFS = 2
CB = MXU_DIM
NA = D_MODEL // CB
NB = RG_W // CB
GB = LANES
NG = RG_W // GB
GATE_WIN = MXU_DIM
GATE_SHIFT = LANES // 2
SC_HIST = (SC_K - 1) * BATCH
RG_HIST = (RG_K - 1) * BATCH
VMEM_LIMIT_BYTES = V7X_VMEM_BYTES - 6 * 1024 * 1024

O_SCB, O_SCC, O_SCV, O_SCG = 0, D_MODEL, 2 * D_MODEL, 3 * D_MODEL
O_RGV = 4 * D_MODEL
O_RGG = O_RGV + RG_W
O_M = O_RGG + RG_W
P_IN = O_M + 2 * D_MODEL
M_PIECE = 2 * D_MODEL // (NB - 1)

WP_SC = 0
WP_RG = WP_SC + D_MODEL
WP_OUT = WP_RG + RG_W
WP_ROWS = WP_OUT + D_MODEL

WIN_CHUNK = 64
WIN_SLOTS = 4
WP_CHUNK = 256
WIN_COL_SCALES = ((0, O_SCG, 1.0), (O_SCG, O_RGV, 0.5), (O_RGV, O_RGG, 1.0), (O_RGG, P_IN, 0.5))

F32 = jnp.float32
BF16 = jnp.bfloat16


def _gate_win_start(c):
    lo = (c * GB) // RG_HEAD_DIM * RG_HEAD_DIM
    hi = -(-(c * GB + GB) // RG_HEAD_DIM) * RG_HEAD_DIM
    start = min(lo // GATE_SHIFT * GATE_SHIFT, RG_W - GATE_WIN)
    assert start <= lo and hi <= start + GATE_WIN
    return start


def _dot(a, b):
    return jnp.dot(a, b, preferred_element_type=F32)


def _half_silu_from_half(hz):
    return hz + hz * jnp.tanh(hz)


def _group_copies(hbm_ref, buf_ref, sem_ref, group, slot, to_hbm):
    copies = []
    for b in range(BATCH):
        hbm = hbm_ref.at[b, pl.ds(group * GROUP_T, GROUP_T), :]
        vmem = buf_ref.at[slot, :, b, :]
        src, dst = (vmem, hbm) if to_hbm else (hbm, vmem)
        copies.append(pltpu.make_async_copy(src, dst, sem_ref.at[slot]))
    return copies


def _block_kernel(x_hbm, c_ref, bada_ref, gnorm_ref, gfinal_ref, sccw_ref, sccb_ref, rgcw_ref, rgcb_ref,
                  gba_ref, gbx_ref, lam_ref, bm_ref, wg_ref, wada_hbm, win_hbm, scw_hbm, rgw_hbm, ow_hbm,
                  out_hbm,
                  mod_scr, win_ref, wproj_ref, stage_in, stage_proj, win_sem, w_sem,
                  xbuf, obuf, in_sem, out_sem,
                  h_tb, z0_scr, ubuf, s_scr, vbuf, hvf, vbf, vbs, a_scr, b_scr, yrg_scr, s2_scr, hstate):
    step = pl.program_id(0)
    xg_cur = lax.rem(step, XBUFS)
    xg_next = lax.rem(step + 1, XBUFS)
    og_cur = lax.rem(step, OBUFS)

    def in_copies(group):
        if isinstance(group, int):
            return _group_copies(x_hbm, xbuf, in_sem, min(group, NSTEPS - 1), group % XBUFS, to_hbm=False)
        return _group_copies(x_hbm, xbuf, in_sem, jnp.minimum(group, NSTEPS - 1),
                             lax.rem(group, XBUFS), to_hbm=False)

    def out_copies(group):
        return _group_copies(out_hbm, obuf, out_sem, group, lax.rem(group, OBUFS), to_hbm=True)

    def norm_in(x_slot, c, h_ref):
        x3 = xbuf[x_slot, c * TS:(c + 1) * TS]
        ms = jnp.mean(x3 * x3, axis=-1, keepdims=True)
        shift = mod_scr[:, 0:D_MODEL]
        scale1 = 1.0 + mod_scr[:, D_MODEL:2 * D_MODEL]
        hn = (x3 * lax.rsqrt(ms + EPS) * gnorm_ref[...]) * scale1 + shift
        h_ref[...] = hn.reshape(ROWS, D_MODEL).astype(BF16)

    def proj(h_ref, col0, width):
        return _dot(h_ref[...], win_ref[:, col0:col0 + width])

    def mixer_a_dots(h_ref, j):
        return tuple(proj(h_ref, o + j * CB, CB) for o in (O_SCB, O_SCC, O_SCV, O_SCG))

    def block0_dots_to_scratch(h_ref):
        for n, zn in enumerate(mixer_a_dots(h_ref, 0)):
            z0_scr[n] = zn

    def load_weights():
        n_win = D_MODEL // WIN_CHUNK

        def win_copy(i):
            slot = i % WIN_SLOTS if isinstance(i, int) else lax.rem(i, WIN_SLOTS)
            return pltpu.make_async_copy(win_hbm.at[pl.ds(i * WIN_CHUNK, WIN_CHUNK), :],
                                         stage_in.at[slot], win_sem.at[slot])

        for i in range(WIN_SLOTS - 1):
            win_copy(i).start()

        def win_body(i, carry):
            slot = lax.rem(i, WIN_SLOTS)

            @pl.when(i + WIN_SLOTS - 1 < n_win)
            def _():
                win_copy(i + WIN_SLOTS - 1).start()

            win_copy(i).wait()
            rows = pl.ds(pl.multiple_of(i * WIN_CHUNK, WIN_CHUNK), WIN_CHUNK)
            for c0, c1, scale in WIN_COL_SCALES:
                v = stage_in[slot, :, c0:c1]
                win_ref[rows, c0:c1] = (v if scale == 1.0 else scale * v).astype(BF16)
            return carry

        chunks = []
        for src, dst0, nrows, scale in ((scw_hbm, WP_SC, D_MODEL, 0.5), (rgw_hbm, WP_RG, RG_W, 0.5),
                                        (ow_hbm, WP_OUT, D_MODEL, 1.0)):
            chunks += [(src, r, dst0 + r, scale) for r in range(0, nrows, WP_CHUNK)]

        def proj_copy(n):
            src, r, _, _ = chunks[n]
            return pltpu.make_async_copy(src.at[r:r + WP_CHUNK, :], stage_proj.at[n % 2], w_sem.at[n % 2])

        proj_copy(0).start()
        lax.fori_loop(0, n_win, win_body, 0)

        for n, (_, _, dst, scale) in enumerate(chunks):
            if n + 1 < len(chunks):
                proj_copy(n + 1).start()
            proj_copy(n).wait()
            v = stage_proj[n % 2]
            wproj_ref[dst:dst + WP_CHUNK, :] = (v if scale == 1.0 else scale * v).astype(BF16)

        def ada_copy(i):
            return pltpu.make_async_copy(wada_hbm.at[i * WIN_CHUNK:(i + 1) * WIN_CHUNK, :],
                                         stage_in.at[i % WIN_SLOTS, :, 0:3 * D_MODEL],
                                         win_sem.at[i % WIN_SLOTS])

        c = c_ref[...]
        c_act = (c * jax.nn.sigmoid(c)).astype(BF16)
        mod = jnp.broadcast_to(bada_ref[...], (BATCH, 3 * D_MODEL))
        for i in range(WIN_SLOTS - 1):
            ada_copy(i).start()
        for i in range(n_win):
            if i + WIN_SLOTS - 1 < n_win:
                ada_copy(i + WIN_SLOTS - 1).start()
            ada_copy(i).wait()
            w_chunk = stage_in[i % WIN_SLOTS, :, 0:3 * D_MODEL].astype(BF16)
            mod = mod + _dot(c_act[:, i * WIN_CHUNK:(i + 1) * WIN_CHUNK], w_chunk)
        mod_scr[...] = mod

    @pl.when(step == 0)
    def _():
        ubuf[:, 0:SC_HIST, :] = jnp.zeros((NA, SC_HIST, CB), F32)
        vbuf[0:RG_HIST, :] = jnp.zeros((RG_HIST, RG_W), F32)
        hstate[...] = jnp.zeros_like(hstate)
        for g in range(2):
            for cp in in_copies(g):
                cp.start()
        load_weights()
        for cp in in_copies(0):
            cp.wait()
        norm_in(0, 0, h_tb.at[0])
        block0_dots_to_scratch(h_tb.at[0])

    @pl.when(step >= OBUFS)
    def _():
        for cp in out_copies(step - OBUFS):
            cp.wait()

    for cp in in_copies(step + 2):
        cp.start()
    for cp in in_copies(step + 1):
        cp.wait()

    def mixer_a_block(j, z):
        zb, zc, zv, hzg = z
        u = zc * zv
        ubuf[j, SC_HIST:SC_HIST + ROWS, :] = u
        cw = sccw_ref[:, j * CB:(j + 1) * CB]
        conv = sccb_ref[:, j * CB:(j + 1) * CB] + ubuf[j, 0:ROWS, :] * cw[0:1, :]
        conv = conv + ubuf[j, BATCH:BATCH + ROWS, :] * cw[1:2, :]
        conv = conv + u * cw[2:3, :]
        ubuf[j, 0:SC_HIST, :] = u[ROWS - SC_HIST:ROWS, :]
        s_scr[j] = (zb * conv * _half_silu_from_half(hzg)).astype(BF16)

    def gate_dot(c):
        start = _gate_win_start(c)
        slabs = vbf if start % LANES == 0 else vbs
        s0 = start // LANES
        lhs = jnp.concatenate([slabs[s0 + i] for i in range(GATE_WIN // LANES)], axis=1)
        return _dot(lhs, wg_ref[c])

    def gate_dots(h_cur, k):
        pieces = [gate_dot(c) for c in range(k * CB // GB, (k + 1) * CB // GB)]
        hg_a = jnp.concatenate([p[:, 0:GB] for p in pieces], axis=1)
        hg_x = jnp.concatenate([p[:, GB:2 * GB] for p in pieces], axis=1)
        return (hg_a, hg_x), proj(h_cur, O_RGG + k * CB, CB)

    def mixer_b_block(k, hg, hzg):
        hg_a, hg_x = hg
        cols = slice(k * CB, (k + 1) * CB)
        t_r = jnp.tanh(hg_a + 0.5 * gba_ref[:, cols])
        t_i = jnp.tanh(hg_x + 0.5 * gbx_ref[:, cols])
        half_k = (-0.5 * LRU_C) * jax.nn.softplus(-lam_ref[:, cols])
        log_a = half_k + half_k * t_r
        a = jnp.exp(log_a)
        a_scr[...] = a
        one_minus_a2 = jnp.tanh(log_a) * (-1.0 - a * a)
        root = jnp.where(one_minus_a2 > 0.0, one_minus_a2 * lax.rsqrt(one_minus_a2), 0.0)
        hvk = hvf[k]
        b_scr[...] = root * (hvk + hvk * t_i)
        h = hstate[k]
        for t in range(TS):
            rows = slice(t * BATCH, (t + 1) * BATCH)
            h = a_scr[rows, :] * h + b_scr[rows, :]
            yrg_scr[rows, :] = h
        hstate[k] = h
        s2_scr[k] = (yrg_scr[...] * _half_silu_from_half(hzg)).astype(BF16)

    def process_tile(c):
        h_cur = h_tb.at[c % 2]
        h_next = h_tb.at[(c + 1) % 2]

        z = tuple(z0_scr[n] for n in range(4))
        for j in range(NA):
            z_next = mixer_a_dots(h_cur, j + 1) if j + 1 < NA else None
            if j == 0:
                if c + 1 < TPS:
                    norm_in(xg_cur, c + 1, h_next)
                else:
                    norm_in(xg_next, 0, h_next)
            mixer_a_block(j, z)
            z = z_next

        vz = proj(h_cur, O_RGV, RG_W)
        m_pieces = [proj(h_cur, O_M, M_PIECE)]
        s_all = jnp.concatenate([s_scr[j] for j in range(NA)], axis=1)
        ya_half = _dot(s_all, wproj_ref[WP_SC:WP_SC + D_MODEL, :])
        vbuf[RG_HIST:RG_HIST + ROWS, :] = vz
        rcw = 0.5 * rgcw_ref[...]
        hv = 0.5 * rgcb_ref[...] + vbuf[0:ROWS, :] * rcw[0:1, :]
        hv = hv + vbuf[BATCH:BATCH + ROWS, :] * rcw[1:2, :]
        hv = hv + vbuf[2 * BATCH:2 * BATCH + ROWS, :] * rcw[2:3, :]
        hv = hv + vz * rcw[3:4, :]
        vbuf[0:RG_HIST, :] = vz[ROWS - RG_HIST:ROWS, :]
        for cidx in range(NG):
            vbf[cidx] = hv[:, cidx * LANES:(cidx + 1) * LANES].astype(BF16)
        for cidx in range(NG - 1):
            lo = GATE_SHIFT + cidx * LANES
            vbs[cidx] = hv[:, lo:lo + LANES].astype(BF16)
        for k in range(NB):
            hvf[k] = hv[:, k * CB:(k + 1) * CB]

        hg, hzg = gate_dots(h_cur, 0)
        for k in range(NB):
            hg_next, hzg_next = gate_dots(h_cur, k + 1) if k + 1 < NB else (None, None)
            if 1 <= k < NB - 1:
                m_pieces.append(proj(h_cur, O_M + k * M_PIECE, M_PIECE))
            mixer_b_block(k, hg, hzg)
            hg, hzg = hg_next, hzg_next
        m = jnp.concatenate(m_pieces, axis=1)

        s2_all = jnp.concatenate([s2_scr[k] for k in range(NB)], axis=1)
        yb_half = _dot(s2_all, wproj_ref[WP_RG:WP_RG + RG_W, :])

        t_a = jnp.tanh(m[:, 0:D_MODEL] + 0.5 * bm_ref[0:1, :])
        t_b = jnp.tanh(m[:, D_MODEL:2 * D_MODEL] + 0.5 * bm_ref[1:2, :])
        merged = ((ya_half + ya_half * t_a) + (yb_half + yb_half * t_b)).astype(BF16)
        y3 = _dot(merged, wproj_ref[WP_OUT:WP_OUT + D_MODEL, :]).reshape(TS, BATCH, D_MODEL)
        block0_dots_to_scratch(h_next)
        xo = xbuf[xg_cur, c * TS:(c + 1) * TS] + mod_scr[:, 2 * D_MODEL:3 * D_MODEL] * y3
        ms = jnp.mean(xo * xo, axis=-1, keepdims=True)
        obuf[og_cur, c * TS:(c + 1) * TS] = xo * lax.rsqrt(ms + EPS) * gfinal_ref[...]

    for c in range(TPS):
        process_tile(c)

    for cp in out_copies(step):
        cp.start()

    @pl.when(step == NSTEPS - 1)
    def _():
        for cp in in_copies(step + 2):
            cp.wait()
        for cp in out_copies(step - 1):
            cp.wait()
        for cp in out_copies(step):
            cp.wait()


def _const_spec(shape):
    nd = len(shape)
    return pl.BlockSpec(shape, lambda i, _nd=nd: (0,) * _nd, pipeline_mode=pl.Buffered(1))


def _gate_weights(w_a, w_x):
    starts = [_gate_win_start(c) for c in range(NG)]
    row_ch = jnp.array(starts)[:, None] + jnp.arange(GATE_WIN)[None, :]
    col_ch = jnp.arange(NG)[:, None] * GB + jnp.arange(GB)[None, :]
    same_head = (row_ch // RG_HEAD_DIM)[:, :, None] == (col_ch // RG_HEAD_DIM)[:, None, :]
    spread = (jnp.arange(RG_HEAD_DIM)[None, :, None] == (col_ch % RG_HEAD_DIM)[:, None, :]).astype(BF16)

    def expand(w):
        rows = w.reshape(RG_W, RG_HEAD_DIM).astype(BF16)
        win = jnp.stack([rows[s:s + GATE_WIN] for s in starts])
        dense = jnp.einsum("kre,kec->krc", win, spread, preferred_element_type=F32)
        return jnp.where(same_head, dense, 0.0).astype(BF16)

    return jnp.concatenate([expand(w_a), expand(w_x)], axis=2)


def kernel(x, c, w_ada, b_ada, g_norm, w_in, sc_conv_w, sc_conv_b, sc_w_out, rg_conv_w, rg_conv_b,
           rg_w_a, rg_b_a, rg_w_x, rg_b_x, rg_lambda, rg_w_out, b_merge, w_out, g_final):
    assert x.shape == (BATCH, SEQ, D_MODEL) and w_ada.shape[0] == 1
    l = 0
    wg = _gate_weights(rg_w_a[l], rg_w_x[l])

    vmem_operands = (
        c, b_ada[l].reshape(1, 3 * D_MODEL), g_norm[l].reshape(1, D_MODEL), g_final.reshape(1, D_MODEL),
        sc_conv_w[l], sc_conv_b[l].reshape(1, D_MODEL), rg_conv_w[l], rg_conv_b[l].reshape(1, RG_W),
        rg_b_a[l].reshape(1, RG_W), rg_b_x[l].reshape(1, RG_W), rg_lambda[l].reshape(1, RG_W), b_merge[l],
        wg)
    hbm_weights = (w_ada[l], w_in[l], sc_w_out[l], rg_w_out[l], w_out[l])
    operands = (x,) + vmem_operands + hbm_weights
    any_spec = pl.BlockSpec(memory_space=pl.ANY)
    in_specs = [any_spec] + [_const_spec(op.shape) for op in vmem_operands] + [any_spec] * len(hbm_weights)
    scratch = [
        pltpu.VMEM((BATCH, 3 * D_MODEL), F32),
        pltpu.VMEM((D_MODEL, P_IN), BF16),
        pltpu.VMEM((WP_ROWS, D_MODEL), BF16),
        pltpu.VMEM((WIN_SLOTS, WIN_CHUNK, P_IN), F32),
        pltpu.VMEM((2, WP_CHUNK, D_MODEL), F32),
        pltpu.SemaphoreType.DMA((WIN_SLOTS,)),
        pltpu.SemaphoreType.DMA((2,)),
        pltpu.VMEM((XBUFS, GROUP_T, BATCH, D_MODEL), F32),
        pltpu.VMEM((OBUFS, GROUP_T, BATCH, D_MODEL), F32),
        pltpu.SemaphoreType.DMA((XBUFS,)),
        pltpu.SemaphoreType.DMA((OBUFS,)),
        pltpu.VMEM((2, ROWS, D_MODEL), BF16),
        pltpu.VMEM((4, ROWS, CB), F32),
        pltpu.VMEM((NA, SC_HIST + ROWS, CB), F32),
        pltpu.VMEM((NA, ROWS, CB), BF16),
        pltpu.VMEM((RG_HIST + ROWS, RG_W), F32),
        pltpu.VMEM((NB, ROWS, CB), F32),
        pltpu.VMEM((NG, ROWS, LANES), BF16),
        pltpu.VMEM((NG - 1, ROWS, LANES), BF16),
        pltpu.VMEM((ROWS, CB), F32),
        pltpu.VMEM((ROWS, CB), F32),
        pltpu.VMEM((ROWS, CB), F32),
        pltpu.VMEM((NB, ROWS, CB), BF16),
        pltpu.VMEM((NB, BATCH, CB), F32),
    ]
    return pl.pallas_call(
        _block_kernel,
        grid=(NSTEPS,),
        in_specs=in_specs,
        out_specs=any_spec,
        out_shape=jax.ShapeDtypeStruct((BATCH, SEQ, D_MODEL), F32),
        scratch_shapes=scratch,
        compiler_params=pltpu.CompilerParams(
            dimension_semantics=("arbitrary",), vmem_limit_bytes=VMEM_LIMIT_BYTES),
        name="hybrid_block",
    )(*operands)
```

```python
import jax
import jax.numpy as jnp
import numpy as np
from jax import lax
from jax.experimental import pallas as pl
from jax.experimental.pallas import tpu as pltpu

D_MODEL = 1024
BATCH = 16
SEQ = 2048
SC_K = 3
RG_W = 1280
RG_HEADS = 16
RG_HEAD_DIM = RG_W // RG_HEADS
RG_K = 4
LRU_C = 8.0
EPS = 1e-6

LANES = 128
MXU_DIM = 256
V7X_VMEM_BYTES = 64 * 1024 * 1024
TS = 16
TPS = 2
NTILES = SEQ // TS
NSTEPS = NTILES // TPS
assert TPS % 2 == 0
ROWS = TS * BATCH
GROUP_T = TPS * TS
XBUFS = 3
OBUFS = 2
CB = MXU_DIM
NA = D_MODEL // CB
NB = RG_W // CB
GB = LANES
NG = RG_W // GB
GATE_WIN = MXU_DIM
GATE_SHIFT = LANES // 2
SC_HIST = (SC_K - 1) * BATCH
RG_HIST = (RG_K - 1) * BATCH
VMEM_LIMIT_BYTES = V7X_VMEM_BYTES - 4 * 1024 * 1024

O_SCB, O_SCC, O_SCV, O_SCG = 0, D_MODEL, 2 * D_MODEL, 3 * D_MODEL
O_RGV = 4 * D_MODEL
O_RGG = O_RGV + RG_W
O_M = O_RGG + RG_W
P_IN = O_M + 2 * D_MODEL
M_PIECE = 2 * D_MODEL // (NB - 1)

WP_SC = 0
WP_RG = WP_SC + D_MODEL
WP_OUT = WP_RG + RG_W
WP_ROWS = WP_OUT + D_MODEL

PITCH_PAD = LANES
ROW_PAD = 8
WIN_CHUNK = 64
WIN_SLOTS = 2
WP_CHUNK = 256
WP_SLOTS = 2
WIN_COL_SCALES = ((0, O_SCG, 1.0), (O_SCG, O_RGV, 0.5), (O_RGV, O_RGG, 1.0), (O_RGG, P_IN, 0.5))

F32 = jnp.float32
BF16 = jnp.bfloat16


def _gate_win_start(c):
    lo = (c * GB) // RG_HEAD_DIM * RG_HEAD_DIM
    hi = -(-(c * GB + GB) // RG_HEAD_DIM) * RG_HEAD_DIM
    start = min(lo // GATE_SHIFT * GATE_SHIFT, RG_W - GATE_WIN)
    assert start <= lo and hi <= start + GATE_WIN
    return start


def _dot(a, b):
    return jnp.dot(a, b, preferred_element_type=F32)


def _half_silu_from_half(hz):
    return hz + hz * jnp.tanh(hz)


def _group_copies(hbm_ref, buf_ref, sem_ref, group, slot, to_hbm):
    copies = []
    for b in range(BATCH):
        hbm = hbm_ref.at[b, pl.ds(group * GROUP_T, GROUP_T), :]
        vmem = buf_ref.at[slot, :, b, :]
        src, dst = (vmem, hbm) if to_hbm else (hbm, vmem)
        copies.append(pltpu.make_async_copy(src, dst, sem_ref.at[slot]))
    return copies


def _block_kernel(x_hbm, c_ref, bada_ref, gnorm_ref, gfinal_ref, sccw_ref, sccb_ref, rgcw_ref, rgcb_ref,
                  gba_ref, gbx_ref, lam_ref, bm_ref, rgwa_ref, rgwx_ref, spread_ref, samehead_ref,
                  wada_hbm, win_hbm, scw_hbm, rgw_hbm, ow_hbm,
                  out_hbm,
                  mod_scr, wg_ref, win_ref, wproj_ref, stage_in, stage_proj, win_sem, w_sem,
                  xbuf, obuf, in_sem, out_sem,
                  h_tb, z0_scr, ubuf, s_scr, vbuf, hvf, vbf, vbs, a_scr, b_scr, yrg_scr, s2_scr, hstate):
    step = pl.program_id(0)
    xg_cur = lax.rem(step, XBUFS)
    xg_next = lax.rem(step + 1, XBUFS)
    og_cur = lax.rem(step, OBUFS)

    def in_copies(group):
        if isinstance(group, int):
            return _group_copies(x_hbm, xbuf, in_sem, min(group, NSTEPS - 1), group % XBUFS, to_hbm=False)
        return _group_copies(x_hbm, xbuf, in_sem, jnp.minimum(group, NSTEPS - 1),
                             lax.rem(group, XBUFS), to_hbm=False)

    def out_copies(group):
        return _group_copies(out_hbm, obuf, out_sem, group, lax.rem(group, OBUFS), to_hbm=True)

    def norm_in(x_slot, c, h_ref):
        x3 = xbuf[x_slot, c * TS:(c + 1) * TS]
        ms = jnp.mean(x3 * x3, axis=-1, keepdims=True)
        shift = mod_scr[:, 0:D_MODEL]
        scale1 = 1.0 + mod_scr[:, D_MODEL:2 * D_MODEL]
        hn = (x3 * lax.rsqrt(ms + EPS) * gnorm_ref[...]) * scale1 + shift
        h_ref[:, 0:D_MODEL] = hn.reshape(ROWS, D_MODEL).astype(BF16)

    def proj(h_ref, col0, width):
        return _dot(h_ref[:, 0:D_MODEL], win_ref[:, col0:col0 + width])

    def mixer_a_dots(h_ref, j):
        return tuple(proj(h_ref, o + j * CB, CB) for o in (O_SCB, O_SCC, O_SCV, O_SCG))

    def block0_dots_to_scratch(h_ref):
        for n, zn in enumerate(mixer_a_dots(h_ref, 0)):
            z0_scr[n, 0:ROWS, :] = zn

    def load_weights():
        n_win = D_MODEL // WIN_CHUNK

        def win_copy(i):
            slot = i % WIN_SLOTS if isinstance(i, int) else lax.rem(i, WIN_SLOTS)
            return pltpu.make_async_copy(win_hbm.at[pl.ds(i * WIN_CHUNK, WIN_CHUNK), :],
                                         stage_in.at[slot], win_sem.at[slot])

        for i in range(WIN_SLOTS - 1):
            win_copy(i).start()

        def win_body(i, carry):
            slot = lax.rem(i, WIN_SLOTS)

            @pl.when(i + WIN_SLOTS - 1 < n_win)
            def _():
                win_copy(i + WIN_SLOTS - 1).start()

            win_copy(i).wait()
            rows = pl.ds(pl.multiple_of(i * WIN_CHUNK, WIN_CHUNK), WIN_CHUNK)
            for c0, c1, scale in WIN_COL_SCALES:
                v = stage_in[slot, :, c0:c1]
                win_ref[rows, c0:c1] = (v if scale == 1.0 else scale * v).astype(BF16)
            return carry

        chunks = []
        for src, dst0, nrows, scale in ((scw_hbm, WP_SC, D_MODEL, 0.5), (rgw_hbm, WP_RG, RG_W, 0.5),
                                        (ow_hbm, WP_OUT, D_MODEL, 1.0)):
            chunks += [(src, r, dst0 + r, scale) for r in range(0, nrows, WP_CHUNK)]

        def proj_copy(n):
            src, r, _, _ = chunks[n]
            return pltpu.make_async_copy(src.at[r:r + WP_CHUNK, :], stage_proj.at[n % WP_SLOTS],
                                         w_sem.at[n % WP_SLOTS])

        def ada_copy(i):
            return pltpu.make_async_copy(wada_hbm.at[i * WIN_CHUNK:(i + 1) * WIN_CHUNK, :],
                                         stage_in.at[i % WIN_SLOTS, :, 0:3 * D_MODEL],
                                         win_sem.at[i % WIN_SLOTS])

        for n in range(WP_SLOTS - 1):
            proj_copy(n).start()
        lax.fori_loop(0, n_win, win_body, 0)
        for i in range(WIN_SLOTS - 1):
            ada_copy(i).start()

        for n, (_, _, dst, scale) in enumerate(chunks):
            if n + WP_SLOTS - 1 < len(chunks):
                proj_copy(n + WP_SLOTS - 1).start()
            proj_copy(n).wait()
            v = stage_proj[n % WP_SLOTS]
            wproj_ref[dst:dst + WP_CHUNK, 0:D_MODEL] = (v if scale == 1.0 else scale * v).astype(BF16)

        c = c_ref[...]
        c_act = (c * jax.nn.sigmoid(c)).astype(BF16)
        mod = jnp.broadcast_to(bada_ref[...], (BATCH, 3 * D_MODEL))
        for i in range(n_win):
            if i + WIN_SLOTS - 1 < n_win:
                ada_copy(i + WIN_SLOTS - 1).start()
            ada_copy(i).wait()
            w_chunk = stage_in[i % WIN_SLOTS, :, 0:3 * D_MODEL].astype(BF16)
            mod = mod + _dot(c_act[:, i * WIN_CHUNK:(i + 1) * WIN_CHUNK], w_chunk)
        mod_scr[...] = mod

        for cblk in range(NG):
            s0 = _gate_win_start(cblk)
            for half, w_ref in enumerate((rgwa_ref, rgwx_ref)):
                rows = w_ref[s0:s0 + GATE_WIN, :].astype(BF16)
                dense = _dot(rows, spread_ref[cblk]) * samehead_ref[cblk]
                wg_ref[cblk, :, half * GB:(half + 1) * GB] = dense.astype(BF16)

    @pl.when(step == 0)
    def _():
        ubuf[:, 0:SC_HIST, 0:CB] = jnp.zeros((NA, SC_HIST, CB), F32)
        vbuf[0:RG_HIST, 0:RG_W] = jnp.zeros((RG_HIST, RG_W), F32)
        hstate[...] = jnp.zeros_like(hstate)
        for g in range(2):
            for cp in in_copies(g):
                cp.start()
        load_weights()
        for cp in in_copies(0):
            cp.wait()
        norm_in(0, 0, h_tb.at[0])
        block0_dots_to_scratch(h_tb.at[0])

    @pl.when(step >= OBUFS)
    def _():
        for cp in out_copies(step - OBUFS):
            cp.wait()

    for cp in in_copies(step + 2):
        cp.start()
    for cp in in_copies(step + 1):
        cp.wait()

    def mixer_a_block(j, z):
        zb, zc, zv, hzg = z
        u = zc * zv
        ubuf[j, SC_HIST:SC_HIST + ROWS, 0:CB] = u
        cw = sccw_ref[:, j * CB:(j + 1) * CB]
        conv = sccb_ref[:, j * CB:(j + 1) * CB] + ubuf[j, 0:ROWS, 0:CB] * cw[0:1, :]
        conv = conv + ubuf[j, BATCH:BATCH + ROWS, 0:CB] * cw[1:2, :]
        conv = conv + u * cw[2:3, :]
        ubuf[j, 0:SC_HIST, 0:CB] = u[ROWS - SC_HIST:ROWS, :]
        s_scr[j] = (zb * conv * _half_silu_from_half(hzg)).astype(BF16)

    def gate_dot(c):
        start = _gate_win_start(c)
        slabs = vbf if start % LANES == 0 else vbs
        s0 = start // LANES
        lhs = jnp.concatenate([slabs[s0 + i] for i in range(GATE_WIN // LANES)], axis=1)
        return _dot(lhs, wg_ref[c])

    def gate_dots(h_cur, k):
        pieces = [gate_dot(c) for c in range(k * CB // GB, (k + 1) * CB // GB)]
        hg_a = jnp.concatenate([p[:, 0:GB] for p in pieces], axis=1)
        hg_x = jnp.concatenate([p[:, GB:2 * GB] for p in pieces], axis=1)
        return (hg_a, hg_x), proj(h_cur, O_RGG + k * CB, CB)

    def mixer_b_block(k, hg, hzg):
        hg_a, hg_x = hg
        cols = slice(k * CB, (k + 1) * CB)
        t_r = jnp.tanh(hg_a + 0.5 * gba_ref[:, cols])
        t_i = jnp.tanh(hg_x + 0.5 * gbx_ref[:, cols])
        half_k = (-0.5 * LRU_C) * jax.nn.softplus(-lam_ref[:, cols])
        log_a = half_k + half_k * t_r
        a = jnp.exp(log_a)
        a_scr[0:ROWS, :] = a
        one_minus_a2 = jnp.tanh(log_a) * (-1.0 - a * a)
        root = jnp.where(one_minus_a2 > 0.0, one_minus_a2 * lax.rsqrt(one_minus_a2), 0.0)
        hvk = hvf[k, 0:ROWS, :]
        b_scr[0:ROWS, :] = root * (hvk + hvk * t_i)
        h = hstate[k]
        for t in range(TS):
            rows = slice(t * BATCH, (t + 1) * BATCH)
            h = a_scr[rows, :] * h + b_scr[rows, :]
            yrg_scr[rows, :] = h
        hstate[k] = h
        s2_scr[k] = (yrg_scr[0:ROWS, :] * _half_silu_from_half(hzg)).astype(BF16)

    def process_tile(c):
        h_cur = h_tb.at[c % 2]
        h_next = h_tb.at[(c + 1) % 2]

        z = tuple(z0_scr[n, 0:ROWS, :] for n in range(4))
        for j in range(NA):
            z_next = mixer_a_dots(h_cur, j + 1) if j + 1 < NA else None
            if j == 0:
                if c + 1 < TPS:
                    norm_in(xg_cur, c + 1, h_next)
                else:
                    norm_in(xg_next, 0, h_next)
            mixer_a_block(j, z)
            z = z_next

        vz = proj(h_cur, O_RGV, RG_W)
        m_pieces = [proj(h_cur, O_M, M_PIECE)]
        s_all = jnp.concatenate([s_scr[j] for j in range(NA)], axis=1)
        ya_half = _dot(s_all, wproj_ref[WP_SC:WP_SC + D_MODEL, 0:D_MODEL])
        vbuf[RG_HIST:RG_HIST + ROWS, 0:RG_W] = vz
        rcw = 0.5 * rgcw_ref[...]
        hv = 0.5 * rgcb_ref[...] + vbuf[0:ROWS, 0:RG_W] * rcw[0:1, :]
        hv = hv + vbuf[BATCH:BATCH + ROWS, 0:RG_W] * rcw[1:2, :]
        hv = hv + vbuf[2 * BATCH:2 * BATCH + ROWS, 0:RG_W] * rcw[2:3, :]
        hv = hv + vz * rcw[3:4, :]
        vbuf[0:RG_HIST, 0:RG_W] = vz[ROWS - RG_HIST:ROWS, :]
        for cidx in range(NG):
            vbf[cidx] = hv[:, cidx * LANES:(cidx + 1) * LANES].astype(BF16)
        for cidx in range(NG - 1):
            lo = GATE_SHIFT + cidx * LANES
            vbs[cidx] = hv[:, lo:lo + LANES].astype(BF16)
        for k in range(NB):
            hvf[k, 0:ROWS, :] = hv[:, k * CB:(k + 1) * CB]

        hg, hzg = gate_dots(h_cur, 0)
        for k in range(NB):
            hg_next, hzg_next = gate_dots(h_cur, k + 1) if k + 1 < NB else (None, None)
            if 1 <= k < NB - 1:
                m_pieces.append(proj(h_cur, O_M + k * M_PIECE, M_PIECE))
            mixer_b_block(k, hg, hzg)
            hg, hzg = hg_next, hzg_next
        m = jnp.concatenate(m_pieces, axis=1)

        s2_all = jnp.concatenate([s2_scr[k] for k in range(NB)], axis=1)
        yb_half = _dot(s2_all, wproj_ref[WP_RG:WP_RG + RG_W, 0:D_MODEL])

        t_a = jnp.tanh(m[:, 0:D_MODEL] + 0.5 * bm_ref[0:1, :])
        t_b = jnp.tanh(m[:, D_MODEL:2 * D_MODEL] + 0.5 * bm_ref[1:2, :])
        merged = ((ya_half + ya_half * t_a) + (yb_half + yb_half * t_b)).astype(BF16)
        y3 = _dot(merged, wproj_ref[WP_OUT:WP_OUT + D_MODEL, 0:D_MODEL]).reshape(TS, BATCH, D_MODEL)
        block0_dots_to_scratch(h_next)
        xo = xbuf[xg_cur, c * TS:(c + 1) * TS] + mod_scr[:, 2 * D_MODEL:3 * D_MODEL] * y3
        ms = jnp.mean(xo * xo, axis=-1, keepdims=True)
        obuf[og_cur, c * TS:(c + 1) * TS] = xo * lax.rsqrt(ms + EPS) * gfinal_ref[...]

    for c in range(TPS):
        process_tile(c)

    for cp in out_copies(step):
        cp.start()

    @pl.when(step == NSTEPS - 1)
    def _():
        for cp in in_copies(step + 2):
            cp.wait()
        for cp in out_copies(step - 1):
            cp.wait()
        for cp in out_copies(step):
            cp.wait()


def _const_spec(shape):
    nd = len(shape)
    return pl.BlockSpec(shape, lambda i, _nd=nd: (0,) * _nd, pipeline_mode=pl.Buffered(1))


def _gate_layout_constants():
    starts = np.array([_gate_win_start(c) for c in range(NG)])
    row_ch = starts[:, None] + np.arange(GATE_WIN)[None, :]
    col_ch = np.arange(NG)[:, None] * GB + np.arange(GB)[None, :]
    same_head = (row_ch // RG_HEAD_DIM)[:, :, None] == (col_ch // RG_HEAD_DIM)[:, None, :]
    spread = np.arange(RG_HEAD_DIM)[None, :, None] == (col_ch % RG_HEAD_DIM)[:, None, :]
    return jnp.asarray(spread, BF16), jnp.asarray(same_head, F32)


def kernel(x, c, w_ada, b_ada, g_norm, w_in, sc_conv_w, sc_conv_b, sc_w_out, rg_conv_w, rg_conv_b,
           rg_w_a, rg_b_a, rg_w_x, rg_b_x, rg_lambda, rg_w_out, b_merge, w_out, g_final):
    assert x.shape == (BATCH, SEQ, D_MODEL) and w_ada.shape[0] == 1
    l = 0
    spread, same_head = _gate_layout_constants()

    vmem_operands = (
        c, b_ada[l].reshape(1, 3 * D_MODEL), g_norm[l].reshape(1, D_MODEL), g_final.reshape(1, D_MODEL),
        sc_conv_w[l], sc_conv_b[l].reshape(1, D_MODEL), rg_conv_w[l], rg_conv_b[l].reshape(1, RG_W),
        rg_b_a[l].reshape(1, RG_W), rg_b_x[l].reshape(1, RG_W), rg_lambda[l].reshape(1, RG_W), b_merge[l],
        rg_w_a[l].reshape(RG_W, RG_HEAD_DIM), rg_w_x[l].reshape(RG_W, RG_HEAD_DIM), spread, same_head)
    hbm_weights = (w_ada[l], w_in[l], sc_w_out[l], rg_w_out[l], w_out[l])
    operands = (x,) + vmem_operands + hbm_weights
    any_spec = pl.BlockSpec(memory_space=pl.ANY)
    in_specs = [any_spec] + [_const_spec(op.shape) for op in vmem_operands] + [any_spec] * len(hbm_weights)
    scratch = [
        pltpu.VMEM((BATCH, 3 * D_MODEL), F32),
        pltpu.VMEM((NG, GATE_WIN, 2 * GB), BF16),
        pltpu.VMEM((D_MODEL, P_IN + PITCH_PAD), BF16),
        pltpu.VMEM((WP_ROWS, D_MODEL + PITCH_PAD), BF16),
        pltpu.VMEM((WIN_SLOTS, WIN_CHUNK, P_IN), F32),
        pltpu.VMEM((WP_SLOTS, WP_CHUNK, D_MODEL), F32),
        pltpu.SemaphoreType.DMA((WIN_SLOTS,)),
        pltpu.SemaphoreType.DMA((WP_SLOTS,)),
        pltpu.VMEM((XBUFS, GROUP_T, BATCH, D_MODEL), F32),
        pltpu.VMEM((OBUFS, GROUP_T, BATCH, D_MODEL), F32),
        pltpu.SemaphoreType.DMA((XBUFS,)),
        pltpu.SemaphoreType.DMA((OBUFS,)),
        pltpu.VMEM((2, ROWS, D_MODEL + PITCH_PAD), BF16),
        pltpu.VMEM((4, ROWS + ROW_PAD, CB), F32),
        pltpu.VMEM((NA, SC_HIST + ROWS, CB + PITCH_PAD), F32),
        pltpu.VMEM((NA, ROWS, CB), BF16),
        pltpu.VMEM((RG_HIST + ROWS, RG_W + PITCH_PAD), F32),
        pltpu.VMEM((NB, ROWS + ROW_PAD, CB), F32),
        pltpu.VMEM((NG, ROWS, LANES), BF16),
        pltpu.VMEM((NG - 1, ROWS, LANES), BF16),
        pltpu.VMEM((ROWS + ROW_PAD, CB), F32),
        pltpu.VMEM((ROWS + ROW_PAD, CB), F32),
        pltpu.VMEM((ROWS + ROW_PAD, CB), F32),
        pltpu.VMEM((NB, ROWS, CB), BF16),
        pltpu.VMEM((NB, BATCH, CB), F32),
    ]
    return pl.pallas_call(
        _block_kernel,
        grid=(NSTEPS,),
        in_specs=in_specs,
        out_specs=any_spec,
        out_shape=jax.ShapeDtypeStruct((BATCH, SEQ, D_MODEL), F32),
        scratch_shapes=scratch,
        compiler_params=pltpu.CompilerParams(
            dimension_semantics=("arbitrary",), vmem_limit_bytes=VMEM_LIMIT_BYTES),
        name="hybrid_block",
    )(*operands)
```

```python
import jax
import jax.numpy as jnp
import numpy as np
from jax import lax
from jax.experimental import pallas as pl
from jax.experimental.pallas import tpu as pltpu

D_MODEL = 1024
BATCH = 16
SEQ = 2048
SC_K = 3
RG_W = 1280
RG_HEADS = 16
RG_HEAD_DIM = RG_W // RG_HEADS
RG_K = 4
LRU_C = 8.0
EPS = 1e-6

LANES = 128
MXU_DIM = 256
V7X_VMEM_BYTES = 64 * 1024 * 1024
TS = 16
TPS = 2
NTILES = SEQ // TS
NSTEPS = NTILES // TPS
assert TPS % 2 == 0
ROWS = TS * BATCH
GROUP_T = TPS * TS
XBUFS = 3
OBUFS = 2
CB = MXU_DIM
NA = D_MODEL // CB
NB = RG_W // CB
GB = LANES
NG = RG_W // GB
GATE_WIN = MXU_DIM
GATE_SHIFT = LANES // 2
SC_HIST = (SC_K - 1) * BATCH
RG_HIST = (RG_K - 1) * BATCH
VMEM_LIMIT_BYTES = V7X_VMEM_BYTES - 4 * 1024 * 1024

O_SCB, O_SCC, O_SCV, O_SCG = 0, D_MODEL, 2 * D_MODEL, 3 * D_MODEL
O_RGV = 4 * D_MODEL
O_RGG = O_RGV + RG_W
O_M = O_RGG + RG_W
P_IN = O_M + 2 * D_MODEL
M_PIECE = 2 * D_MODEL // (NB - 1)

WP_SC = 0
WP_RG = WP_SC + D_MODEL
WP_OUT = WP_RG + RG_W
WP_ROWS = WP_OUT + D_MODEL

PITCH_PAD = LANES
WIN_CHUNK = 64
WIN_SLOTS = 3
WP_CHUNK = 256
WP_SLOTS = 2
WIN_COL_SCALES = ((0, O_SCG, 1.0), (O_SCG, O_RGV, 0.5), (O_RGV, O_RGG, 1.0), (O_RGG, P_IN, 0.5))

F32 = jnp.float32
BF16 = jnp.bfloat16


def _gate_win_start(c):
    lo = (c * GB) // RG_HEAD_DIM * RG_HEAD_DIM
    hi = -(-(c * GB + GB) // RG_HEAD_DIM) * RG_HEAD_DIM
    start = min(lo // GATE_SHIFT * GATE_SHIFT, RG_W - GATE_WIN)
    assert start <= lo and hi <= start + GATE_WIN
    return start


def _dot(a, b):
    return jnp.dot(a, b, preferred_element_type=F32)


def _half_silu_from_half(hz):
    return hz + hz * jnp.tanh(hz)


def _group_copies(hbm_ref, buf_ref, sem_ref, group, slot, to_hbm):
    copies = []
    for b in range(BATCH):
        hbm = hbm_ref.at[b, pl.ds(group * GROUP_T, GROUP_T), :]
        vmem = buf_ref.at[slot, :, b, :]
        src, dst = (vmem, hbm) if to_hbm else (hbm, vmem)
        copies.append(pltpu.make_async_copy(src, dst, sem_ref.at[slot]))
    return copies


def _block_kernel(x_hbm, c_ref, bada_ref, gnorm_ref, gfinal_ref, sccw_ref, sccb_ref, rgcw_ref, rgcb_ref,
                  gba_ref, gbx_ref, lam_ref, bm_ref, rgwa_ref, rgwx_ref, spread_ref, samehead_ref,
                  wada_hbm, win_hbm, scw_hbm, rgw_hbm, ow_hbm,
                  out_hbm,
                  mod_scr, wg_ref, win_ref, wproj_ref, stage_in, stage_proj, win_sem, w_sem,
                  xbuf, obuf, in_sem, out_sem,
                  h_tb, z0_scr, ubuf, s_scr, vbuf, hvf, vbf, vbs, a_scr, b_scr, yrg_scr, s2_scr, hstate):
    step = pl.program_id(0)
    xg_cur = lax.rem(step, XBUFS)
    xg_next = lax.rem(step + 1, XBUFS)
    og_cur = lax.rem(step, OBUFS)

    def in_copies(group):
        if isinstance(group, int):
            return _group_copies(x_hbm, xbuf, in_sem, min(group, NSTEPS - 1), group % XBUFS, to_hbm=False)
        return _group_copies(x_hbm, xbuf, in_sem, jnp.minimum(group, NSTEPS - 1),
                             lax.rem(group, XBUFS), to_hbm=False)

    def out_copies(group):
        return _group_copies(out_hbm, obuf, out_sem, group, lax.rem(group, OBUFS), to_hbm=True)

    def norm_in(x_slot, c, h_ref):
        x3 = xbuf[x_slot, c * TS:(c + 1) * TS]
        ms = jnp.mean(x3 * x3, axis=-1, keepdims=True)
        shift = mod_scr[:, 0:D_MODEL]
        scale1 = 1.0 + mod_scr[:, D_MODEL:2 * D_MODEL]
        hn = (x3 * lax.rsqrt(ms + EPS) * gnorm_ref[...]) * scale1 + shift
        h_ref[:, 0:D_MODEL] = hn.reshape(ROWS, D_MODEL).astype(BF16)

    def proj(h_ref, col0, width):
        return _dot(h_ref[:, 0:D_MODEL], win_ref[:, col0:col0 + width])

    def mixer_a_dots(h_ref, j):
        return tuple(proj(h_ref, o + j * CB, CB) for o in (O_SCB, O_SCC, O_SCV, O_SCG))

    def block0_dots_to_scratch(h_ref):
        for n, zn in enumerate(mixer_a_dots(h_ref, 0)):
            z0_scr[n] = zn

    def load_weights():
        n_win = D_MODEL // WIN_CHUNK

        def win_copy(i):
            slot = i % WIN_SLOTS if isinstance(i, int) else lax.rem(i, WIN_SLOTS)
            return pltpu.make_async_copy(win_hbm.at[pl.ds(i * WIN_CHUNK, WIN_CHUNK), :],
                                         stage_in.at[slot], win_sem.at[slot])

        for i in range(WIN_SLOTS - 1):
            win_copy(i).start()

        def win_body(i, carry):
            slot = lax.rem(i, WIN_SLOTS)

            @pl.when(i + WIN_SLOTS - 1 < n_win)
            def _():
                win_copy(i + WIN_SLOTS - 1).start()

            win_copy(i).wait()
            rows = pl.ds(pl.multiple_of(i * WIN_CHUNK, WIN_CHUNK), WIN_CHUNK)
            for c0, c1, scale in WIN_COL_SCALES:
                v = stage_in[slot, :, c0:c1]
                win_ref[rows, c0:c1] = (v if scale == 1.0 else scale * v).astype(BF16)
            return carry

        chunks = []
        for src, dst0, nrows, scale in ((scw_hbm, WP_SC, D_MODEL, 0.5), (rgw_hbm, WP_RG, RG_W, 0.5),
                                        (ow_hbm, WP_OUT, D_MODEL, 1.0)):
            chunks += [(src, r, dst0 + r, scale) for r in range(0, nrows, WP_CHUNK)]

        def proj_copy(n):
            src, r, _, _ = chunks[n]
            return pltpu.make_async_copy(src.at[r:r + WP_CHUNK, :], stage_proj.at[n % WP_SLOTS],
                                         w_sem.at[n % WP_SLOTS])

        def ada_copy(i):
            return pltpu.make_async_copy(wada_hbm.at[i * WIN_CHUNK:(i + 1) * WIN_CHUNK, :],
                                         stage_in.at[i % WIN_SLOTS, :, 0:3 * D_MODEL],
                                         win_sem.at[i % WIN_SLOTS])

        for n in range(WP_SLOTS - 1):
            proj_copy(n).start()
        lax.fori_loop(0, n_win, win_body, 0)
        for i in range(WIN_SLOTS - 1):
            ada_copy(i).start()

        for n, (_, _, dst, scale) in enumerate(chunks):
            if n + WP_SLOTS - 1 < len(chunks):
                proj_copy(n + WP_SLOTS - 1).start()
            proj_copy(n).wait()
            v = stage_proj[n % WP_SLOTS]
            wproj_ref[dst:dst + WP_CHUNK, 0:D_MODEL] = (v if scale == 1.0 else scale * v).astype(BF16)

        c = c_ref[...]
        c_act = (c * jax.nn.sigmoid(c)).astype(BF16)
        mod = jnp.broadcast_to(bada_ref[...], (BATCH, 3 * D_MODEL))
        for i in range(n_win):
            if i + WIN_SLOTS - 1 < n_win:
                ada_copy(i + WIN_SLOTS - 1).start()
            ada_copy(i).wait()
            w_chunk = stage_in[i % WIN_SLOTS, :, 0:3 * D_MODEL].astype(BF16)
            mod = mod + _dot(c_act[:, i * WIN_CHUNK:(i + 1) * WIN_CHUNK], w_chunk)
        mod_scr[...] = mod

        for cblk in range(NG):
            s0 = _gate_win_start(cblk)
            for half, w_ref in enumerate((rgwa_ref, rgwx_ref)):
                rows = w_ref[s0:s0 + GATE_WIN, :].astype(BF16)
                dense = _dot(rows, spread_ref[cblk]) * samehead_ref[cblk].astype(F32)
                wg_ref[cblk, :, half * GB:(half + 1) * GB] = dense.astype(BF16)

    @pl.when(step == 0)
    def _():
        ubuf[:, 0:SC_HIST, :] = jnp.zeros((NA, SC_HIST, CB), F32)
        vbuf[0:RG_HIST, :] = jnp.zeros((RG_HIST, RG_W), F32)
        hstate[...] = jnp.zeros_like(hstate)
        for g in range(2):
            for cp in in_copies(g):
                cp.start()
        load_weights()
        for cp in in_copies(0):
            cp.wait()
        norm_in(0, 0, h_tb.at[0])
        block0_dots_to_scratch(h_tb.at[0])

    @pl.when(step >= OBUFS)
    def _():
        for cp in out_copies(step - OBUFS):
            cp.wait()

    for cp in in_copies(step + 2):
        cp.start()
    for cp in in_copies(step + 1):
        cp.wait()

    def mixer_a_block(j, z):
        zb, zc, zv, hzg = z
        u = zc * zv
        ubuf[j, SC_HIST:SC_HIST + ROWS, :] = u
        cw = sccw_ref[:, j * CB:(j + 1) * CB]
        conv = sccb_ref[:, j * CB:(j + 1) * CB] + ubuf[j, 0:ROWS, :] * cw[0:1, :]
        conv = conv + ubuf[j, BATCH:BATCH + ROWS, :] * cw[1:2, :]
        conv = conv + u * cw[2:3, :]
        ubuf[j, 0:SC_HIST, :] = u[ROWS - SC_HIST:ROWS, :]
        s_scr[j] = (zb * conv * _half_silu_from_half(hzg)).astype(BF16)

    def gate_dot(c):
        start = _gate_win_start(c)
        slabs = vbf if start % LANES == 0 else vbs
        s0 = start // LANES
        lhs = jnp.concatenate([slabs[s0 + i] for i in range(GATE_WIN // LANES)], axis=1)
        return _dot(lhs, wg_ref[c])

    def gate_dots(h_cur, k):
        pieces = [gate_dot(c) for c in range(k * CB // GB, (k + 1) * CB // GB)]
        hg_a = jnp.concatenate([p[:, 0:GB] for p in pieces], axis=1)
        hg_x = jnp.concatenate([p[:, GB:2 * GB] for p in pieces], axis=1)
        return (hg_a, hg_x), proj(h_cur, O_RGG + k * CB, CB)

    def mixer_b_block(k, hg, hzg):
        hg_a, hg_x = hg
        cols = slice(k * CB, (k + 1) * CB)
        t_r = jnp.tanh(hg_a + 0.5 * gba_ref[:, cols])
        t_i = jnp.tanh(hg_x + 0.5 * gbx_ref[:, cols])
        half_k = (-0.5 * LRU_C) * jax.nn.softplus(-lam_ref[:, cols])
        log_a = half_k + half_k * t_r
        a = jnp.exp(log_a)
        a_scr[...] = a
        one_minus_a2 = jnp.tanh(log_a) * (-1.0 - a * a)
        root = jnp.where(one_minus_a2 > 0.0, one_minus_a2 * lax.rsqrt(one_minus_a2), 0.0)
        hvk = hvf[k]
        b_scr[...] = root * (hvk + hvk * t_i)
        h = hstate[k]
        for t in range(TS):
            rows = slice(t * BATCH, (t + 1) * BATCH)
            h = a_scr[rows, :] * h + b_scr[rows, :]
            yrg_scr[rows, :] = h
        hstate[k] = h
        s2_scr[k] = (yrg_scr[...] * _half_silu_from_half(hzg)).astype(BF16)

    def process_tile(c):
        h_cur = h_tb.at[c % 2]
        h_next = h_tb.at[(c + 1) % 2]

        z = tuple(z0_scr[n] for n in range(4))
        for j in range(NA):
            z_next = mixer_a_dots(h_cur, j + 1) if j + 1 < NA else None
            if j == 0:
                if c + 1 < TPS:
                    norm_in(xg_cur, c + 1, h_next)
                else:
                    norm_in(xg_next, 0, h_next)
            mixer_a_block(j, z)
            z = z_next

        vz = proj(h_cur, O_RGV, RG_W)
        m_pieces = [proj(h_cur, O_M, M_PIECE)]
        s_all = jnp.concatenate([s_scr[j] for j in range(NA)], axis=1)
        ya_half = _dot(s_all, wproj_ref[WP_SC:WP_SC + D_MODEL, 0:D_MODEL])
        vbuf[RG_HIST:RG_HIST + ROWS, :] = vz
        rcw = 0.5 * rgcw_ref[...]
        hv = 0.5 * rgcb_ref[...] + vbuf[0:ROWS, :] * rcw[0:1, :]
        hv = hv + vbuf[BATCH:BATCH + ROWS, :] * rcw[1:2, :]
        hv = hv + vbuf[2 * BATCH:2 * BATCH + ROWS, :] * rcw[2:3, :]
        hv = hv + vz * rcw[3:4, :]
        vbuf[0:RG_HIST, :] = vz[ROWS - RG_HIST:ROWS, :]
        for cidx in range(NG):
            vbf[cidx] = hv[:, cidx * LANES:(cidx + 1) * LANES].astype(BF16)
        for cidx in range(NG - 1):
            lo = GATE_SHIFT + cidx * LANES
            vbs[cidx] = hv[:, lo:lo + LANES].astype(BF16)
        for k in range(NB):
            hvf[k] = hv[:, k * CB:(k + 1) * CB]

        hg, hzg = gate_dots(h_cur, 0)
        for k in range(NB):
            hg_next, hzg_next = gate_dots(h_cur, k + 1) if k + 1 < NB else (None, None)
            if 1 <= k < NB - 1:
                m_pieces.append(proj(h_cur, O_M + k * M_PIECE, M_PIECE))
            mixer_b_block(k, hg, hzg)
            hg, hzg = hg_next, hzg_next
        m = jnp.concatenate(m_pieces, axis=1)

        s2_all = jnp.concatenate([s2_scr[k] for k in range(NB)], axis=1)
        yb_half = _dot(s2_all, wproj_ref[WP_RG:WP_RG + RG_W, 0:D_MODEL])

        t_a = jnp.tanh(m[:, 0:D_MODEL] + 0.5 * bm_ref[0:1, :])
        t_b = jnp.tanh(m[:, D_MODEL:2 * D_MODEL] + 0.5 * bm_ref[1:2, :])
        merged = ((ya_half + ya_half * t_a) + (yb_half + yb_half * t_b)).astype(BF16)
        y3 = _dot(merged, wproj_ref[WP_OUT:WP_OUT + D_MODEL, 0:D_MODEL]).reshape(TS, BATCH, D_MODEL)
        block0_dots_to_scratch(h_next)
        xo = xbuf[xg_cur, c * TS:(c + 1) * TS] + mod_scr[:, 2 * D_MODEL:3 * D_MODEL] * y3
        ms = jnp.mean(xo * xo, axis=-1, keepdims=True)
        obuf[og_cur, c * TS:(c + 1) * TS] = xo * lax.rsqrt(ms + EPS) * gfinal_ref[...]

    for c in range(TPS):
        process_tile(c)

    for cp in out_copies(step):
        cp.start()

    @pl.when(step == NSTEPS - 1)
    def _():
        for cp in in_copies(step + 2):
            cp.wait()
        for cp in out_copies(step - 1):
            cp.wait()
        for cp in out_copies(step):
            cp.wait()


def _const_spec(shape):
    nd = len(shape)
    return pl.BlockSpec(shape, lambda i, _nd=nd: (0,) * _nd, pipeline_mode=pl.Buffered(1))


def _gate_layout_constants():
    starts = np.array([_gate_win_start(c) for c in range(NG)])
    row_ch = starts[:, None] + np.arange(GATE_WIN)[None, :]
    col_ch = np.arange(NG)[:, None] * GB + np.arange(GB)[None, :]
    same_head = (row_ch // RG_HEAD_DIM)[:, :, None] == (col_ch // RG_HEAD_DIM)[:, None, :]
    spread = np.arange(RG_HEAD_DIM)[None, :, None] == (col_ch % RG_HEAD_DIM)[:, None, :]
    return jnp.asarray(spread, BF16), jnp.asarray(same_head, BF16)


def kernel(x, c, w_ada, b_ada, g_norm, w_in, sc_conv_w, sc_conv_b, sc_w_out, rg_conv_w, rg_conv_b,
           rg_w_a, rg_b_a, rg_w_x, rg_b_x, rg_lambda, rg_w_out, b_merge, w_out, g_final):
    assert x.shape == (BATCH, SEQ, D_MODEL) and w_ada.shape[0] == 1
    l = 0
    spread, same_head = _gate_layout_constants()

    vmem_operands = (
        c, b_ada[l].reshape(1, 3 * D_MODEL), g_norm[l].reshape(1, D_MODEL), g_final.reshape(1, D_MODEL),
        sc_conv_w[l], sc_conv_b[l].reshape(1, D_MODEL), rg_conv_w[l], rg_conv_b[l].reshape(1, RG_W),
        rg_b_a[l].reshape(1, RG_W), rg_b_x[l].reshape(1, RG_W), rg_lambda[l].reshape(1, RG_W), b_merge[l],
        rg_w_a[l].reshape(RG_W, RG_HEAD_DIM), rg_w_x[l].reshape(RG_W, RG_HEAD_DIM), spread, same_head)
    hbm_weights = (w_ada[l], w_in[l], sc_w_out[l], rg_w_out[l], w_out[l])
    operands = (x,) + vmem_operands + hbm_weights
    any_spec = pl.BlockSpec(memory_space=pl.ANY)
    in_specs = [any_spec] + [_const_spec(op.shape) for op in vmem_operands] + [any_spec] * len(hbm_weights)
    scratch = [
        pltpu.VMEM((BATCH, 3 * D_MODEL), F32),
        pltpu.VMEM((NG, GATE_WIN, 2 * GB), BF16),
        pltpu.VMEM((D_MODEL, P_IN + PITCH_PAD), BF16),
        pltpu.VMEM((WP_ROWS, D_MODEL + PITCH_PAD), BF16),
        pltpu.VMEM((WIN_SLOTS, WIN_CHUNK, P_IN), F32),
        pltpu.VMEM((WP_SLOTS, WP_CHUNK, D_MODEL), F32),
        pltpu.SemaphoreType.DMA((WIN_SLOTS,)),
        pltpu.SemaphoreType.DMA((WP_SLOTS,)),
        pltpu.VMEM((XBUFS, GROUP_T, BATCH, D_MODEL), F32),
        pltpu.VMEM((OBUFS, GROUP_T, BATCH, D_MODEL), F32),
        pltpu.SemaphoreType.DMA((XBUFS,)),
        pltpu.SemaphoreType.DMA((OBUFS,)),
        pltpu.VMEM((2, ROWS, D_MODEL + PITCH_PAD), BF16),
        pltpu.VMEM((4, ROWS, CB), F32),
        pltpu.VMEM((NA, SC_HIST + ROWS, CB), F32),
        pltpu.VMEM((NA, ROWS, CB), BF16),
        pltpu.VMEM((RG_HIST + ROWS, RG_W), F32),
        pltpu.VMEM((NB, ROWS, CB), F32),
        pltpu.VMEM((NG, ROWS, LANES), BF16),
        pltpu.VMEM((NG - 1, ROWS, LANES), BF16),
        pltpu.VMEM((ROWS, CB), F32),
        pltpu.VMEM((ROWS, CB), F32),
        pltpu.VMEM((ROWS, CB), F32),
        pltpu.VMEM((NB, ROWS, CB), BF16),
        pltpu.VMEM((NB, BATCH, CB), F32),
    ]
    return pl.pallas_call(
        _block_kernel,
        grid=(NSTEPS,),
        in_specs=in_specs,
        out_specs=any_spec,
        out_shape=jax.ShapeDtypeStruct((BATCH, SEQ, D_MODEL), F32),
        scratch_shapes=scratch,
        compiler_params=pltpu.CompilerParams(
            dimension_semantics=("arbitrary",), vmem_limit_bytes=VMEM_LIMIT_BYTES),
        name="hybrid_block",
    )(*operands)
```

```python
import jax
import jax.numpy as jnp
import numpy as np
from jax import lax
from jax.experimental import pallas as pl
from jax.experimental.pallas import tpu as pltpu

D_MODEL = 1024
BATCH = 16
SEQ = 2048
SC_K = 3
RG_W = 1280
RG_HEADS = 16
RG_HEAD_DIM = RG_W // RG_HEADS
RG_K = 4
LRU_C = 8.0
EPS = 1e-6

LANES = 128
MXU_DIM = 256
V7X_VMEM_BYTES = 64 * 1024 * 1024
TS = 16
TPS = 2
NTILES = SEQ // TS
NSTEPS = NTILES // TPS
assert TPS % 2 == 0
ROWS = TS * BATCH
GROUP_T = TPS * TS
XBUFS = 3
OBUFS = 2
CB = MXU_DIM
NA = D_MODEL // CB
NB = RG_W // CB
GB = LANES
NG = RG_W // GB
GATE_WIN = MXU_DIM
GATE_SHIFT = LANES // 2
SC_HIST = (SC_K - 1) * BATCH
RG_HIST = (RG_K - 1) * BATCH
VMEM_LIMIT_BYTES = V7X_VMEM_BYTES - 4 * 1024 * 1024

O_SCB, O_SCC, O_SCV, O_SCG = 0, D_MODEL, 2 * D_MODEL, 3 * D_MODEL
O_RGV = 4 * D_MODEL
O_RGG = O_RGV + RG_W
O_M = O_RGG + RG_W
P_IN = O_M + 2 * D_MODEL
M_PIECE = 2 * D_MODEL // (NB - 1)

WP_SC = 0
WP_RG = WP_SC + D_MODEL
WP_OUT = WP_RG + RG_W
WP_ROWS = WP_OUT + D_MODEL

PITCH_PAD = 2 * LANES

RGP_CW, RGP_CB, RGP_BA, RGP_BX, RGP_LAM = 0, RG_K, RG_K + 1, RG_K + 2, RG_K + 3
WIN_CHUNK = 64
WIN_SLOTS = 2
WP_CHUNK = 256
WP_SLOTS = 2
WIN_COL_SCALES = ((0, O_SCG, 1.0), (O_SCG, O_RGV, 0.5), (O_RGV, O_RGG, 1.0), (O_RGG, P_IN, 0.5))

F32 = jnp.float32
BF16 = jnp.bfloat16


def _gate_win_start(c):
    lo = (c * GB) // RG_HEAD_DIM * RG_HEAD_DIM
    hi = -(-(c * GB + GB) // RG_HEAD_DIM) * RG_HEAD_DIM
    start = min(lo // GATE_SHIFT * GATE_SHIFT, RG_W - GATE_WIN)
    assert start <= lo and hi <= start + GATE_WIN
    return start


def _dot(a, b):
    return jnp.dot(a, b, preferred_element_type=F32)


def _half_silu_from_half(hz):
    return hz + hz * jnp.tanh(hz)


def _group_copies(hbm_ref, buf_ref, sem_ref, group, slot, to_hbm):
    copies = []
    for b in range(BATCH):
        hbm = hbm_ref.at[b, pl.ds(group * GROUP_T, GROUP_T), :]
        vmem = buf_ref.at[slot, :, b, :]
        src, dst = (vmem, hbm) if to_hbm else (hbm, vmem)
        copies.append(pltpu.make_async_copy(src, dst, sem_ref.at[slot]))
    return copies


def _block_kernel(x_hbm, c_ref, bada_ref, gnorm_ref, gfinal_ref, sccw_ref, sccb_ref,
                  bm_ref, rgwa_ref, rgwx_ref, spread_ref, samehead_ref,
                  rgcw_hbm, rgcb_hbm, gba_hbm, gbx_hbm, lam_hbm,
                  wada_hbm, win_hbm, scw_hbm, rgw_hbm, ow_hbm,
                  out_hbm,
                  rgp, rgp_sem, mod_scr, wg_ref, win_ref, wproj_ref, stage_in, stage_proj, win_sem, w_sem,
                  xbuf, obuf, in_sem, out_sem,
                  h_tb, z0_scr, ubuf, s_scr, vbuf, hvf, vbf, vbs, a_scr, b_scr, yrg_scr, s2_scr, hstate):
    step = pl.program_id(0)
    xg_cur = lax.rem(step, XBUFS)
    xg_next = lax.rem(step + 1, XBUFS)
    og_cur = lax.rem(step, OBUFS)

    def in_copies(group):
        if isinstance(group, int):
            return _group_copies(x_hbm, xbuf, in_sem, min(group, NSTEPS - 1), group % XBUFS, to_hbm=False)
        return _group_copies(x_hbm, xbuf, in_sem, jnp.minimum(group, NSTEPS - 1),
                             lax.rem(group, XBUFS), to_hbm=False)

    def out_copies(group):
        return _group_copies(out_hbm, obuf, out_sem, group, lax.rem(group, OBUFS), to_hbm=True)

    def norm_in(x_slot, c, h_ref):
        x3 = xbuf[x_slot, c * TS:(c + 1) * TS]
        ms = jnp.mean(x3 * x3, axis=-1, keepdims=True)
        shift = mod_scr[:, 0:D_MODEL]
        scale1 = 1.0 + mod_scr[:, D_MODEL:2 * D_MODEL]
        hn = (x3 * lax.rsqrt(ms + EPS) * gnorm_ref[...]) * scale1 + shift
        h_ref[:, 0:D_MODEL] = hn.reshape(ROWS, D_MODEL).astype(BF16)

    def proj(h_ref, col0, width):
        return _dot(h_ref[:, 0:D_MODEL], win_ref[:, col0:col0 + width])

    def mixer_a_dots(h_ref, j):
        return tuple(proj(h_ref, o + j * CB, CB) for o in (O_SCB, O_SCC, O_SCV, O_SCG))

    def block0_dots_to_scratch(h_ref):
        for n, zn in enumerate(mixer_a_dots(h_ref, 0)):
            z0_scr[n] = zn

    def load_weights():
        n_win = D_MODEL // WIN_CHUNK

        def win_copy(i):
            slot = i % WIN_SLOTS if isinstance(i, int) else lax.rem(i, WIN_SLOTS)
            return pltpu.make_async_copy(win_hbm.at[pl.ds(i * WIN_CHUNK, WIN_CHUNK), :],
                                         stage_in.at[slot], win_sem.at[slot])

        for i in range(WIN_SLOTS - 1):
            win_copy(i).start()

        def win_body(i, carry):
            slot = lax.rem(i, WIN_SLOTS)

            @pl.when(i + WIN_SLOTS - 1 < n_win)
            def _():
                win_copy(i + WIN_SLOTS - 1).start()

            win_copy(i).wait()
            rows = pl.ds(pl.multiple_of(i * WIN_CHUNK, WIN_CHUNK), WIN_CHUNK)
            for c0, c1, scale in WIN_COL_SCALES:
                v = stage_in[slot, :, c0:c1]
                win_ref[rows, c0:c1] = (v if scale == 1.0 else scale * v).astype(BF16)
            return carry

        chunks = []
        for src, dst0, nrows, scale in ((scw_hbm, WP_SC, D_MODEL, 0.5), (rgw_hbm, WP_RG, RG_W, 0.5),
                                        (ow_hbm, WP_OUT, D_MODEL, 1.0)):
            chunks += [(src, r, dst0 + r, scale) for r in range(0, nrows, WP_CHUNK)]

        def proj_copy(n):
            src, r, _, _ = chunks[n]
            return pltpu.make_async_copy(src.at[r:r + WP_CHUNK, :], stage_proj.at[n % WP_SLOTS],
                                         w_sem.at[n % WP_SLOTS])

        def ada_copy(i):
            return pltpu.make_async_copy(wada_hbm.at[i * WIN_CHUNK:(i + 1) * WIN_CHUNK, :],
                                         stage_in.at[i % WIN_SLOTS, :, 0:3 * D_MODEL],
                                         win_sem.at[i % WIN_SLOTS])

        for n in range(WP_SLOTS - 1):
            proj_copy(n).start()
        lax.fori_loop(0, n_win, win_body, 0)
        for i in range(WIN_SLOTS - 1):
            ada_copy(i).start()

        for n, (_, _, dst, scale) in enumerate(chunks):
            if n + WP_SLOTS - 1 < len(chunks):
                proj_copy(n + WP_SLOTS - 1).start()
            proj_copy(n).wait()
            v = stage_proj[n % WP_SLOTS]
            wproj_ref[dst:dst + WP_CHUNK, 0:D_MODEL] = (v if scale == 1.0 else scale * v).astype(BF16)

        c = c_ref[...]
        c_act = (c * jax.nn.sigmoid(c)).astype(BF16)
        mod = jnp.broadcast_to(bada_ref[...], (BATCH, 3 * D_MODEL))
        for i in range(n_win):
            if i + WIN_SLOTS - 1 < n_win:
                ada_copy(i + WIN_SLOTS - 1).start()
            ada_copy(i).wait()
            w_chunk = stage_in[i % WIN_SLOTS, :, 0:3 * D_MODEL].astype(BF16)
            mod = mod + _dot(c_act[:, i * WIN_CHUNK:(i + 1) * WIN_CHUNK], w_chunk)
        mod_scr[...] = mod

        for cblk in range(NG):
            s0 = _gate_win_start(cblk)
            for half, w_ref in enumerate((rgwa_ref, rgwx_ref)):
                rows = w_ref[s0:s0 + GATE_WIN, :].astype(BF16)
                dense = _dot(rows, spread_ref[cblk, 0:RG_HEAD_DIM, :]) * samehead_ref[cblk].astype(F32)
                wg_ref[cblk, :, half * GB:(half + 1) * GB] = dense.astype(BF16)

    @pl.when(step == 0)
    def _():
        ubuf[:, 0:SC_HIST, :] = jnp.zeros((NA, SC_HIST, CB), F32)
        vbuf[0:RG_HIST, :] = jnp.zeros((RG_HIST, RG_W), F32)
        hstate[...] = jnp.zeros_like(hstate)
        for g in range(2):
            for cp in in_copies(g):
                cp.start()
        small = [pltpu.make_async_copy(src, rgp.at[r0:r0 + src.shape[0], 0:RG_W], rgp_sem.at[0])
                 for src, r0 in ((rgcw_hbm, RGP_CW), (rgcb_hbm, RGP_CB), (gba_hbm, RGP_BA), (gbx_hbm, RGP_BX),
                                 (lam_hbm, RGP_LAM))]
        for cp in small:
            cp.start()
        load_weights()
        for cp in small:
            cp.wait()
        for cp in in_copies(0):
            cp.wait()
        norm_in(0, 0, h_tb.at[0])
        block0_dots_to_scratch(h_tb.at[0])

    @pl.when(step >= OBUFS)
    def _():
        for cp in out_copies(step - OBUFS):
            cp.wait()

    for cp in in_copies(step + 2):
        cp.start()
    for cp in in_copies(step + 1):
        cp.wait()

    def mixer_a_block(j, z):
        zb, zc, zv, hzg = z
        u = zc * zv
        ubuf[j, SC_HIST:SC_HIST + ROWS, :] = u
        cw = sccw_ref[:, j * CB:(j + 1) * CB]
        conv = sccb_ref[:, j * CB:(j + 1) * CB] + ubuf[j, 0:ROWS, :] * cw[0:1, :]
        conv = conv + ubuf[j, BATCH:BATCH + ROWS, :] * cw[1:2, :]
        conv = conv + u * cw[2:3, :]
        ubuf[j, 0:SC_HIST, :] = u[ROWS - SC_HIST:ROWS, :]
        s_scr[j] = (zb * conv * _half_silu_from_half(hzg)).astype(BF16)

    def gate_dot(c):
        start = _gate_win_start(c)
        slabs = vbf if start % LANES == 0 else vbs
        s0 = start // LANES
        lhs = jnp.concatenate([slabs[s0 + i] for i in range(GATE_WIN // LANES)], axis=1)
        return _dot(lhs, wg_ref[c])

    def gate_dots(h_cur, k):
        pieces = [gate_dot(c) for c in range(k * CB // GB, (k + 1) * CB // GB)]
        hg_a = jnp.concatenate([p[:, 0:GB] for p in pieces], axis=1)
        hg_x = jnp.concatenate([p[:, GB:2 * GB] for p in pieces], axis=1)
        return (hg_a, hg_x), proj(h_cur, O_RGG + k * CB, CB)

    def mixer_b_block(k, hg, hzg):
        hg_a, hg_x = hg
        cols = slice(k * CB, (k + 1) * CB)
        t_r = jnp.tanh(hg_a + 0.5 * rgp[RGP_BA:RGP_BA + 1, cols])
        t_i = jnp.tanh(hg_x + 0.5 * rgp[RGP_BX:RGP_BX + 1, cols])
        half_k = (-0.5 * LRU_C) * jax.nn.softplus(-rgp[RGP_LAM:RGP_LAM + 1, cols])
        log_a = half_k + half_k * t_r
        a = jnp.exp(log_a)
        a_scr[...] = a
        one_minus_a2 = jnp.tanh(log_a) * (-1.0 - a * a)
        root = jnp.where(one_minus_a2 > 0.0, one_minus_a2 * lax.rsqrt(one_minus_a2), 0.0)
        hvk = hvf[k]
        b_scr[...] = root * (hvk + hvk * t_i)
        h = hstate[k]
        for t in range(TS):
            rows = slice(t * BATCH, (t + 1) * BATCH)
            h = a_scr[rows, :] * h + b_scr[rows, :]
            yrg_scr[rows, :] = h
        hstate[k] = h
        s2_scr[k] = (yrg_scr[...] * _half_silu_from_half(hzg)).astype(BF16)

    def process_tile(c):
        h_cur = h_tb.at[c % 2]
        h_next = h_tb.at[(c + 1) % 2]

        z = tuple(z0_scr[n] for n in range(4))
        for j in range(NA):
            z_next = mixer_a_dots(h_cur, j + 1) if j + 1 < NA else None
            if j == 0:
                if c + 1 < TPS:
                    norm_in(xg_cur, c + 1, h_next)
                else:
                    norm_in(xg_next, 0, h_next)
            mixer_a_block(j, z)
            z = z_next

        vz = proj(h_cur, O_RGV, RG_W)
        m_pieces = [proj(h_cur, O_M, M_PIECE)]
        s_all = jnp.concatenate([s_scr[j] for j in range(NA)], axis=1)
        ya_half = _dot(s_all, wproj_ref[WP_SC:WP_SC + D_MODEL, 0:D_MODEL])
        vbuf[RG_HIST:RG_HIST + ROWS, :] = vz
        rcw = 0.5 * rgp[RGP_CW:RGP_CW + RG_K, 0:RG_W]
        hv = 0.5 * rgp[RGP_CB:RGP_CB + 1, 0:RG_W] + vbuf[0:ROWS, :] * rcw[0:1, :]
        hv = hv + vbuf[BATCH:BATCH + ROWS, :] * rcw[1:2, :]
        hv = hv + vbuf[2 * BATCH:2 * BATCH + ROWS, :] * rcw[2:3, :]
        hv = hv + vz * rcw[3:4, :]
        vbuf[0:RG_HIST, :] = vz[ROWS - RG_HIST:ROWS, :]
        for cidx in range(NG):
            vbf[cidx] = hv[:, cidx * LANES:(cidx + 1) * LANES].astype(BF16)
        for cidx in range(NG - 1):
            lo = GATE_SHIFT + cidx * LANES
            vbs[cidx] = hv[:, lo:lo + LANES].astype(BF16)
        for k in range(NB):
            hvf[k] = hv[:, k * CB:(k + 1) * CB]

        hg, hzg = gate_dots(h_cur, 0)
        for k in range(NB):
            hg_next, hzg_next = gate_dots(h_cur, k + 1) if k + 1 < NB else (None, None)
            if 1 <= k < NB - 1:
                m_pieces.append(proj(h_cur, O_M + k * M_PIECE, M_PIECE))
            mixer_b_block(k, hg, hzg)
            hg, hzg = hg_next, hzg_next
        m = jnp.concatenate(m_pieces, axis=1)

        s2_all = jnp.concatenate([s2_scr[k] for k in range(NB)], axis=1)
        yb_half = _dot(s2_all, wproj_ref[WP_RG:WP_RG + RG_W, 0:D_MODEL])

        t_a = jnp.tanh(m[:, 0:D_MODEL] + 0.5 * bm_ref[0:1, :])
        t_b = jnp.tanh(m[:, D_MODEL:2 * D_MODEL] + 0.5 * bm_ref[1:2, :])
        merged = ((ya_half + ya_half * t_a) + (yb_half + yb_half * t_b)).astype(BF16)
        y3 = _dot(merged, wproj_ref[WP_OUT:WP_OUT + D_MODEL, 0:D_MODEL]).reshape(TS, BATCH, D_MODEL)
        block0_dots_to_scratch(h_next)
        xo = xbuf[xg_cur, c * TS:(c + 1) * TS] + mod_scr[:, 2 * D_MODEL:3 * D_MODEL] * y3
        ms = jnp.mean(xo * xo, axis=-1, keepdims=True)
        obuf[og_cur, c * TS:(c + 1) * TS] = xo * lax.rsqrt(ms + EPS) * gfinal_ref[...]

    for c in range(TPS):
        process_tile(c)

    for cp in out_copies(step):
        cp.start()

    @pl.when(step == NSTEPS - 1)
    def _():
        for cp in in_copies(step + 2):
            cp.wait()
        for cp in out_copies(step - 1):
            cp.wait()
        for cp in out_copies(step):
            cp.wait()


def _const_spec(shape):
    nd = len(shape)
    return pl.BlockSpec(shape, lambda i, _nd=nd: (0,) * _nd, pipeline_mode=pl.Buffered(1))


def _gate_layout_constants():
    starts = np.array([_gate_win_start(c) for c in range(NG)])
    row_ch = starts[:, None] + np.arange(GATE_WIN)[None, :]
    col_ch = np.arange(NG)[:, None] * GB + np.arange(GB)[None, :]
    same_head = (row_ch // RG_HEAD_DIM)[:, :, None] == (col_ch // RG_HEAD_DIM)[:, None, :]
    spread = np.arange(LANES)[None, :, None] == (col_ch % RG_HEAD_DIM)[:, None, :]
    return jnp.asarray(spread, BF16), jnp.asarray(same_head, BF16)


def kernel(x, c, w_ada, b_ada, g_norm, w_in, sc_conv_w, sc_conv_b, sc_w_out, rg_conv_w, rg_conv_b,
           rg_w_a, rg_b_a, rg_w_x, rg_b_x, rg_lambda, rg_w_out, b_merge, w_out, g_final):
    assert x.shape == (BATCH, SEQ, D_MODEL) and w_ada.shape[0] == 1
    l = 0
    spread, same_head = _gate_layout_constants()

    vmem_operands = (
        c, b_ada[l].reshape(1, 3 * D_MODEL), g_norm[l].reshape(1, D_MODEL), g_final.reshape(1, D_MODEL),
        sc_conv_w[l], sc_conv_b[l].reshape(1, D_MODEL), b_merge[l],
        rg_w_a[l].reshape(RG_W, RG_HEAD_DIM), rg_w_x[l].reshape(RG_W, RG_HEAD_DIM), spread, same_head)
    hbm_weights = (rg_conv_w[l], rg_conv_b[l].reshape(1, RG_W), rg_b_a[l].reshape(1, RG_W),
                   rg_b_x[l].reshape(1, RG_W), rg_lambda[l].reshape(1, RG_W),
                   w_ada[l], w_in[l], sc_w_out[l], rg_w_out[l], w_out[l])
    operands = (x,) + vmem_operands + hbm_weights
    any_spec = pl.BlockSpec(memory_space=pl.ANY)
    in_specs = [any_spec] + [_const_spec(op.shape) for op in vmem_operands] + [any_spec] * len(hbm_weights)
    scratch = [
        pltpu.VMEM((8, 2 * D_MODEL), F32),
        pltpu.SemaphoreType.DMA((1,)),
        pltpu.VMEM((BATCH, 3 * D_MODEL), F32),
        pltpu.VMEM((NG, GATE_WIN, 2 * GB), BF16),
        pltpu.VMEM((D_MODEL, P_IN + PITCH_PAD), BF16),
        pltpu.VMEM((WP_ROWS, D_MODEL + PITCH_PAD), BF16),
        pltpu.VMEM((WIN_SLOTS, WIN_CHUNK, P_IN), F32),
        pltpu.VMEM((WP_SLOTS, WP_CHUNK, D_MODEL), F32),
        pltpu.SemaphoreType.DMA((WIN_SLOTS,)),
        pltpu.SemaphoreType.DMA((WP_SLOTS,)),
        pltpu.VMEM((XBUFS, GROUP_T, BATCH, D_MODEL), F32),
        pltpu.VMEM((OBUFS, GROUP_T, BATCH, D_MODEL), F32),
        pltpu.SemaphoreType.DMA((XBUFS,)),
        pltpu.SemaphoreType.DMA((OBUFS,)),
        pltpu.VMEM((2, ROWS, D_MODEL + PITCH_PAD), BF16),
        pltpu.VMEM((4, ROWS, CB), F32),
        pltpu.VMEM((NA, SC_HIST + ROWS, CB), F32),
        pltpu.VMEM((NA, ROWS, CB), BF16),
        pltpu.VMEM((RG_HIST + ROWS, RG_W), F32),
        pltpu.VMEM((NB, ROWS, CB), F32),
        pltpu.VMEM((NG, ROWS, LANES), BF16),
        pltpu.VMEM((NG - 1, ROWS, LANES), BF16),
        pltpu.VMEM((ROWS, CB), F32),
        pltpu.VMEM((ROWS, CB), F32),
        pltpu.VMEM((ROWS, CB), F32),
        pltpu.VMEM((NB, ROWS, CB), BF16),
        pltpu.VMEM((NB, BATCH, CB), F32),
    ]
    return pl.pallas_call(
        _block_kernel,
        grid=(NSTEPS,),
        in_specs=in_specs,
        out_specs=any_spec,
        out_shape=jax.ShapeDtypeStruct((BATCH, SEQ, D_MODEL), F32),
        scratch_shapes=scratch,
        compiler_params=pltpu.CompilerParams(
            dimension_semantics=("arbitrary",), vmem_limit_bytes=VMEM_LIMIT_BYTES),
        name="hybrid_block",
    )(*operands)
```

```python
import jax
import jax.numpy as jnp
import numpy as np
from jax import lax
from jax.experimental import pallas as pl
from jax.experimental.pallas import tpu as pltpu

D_MODEL = 1024
BATCH = 16
SEQ = 2048
SC_K = 3
RG_W = 1280
RG_HEADS = 16
RG_HEAD_DIM = RG_W // RG_HEADS
RG_K = 4
LRU_C = 8.0
EPS = 1e-6

LANES = 128
MXU_DIM = 256
V7X_VMEM_BYTES = 64 * 1024 * 1024
TS = 16
TPS = 2
NTILES = SEQ // TS
NSTEPS = NTILES // TPS
assert TPS % 2 == 0
ROWS = TS * BATCH
GROUP_T = TPS * TS
XBUFS = 3
OBUFS = 2
CB = MXU_DIM
NA = D_MODEL // CB
NB = RG_W // CB
GB = LANES
NG = RG_W // GB
GATE_WIN = MXU_DIM
GATE_SHIFT = LANES // 2
SC_HIST = (SC_K - 1) * BATCH
RG_HIST = (RG_K - 1) * BATCH
VMEM_LIMIT_BYTES = V7X_VMEM_BYTES - 4 * 1024 * 1024

O_SCB, O_SCC, O_SCV, O_SCG = 0, D_MODEL, 2 * D_MODEL, 3 * D_MODEL
O_RGV = 4 * D_MODEL
O_RGG = O_RGV + RG_W
O_M = O_RGG + RG_W
P_IN = O_M + 2 * D_MODEL
M_PIECE = 2 * D_MODEL // (NB - 1)

WP_SC = 0
WP_RG = WP_SC + D_MODEL
WP_OUT = WP_RG + RG_W
WP_ROWS = WP_OUT + D_MODEL

PITCH_PAD = LANES

RGP_CW, RGP_CB, RGP_BA, RGP_BX, RGP_LAM = 0, RG_K, RG_K + 1, RG_K + 2, RG_K + 3
WIN_CHUNK = 64
WIN_SLOTS = 3
WP_CHUNK = 256
WP_SLOTS = 2
WIN_COL_SCALES = ((0, O_SCG, 1.0), (O_SCG, O_RGV, 0.5), (O_RGV, O_RGG, 1.0), (O_RGG, P_IN, 0.5))

F32 = jnp.float32
BF16 = jnp.bfloat16


def _gate_win_start(c):
    lo = (c * GB) // RG_HEAD_DIM * RG_HEAD_DIM
    hi = -(-(c * GB + GB) // RG_HEAD_DIM) * RG_HEAD_DIM
    start = min(lo // GATE_SHIFT * GATE_SHIFT, RG_W - GATE_WIN)
    assert start <= lo and hi <= start + GATE_WIN
    return start


def _dot(a, b):
    return jnp.dot(a, b, preferred_element_type=F32)


def _half_silu_from_half(hz):
    return hz + hz * jnp.tanh(hz)


def _group_copies(hbm_ref, buf_ref, sem_ref, group, slot, to_hbm):
    copies = []
    for b in range(BATCH):
        hbm = hbm_ref.at[b, pl.ds(group * GROUP_T, GROUP_T), :]
        vmem = buf_ref.at[slot, :, b, :]
        src, dst = (vmem, hbm) if to_hbm else (hbm, vmem)
        copies.append(pltpu.make_async_copy(src, dst, sem_ref.at[slot]))
    return copies


def _block_kernel(x_hbm, c_ref, bada_ref, gnorm_ref, gfinal_ref, sccw_ref, sccb_ref,
                  bm_ref, rgwa_ref, rgwx_ref, spread_ref, samehead_ref,
                  rgcw_hbm, rgcb_hbm, gba_hbm, gbx_hbm, lam_hbm,
                  wada_hbm, win_hbm, scw_hbm, rgw_hbm, ow_hbm,
                  out_hbm,
                  rgp, rgp_sem, mod_scr, wg_ref, win_ref, wproj_ref, stage_in, stage_proj, win_sem, w_sem,
                  xbuf, obuf, in_sem, out_sem,
                  h_tb, z0_scr, ubuf, s_scr, vbuf, hvf, vbf, vbs, a_scr, b_scr, yrg_scr, s2_scr, hstate):
    step = pl.program_id(0)
    xg_cur = lax.rem(step, XBUFS)
    xg_next = lax.rem(step + 1, XBUFS)
    og_cur = lax.rem(step, OBUFS)

    def in_copies(group):
        if isinstance(group, int):
            return _group_copies(x_hbm, xbuf, in_sem, min(group, NSTEPS - 1), group % XBUFS, to_hbm=False)
        return _group_copies(x_hbm, xbuf, in_sem, jnp.minimum(group, NSTEPS - 1),
                             lax.rem(group, XBUFS), to_hbm=False)

    def out_copies(group):
        return _group_copies(out_hbm, obuf, out_sem, group, lax.rem(group, OBUFS), to_hbm=True)

    def norm_in(x_slot, c, h_ref):
        x3 = xbuf[x_slot, c * TS:(c + 1) * TS]
        ms = jnp.mean(x3 * x3, axis=-1, keepdims=True)
        shift = mod_scr[:, 0:D_MODEL]
        scale1 = 1.0 + mod_scr[:, D_MODEL:2 * D_MODEL]
        hn = (x3 * lax.rsqrt(ms + EPS) * gnorm_ref[...]) * scale1 + shift
        h_ref[:, 0:D_MODEL] = hn.reshape(ROWS, D_MODEL).astype(BF16)

    def proj(h_ref, col0, width):
        return _dot(h_ref[:, 0:D_MODEL], win_ref[:, col0:col0 + width])

    def mixer_a_dots(h_ref, j):
        return tuple(proj(h_ref, o + j * CB, CB) for o in (O_SCB, O_SCC, O_SCV, O_SCG))

    def block0_dots_to_scratch(h_ref):
        for n, zn in enumerate(mixer_a_dots(h_ref, 0)):
            z0_scr[n] = zn

    def load_weights():
        n_win = D_MODEL // WIN_CHUNK

        def win_copy(i):
            slot = i % WIN_SLOTS if isinstance(i, int) else lax.rem(i, WIN_SLOTS)
            return pltpu.make_async_copy(win_hbm.at[pl.ds(i * WIN_CHUNK, WIN_CHUNK), :],
                                         stage_in.at[slot], win_sem.at[slot])

        for i in range(WIN_SLOTS - 1):
            win_copy(i).start()

        def win_body(i, carry):
            slot = lax.rem(i, WIN_SLOTS)

            @pl.when(i + WIN_SLOTS - 1 < n_win)
            def _():
                win_copy(i + WIN_SLOTS - 1).start()

            win_copy(i).wait()
            rows = pl.ds(pl.multiple_of(i * WIN_CHUNK, WIN_CHUNK), WIN_CHUNK)
            for c0, c1, scale in WIN_COL_SCALES:
                v = stage_in[slot, :, c0:c1]
                win_ref[rows, c0:c1] = (v if scale == 1.0 else scale * v).astype(BF16)
            return carry

        chunks = []
        for src, dst0, nrows, scale in ((scw_hbm, WP_SC, D_MODEL, 0.5), (rgw_hbm, WP_RG, RG_W, 0.5),
                                        (ow_hbm, WP_OUT, D_MODEL, 1.0)):
            chunks += [(src, r, dst0 + r, scale) for r in range(0, nrows, WP_CHUNK)]

        def proj_copy(n):
            src, r, _, _ = chunks[n]
            return pltpu.make_async_copy(src.at[r:r + WP_CHUNK, :], stage_proj.at[n % WP_SLOTS],
                                         w_sem.at[n % WP_SLOTS])

        def ada_copy(i):
            return pltpu.make_async_copy(wada_hbm.at[i * WIN_CHUNK:(i + 1) * WIN_CHUNK, :],
                                         stage_in.at[i % WIN_SLOTS, :, 0:3 * D_MODEL],
                                         win_sem.at[i % WIN_SLOTS])

        for n in range(WP_SLOTS - 1):
            proj_copy(n).start()
        lax.fori_loop(0, n_win, win_body, 0)
        for i in range(WIN_SLOTS - 1):
            ada_copy(i).start()

        for n, (_, _, dst, scale) in enumerate(chunks):
            if n + WP_SLOTS - 1 < len(chunks):
                proj_copy(n + WP_SLOTS - 1).start()
            proj_copy(n).wait()
            v = stage_proj[n % WP_SLOTS]
            wproj_ref[dst:dst + WP_CHUNK, 0:D_MODEL] = (v if scale == 1.0 else scale * v).astype(BF16)

        c = c_ref[...]
        c_act = (c * jax.nn.sigmoid(c)).astype(BF16)
        mod = jnp.broadcast_to(bada_ref[...], (BATCH, 3 * D_MODEL))
        for i in range(n_win):
            if i + WIN_SLOTS - 1 < n_win:
                ada_copy(i + WIN_SLOTS - 1).start()
            ada_copy(i).wait()
            w_chunk = stage_in[i % WIN_SLOTS, :, 0:3 * D_MODEL].astype(BF16)
            mod = mod + _dot(c_act[:, i * WIN_CHUNK:(i + 1) * WIN_CHUNK], w_chunk)
        mod_scr[...] = mod

        for cblk in range(NG):
            s0 = _gate_win_start(cblk)
            for half, w_ref in enumerate((rgwa_ref, rgwx_ref)):
                rows = w_ref[s0:s0 + GATE_WIN, :].astype(BF16)
                dense = _dot(rows, spread_ref[cblk, 0:RG_HEAD_DIM, :]) * samehead_ref[cblk].astype(F32)
                wg_ref[cblk, :, half * GB:(half + 1) * GB] = dense.astype(BF16)

    @pl.when(step == 0)
    def _():
        ubuf[:, 0:SC_HIST, :] = jnp.zeros((NA, SC_HIST, CB), F32)
        vbuf[0:RG_HIST, :] = jnp.zeros((RG_HIST, RG_W), F32)
        hstate[...] = jnp.zeros_like(hstate)
        for g in range(2):
            for cp in in_copies(g):
                cp.start()
        small = [pltpu.make_async_copy(src, rgp.at[r0:r0 + src.shape[0], 0:RG_W], rgp_sem.at[0])
                 for src, r0 in ((rgcw_hbm, RGP_CW), (rgcb_hbm, RGP_CB), (gba_hbm, RGP_BA), (gbx_hbm, RGP_BX),
                                 (lam_hbm, RGP_LAM))]
        for cp in small:
            cp.start()
        load_weights()
        for cp in small:
            cp.wait()
        for cp in in_copies(0):
            cp.wait()
        norm_in(0, 0, h_tb.at[0])
        block0_dots_to_scratch(h_tb.at[0])

    @pl.when(step >= OBUFS)
    def _():
        for cp in out_copies(step - OBUFS):
            cp.wait()

    for cp in in_copies(step + 2):
        cp.start()
    for cp in in_copies(step + 1):
        cp.wait()

    def mixer_a_block(j, z):
        zb, zc, zv, hzg = z
        u = zc * zv
        ubuf[j, SC_HIST:SC_HIST + ROWS, :] = u
        cw = sccw_ref[:, j * CB:(j + 1) * CB]
        conv = sccb_ref[:, j * CB:(j + 1) * CB] + ubuf[j, 0:ROWS, :] * cw[0:1, :]
        conv = conv + ubuf[j, BATCH:BATCH + ROWS, :] * cw[1:2, :]
        conv = conv + u * cw[2:3, :]
        ubuf[j, 0:SC_HIST, :] = u[ROWS - SC_HIST:ROWS, :]
        s_scr[j] = (zb * conv * _half_silu_from_half(hzg)).astype(BF16)

    def gate_dot(c):
        start = _gate_win_start(c)
        slabs = vbf if start % LANES == 0 else vbs
        s0 = start // LANES
        lhs = jnp.concatenate([slabs[s0 + i] for i in range(GATE_WIN // LANES)], axis=1)
        return _dot(lhs, wg_ref[c])

    def gate_dots(h_cur, k):
        pieces = [gate_dot(c) for c in range(k * CB // GB, (k + 1) * CB // GB)]
        hg_a = jnp.concatenate([p[:, 0:GB] for p in pieces], axis=1)
        hg_x = jnp.concatenate([p[:, GB:2 * GB] for p in pieces], axis=1)
        return (hg_a, hg_x), proj(h_cur, O_RGG + k * CB, CB)

    def mixer_b_block(k, hg, hzg):
        hg_a, hg_x = hg
        cols = slice(k * CB, (k + 1) * CB)
        t_r = jnp.tanh(hg_a + 0.5 * rgp[RGP_BA:RGP_BA + 1, cols])
        t_i = jnp.tanh(hg_x + 0.5 * rgp[RGP_BX:RGP_BX + 1, cols])
        half_k = (-0.5 * LRU_C) * jax.nn.softplus(-rgp[RGP_LAM:RGP_LAM + 1, cols])
        log_a = half_k + half_k * t_r
        a = jnp.exp(log_a)
        a_scr[...] = a
        one_minus_a2 = jnp.tanh(log_a) * (-1.0 - a * a)
        root = jnp.where(one_minus_a2 > 0.0, one_minus_a2 * lax.rsqrt(one_minus_a2), 0.0)
        hvk = hvf[k]
        b_scr[...] = root * (hvk + hvk * t_i)
        h = hstate[k]
        for t in range(TS):
            rows = slice(t * BATCH, (t + 1) * BATCH)
            h = a_scr[rows, :] * h + b_scr[rows, :]
            yrg_scr[rows, :] = h
        hstate[k] = h
        s2_scr[k] = (yrg_scr[...] * _half_silu_from_half(hzg)).astype(BF16)

    def process_tile(c):
        h_cur = h_tb.at[c % 2]
        h_next = h_tb.at[(c + 1) % 2]

        z = tuple(z0_scr[n] for n in range(4))
        for j in range(NA):
            z_next = mixer_a_dots(h_cur, j + 1) if j + 1 < NA else None
            if j == 0:
                if c + 1 < TPS:
                    norm_in(xg_cur, c + 1, h_next)
                else:
                    norm_in(xg_next, 0, h_next)
            mixer_a_block(j, z)
            z = z_next

        vz = proj(h_cur, O_RGV, RG_W)
        m_pieces = [proj(h_cur, O_M, M_PIECE)]
        s_all = jnp.concatenate([s_scr[j] for j in range(NA)], axis=1)
        ya_half = _dot(s_all, wproj_ref[WP_SC:WP_SC + D_MODEL, 0:D_MODEL])
        vbuf[RG_HIST:RG_HIST + ROWS, :] = vz
        rcw = 0.5 * rgp[RGP_CW:RGP_CW + RG_K, 0:RG_W]
        hv = 0.5 * rgp[RGP_CB:RGP_CB + 1, 0:RG_W] + vbuf[0:ROWS, :] * rcw[0:1, :]
        hv = hv + vbuf[BATCH:BATCH + ROWS, :] * rcw[1:2, :]
        hv = hv + vbuf[2 * BATCH:2 * BATCH + ROWS, :] * rcw[2:3, :]
        hv = hv + vz * rcw[3:4, :]
        vbuf[0:RG_HIST, :] = vz[ROWS - RG_HIST:ROWS, :]
        for cidx in range(NG):
            vbf[cidx] = hv[:, cidx * LANES:(cidx + 1) * LANES].astype(BF16)
        for cidx in range(NG - 1):
            lo = GATE_SHIFT + cidx * LANES
            vbs[cidx] = hv[:, lo:lo + LANES].astype(BF16)
        for k in range(NB):
            hvf[k] = hv[:, k * CB:(k + 1) * CB]

        hg, hzg = gate_dots(h_cur, 0)
        for k in range(NB):
            hg_next, hzg_next = gate_dots(h_cur, k + 1) if k + 1 < NB else (None, None)
            if 1 <= k < NB - 1:
                m_pieces.append(proj(h_cur, O_M + k * M_PIECE, M_PIECE))
            mixer_b_block(k, hg, hzg)
            hg, hzg = hg_next, hzg_next
        m = jnp.concatenate(m_pieces, axis=1)

        s2_all = jnp.concatenate([s2_scr[k] for k in range(NB)], axis=1)
        yb_half = _dot(s2_all, wproj_ref[WP_RG:WP_RG + RG_W, 0:D_MODEL])

        t_a = jnp.tanh(m[:, 0:D_MODEL] + 0.5 * bm_ref[0:1, :])
        t_b = jnp.tanh(m[:, D_MODEL:2 * D_MODEL] + 0.5 * bm_ref[1:2, :])
        merged = ((ya_half + ya_half * t_a) + (yb_half + yb_half * t_b)).astype(BF16)
        for j in range(NA):
            s2_scr[j] = merged[:, j * CB:(j + 1) * CB]
        merged = jnp.concatenate([s2_scr[j] for j in range(NA)], axis=1)
        y3 = _dot(merged, wproj_ref[WP_OUT:WP_OUT + D_MODEL, 0:D_MODEL]).reshape(TS, BATCH, D_MODEL)
        block0_dots_to_scratch(h_next)
        xo = xbuf[xg_cur, c * TS:(c + 1) * TS] + mod_scr[:, 2 * D_MODEL:3 * D_MODEL] * y3
        ms = jnp.mean(xo * xo, axis=-1, keepdims=True)
        obuf[og_cur, c * TS:(c + 1) * TS] = xo * lax.rsqrt(ms + EPS) * gfinal_ref[...]

    for c in range(TPS):
        process_tile(c)

    for cp in out_copies(step):
        cp.start()

    @pl.when(step == NSTEPS - 1)
    def _():
        for cp in in_copies(step + 2):
            cp.wait()
        for cp in out_copies(step - 1):
            cp.wait()
        for cp in out_copies(step):
            cp.wait()


def _const_spec(shape):
    nd = len(shape)
    return pl.BlockSpec(shape, lambda i, _nd=nd: (0,) * _nd, pipeline_mode=pl.Buffered(1))


def _gate_layout_constants():
    starts = np.array([_gate_win_start(c) for c in range(NG)])
    row_ch = starts[:, None] + np.arange(GATE_WIN)[None, :]
    col_ch = np.arange(NG)[:, None] * GB + np.arange(GB)[None, :]
    same_head = (row_ch // RG_HEAD_DIM)[:, :, None] == (col_ch // RG_HEAD_DIM)[:, None, :]
    spread = np.arange(LANES)[None, :, None] == (col_ch % RG_HEAD_DIM)[:, None, :]
    return jnp.asarray(spread, BF16), jnp.asarray(same_head, BF16)


def kernel(x, c, w_ada, b_ada, g_norm, w_in, sc_conv_w, sc_conv_b, sc_w_out, rg_conv_w, rg_conv_b,
           rg_w_a, rg_b_a, rg_w_x, rg_b_x, rg_lambda, rg_w_out, b_merge, w_out, g_final):
    assert x.shape == (BATCH, SEQ, D_MODEL) and w_ada.shape[0] == 1
    l = 0
    spread, same_head = _gate_layout_constants()

    vmem_operands = (
        c, b_ada[l].reshape(1, 3 * D_MODEL), g_norm[l].reshape(1, D_MODEL), g_final.reshape(1, D_MODEL),
        sc_conv_w[l], sc_conv_b[l].reshape(1, D_MODEL), b_merge[l],
        rg_w_a[l].reshape(RG_W, RG_HEAD_DIM), rg_w_x[l].reshape(RG_W, RG_HEAD_DIM), spread, same_head)
    hbm_weights = (rg_conv_w[l], rg_conv_b[l].reshape(1, RG_W), rg_b_a[l].reshape(1, RG_W),
                   rg_b_x[l].reshape(1, RG_W), rg_lambda[l].reshape(1, RG_W),
                   w_ada[l], w_in[l], sc_w_out[l], rg_w_out[l], w_out[l])
    operands = (x,) + vmem_operands + hbm_weights
    any_spec = pl.BlockSpec(memory_space=pl.ANY)
    in_specs = [any_spec] + [_const_spec(op.shape) for op in vmem_operands] + [any_spec] * len(hbm_weights)
    scratch = [
        pltpu.VMEM((8, 2 * D_MODEL), F32),
        pltpu.SemaphoreType.DMA((1,)),
        pltpu.VMEM((BATCH, 3 * D_MODEL), F32),
        pltpu.VMEM((NG, GATE_WIN, 2 * GB), BF16),
        pltpu.VMEM((D_MODEL, P_IN + PITCH_PAD), BF16),
        pltpu.VMEM((WP_ROWS, D_MODEL + PITCH_PAD), BF16),
        pltpu.VMEM((WIN_SLOTS, WIN_CHUNK, P_IN), F32),
        pltpu.VMEM((WP_SLOTS, WP_CHUNK, D_MODEL), F32),
        pltpu.SemaphoreType.DMA((WIN_SLOTS,)),
        pltpu.SemaphoreType.DMA((WP_SLOTS,)),
        pltpu.VMEM((XBUFS, GROUP_T, BATCH, D_MODEL), F32),
        pltpu.VMEM((OBUFS, GROUP_T, BATCH, D_MODEL), F32),
        pltpu.SemaphoreType.DMA((XBUFS,)),
        pltpu.SemaphoreType.DMA((OBUFS,)),
        pltpu.VMEM((2, ROWS, D_MODEL + PITCH_PAD), BF16),
        pltpu.VMEM((4, ROWS, CB), F32),
        pltpu.VMEM((NA, SC_HIST + ROWS, CB), F32),
        pltpu.VMEM((NA, ROWS, CB), BF16),
        pltpu.VMEM((RG_HIST + ROWS, RG_W), F32),
        pltpu.VMEM((NB, ROWS, CB), F32),
        pltpu.VMEM((NG, ROWS, LANES), BF16),
        pltpu.VMEM((NG - 1, ROWS, LANES), BF16),
        pltpu.VMEM((ROWS, CB), F32),
        pltpu.VMEM((ROWS, CB), F32),
        pltpu.VMEM((ROWS, CB), F32),
        pltpu.VMEM((NB, ROWS, CB), BF16),
        pltpu.VMEM((NB, BATCH, CB), F32),
    ]
    return pl.pallas_call(
        _block_kernel,
        grid=(NSTEPS,),
        in_specs=in_specs,
        out_specs=any_spec,
        out_shape=jax.ShapeDtypeStruct((BATCH, SEQ, D_MODEL), F32),
        scratch_shapes=scratch,
        compiler_params=pltpu.CompilerParams(
            dimension_semantics=("arbitrary",), vmem_limit_bytes=VMEM_LIMIT_BYTES),
        name="hybrid_block",
    )(*operands)
```

```python
import jax
import jax.numpy as jnp
import numpy as np
from jax import lax
from jax.experimental import pallas as pl
from jax.experimental.pallas import tpu as pltpu

D_MODEL = 1024
BATCH = 16
SEQ = 2048
SC_K = 3
RG_W = 1280
RG_HEADS = 16
RG_HEAD_DIM = RG_W // RG_HEADS
RG_K = 4
LRU_C = 8.0
EPS = 1e-6

LANES = 128
MXU_DIM = 256
V7X_VMEM_BYTES = 64 * 1024 * 1024
TS = 16
TPS = 2
NTILES = SEQ // TS
NSTEPS = NTILES // TPS
assert TPS % 2 == 0
ROWS = TS * BATCH
GROUP_T = TPS * TS
XBUFS = 3
OBUFS = 2
CB = MXU_DIM
NA = D_MODEL // CB
NB = RG_W // CB
GB = LANES
NG = RG_W // GB
GATE_WIN = MXU_DIM
GATE_SHIFT = LANES // 2
SC_HIST = (SC_K - 1) * BATCH
RG_HIST = (RG_K - 1) * BATCH
VMEM_LIMIT_BYTES = V7X_VMEM_BYTES - 4 * 1024 * 1024

O_SCB, O_SCC, O_SCV, O_SCG = 0, D_MODEL, 2 * D_MODEL, 3 * D_MODEL
O_RGV = 4 * D_MODEL
O_RGG = O_RGV + RG_W
O_M = O_RGG + RG_W
P_IN = O_M + 2 * D_MODEL
M_PIECE = 2 * D_MODEL // (NB - 1)

WP_SC = 0
WP_RG = WP_SC + D_MODEL
WP_OUT = WP_RG + RG_W
WP_ROWS = WP_OUT + D_MODEL

PITCH_PAD = LANES

RGP_CW, RGP_CB, RGP_BA, RGP_BX, RGP_LAM = 0, RG_K, RG_K + 1, RG_K + 2, RG_K + 3
WIN_CHUNK = 64
WIN_SLOTS = 3
WP_CHUNK = 256
WP_SLOTS = 2
WIN_COL_SCALES = ((0, O_SCG, 1.0), (O_SCG, O_RGV, 0.5), (O_RGV, O_RGG, 1.0), (O_RGG, P_IN, 0.5))

F32 = jnp.float32
BF16 = jnp.bfloat16


def _gate_win_start(c):
    lo = (c * GB) // RG_HEAD_DIM * RG_HEAD_DIM
    hi = -(-(c * GB + GB) // RG_HEAD_DIM) * RG_HEAD_DIM
    start = min(lo // GATE_SHIFT * GATE_SHIFT, RG_W - GATE_WIN)
    assert start <= lo and hi <= start + GATE_WIN
    return start


def _dot(a, b):
    return jnp.dot(a, b, preferred_element_type=F32)


def _half_silu_from_half(hz):
    return hz + hz * jnp.tanh(hz)


def _group_copies(hbm_ref, buf_ref, sem_ref, group, slot, to_hbm):
    copies = []
    for b in range(BATCH):
        hbm = hbm_ref.at[b, pl.ds(group * GROUP_T, GROUP_T), :]
        vmem = buf_ref.at[slot, :, b, :]
        src, dst = (vmem, hbm) if to_hbm else (hbm, vmem)
        copies.append(pltpu.make_async_copy(src, dst, sem_ref.at[slot]))
    return copies


def _block_kernel(x_hbm, c_ref, bada_ref, gnorm_ref, gfinal_ref, sccw_ref, sccb_ref,
                  bm_ref, rgwa_ref, rgwx_ref, spread_ref, samehead_ref,
                  rgcw_hbm, rgcb_hbm, gba_hbm, gbx_hbm, lam_hbm,
                  wada_hbm, win_hbm, scw_hbm, rgw_hbm, ow_hbm,
                  out_hbm,
                  rgp, rgp_sem, mod_scr, wg_ref, win_ref, wproj_ref, stage_in, stage_proj, win_sem, w_sem,
                  xbuf, obuf, in_sem, out_sem,
                  h_tb, z0_scr, ubuf, s_scr, vbuf, hvf, vbf, vbs, a_scr, b_scr, yrg_scr, s2_scr, hstate):
    step = pl.program_id(0)
    xg_cur = lax.rem(step, XBUFS)
    xg_next = lax.rem(step + 1, XBUFS)
    og_cur = lax.rem(step, OBUFS)

    def in_copies(group):
        if isinstance(group, int):
            return _group_copies(x_hbm, xbuf, in_sem, min(group, NSTEPS - 1), group % XBUFS, to_hbm=False)
        return _group_copies(x_hbm, xbuf, in_sem, jnp.minimum(group, NSTEPS - 1),
                             lax.rem(group, XBUFS), to_hbm=False)

    def out_copies(group):
        return _group_copies(out_hbm, obuf, out_sem, group, lax.rem(group, OBUFS), to_hbm=True)

    def norm_in(x_slot, c, h_ref):
        x3 = xbuf[x_slot, c * TS:(c + 1) * TS]
        ms = jnp.mean(x3 * x3, axis=-1, keepdims=True)
        shift = mod_scr[:, 0:D_MODEL]
        scale1 = 1.0 + mod_scr[:, D_MODEL:2 * D_MODEL]
        hn = (x3 * lax.rsqrt(ms + EPS) * gnorm_ref[...]) * scale1 + shift
        h_ref[:, 0:D_MODEL] = hn.reshape(ROWS, D_MODEL).astype(BF16)

    def proj(h_ref, col0, width):
        return _dot(h_ref[:, 0:D_MODEL], win_ref[:, col0:col0 + width])

    def mixer_a_dots(h_ref, j):
        return tuple(proj(h_ref, o + j * CB, CB) for o in (O_SCB, O_SCC, O_SCV, O_SCG))

    def block0_dots_to_scratch(h_ref):
        for n, zn in enumerate(mixer_a_dots(h_ref, 0)):
            z0_scr[n] = zn

    def load_weights():
        n_win = D_MODEL // WIN_CHUNK

        def win_copy(i):
            slot = i % WIN_SLOTS if isinstance(i, int) else lax.rem(i, WIN_SLOTS)
            return pltpu.make_async_copy(win_hbm.at[pl.ds(i * WIN_CHUNK, WIN_CHUNK), :],
                                         stage_in.at[slot], win_sem.at[slot])

        for i in range(WIN_SLOTS - 1):
            win_copy(i).start()

        def win_body(i, carry):
            slot = lax.rem(i, WIN_SLOTS)

            @pl.when(i + WIN_SLOTS - 1 < n_win)
            def _():
                win_copy(i + WIN_SLOTS - 1).start()

            win_copy(i).wait()
            rows = pl.ds(pl.multiple_of(i * WIN_CHUNK, WIN_CHUNK), WIN_CHUNK)
            for c0, c1, scale in WIN_COL_SCALES:
                v = stage_in[slot, :, c0:c1]
                win_ref[rows, c0:c1] = (v if scale == 1.0 else scale * v).astype(BF16)
            return carry

        chunks = []
        for src, dst0, nrows, scale in ((scw_hbm, WP_SC, D_MODEL, 0.5), (rgw_hbm, WP_RG, RG_W, 0.5),
                                        (ow_hbm, WP_OUT, D_MODEL, 1.0)):
            chunks += [(src, r, dst0 + r, scale) for r in range(0, nrows, WP_CHUNK)]

        def proj_copy(n):
            src, r, _, _ = chunks[n]
            return pltpu.make_async_copy(src.at[r:r + WP_CHUNK, :], stage_proj.at[n % WP_SLOTS],
                                         w_sem.at[n % WP_SLOTS])

        def ada_copy(i):
            return pltpu.make_async_copy(wada_hbm.at[i * WIN_CHUNK:(i + 1) * WIN_CHUNK, :],
                                         stage_in.at[i % WIN_SLOTS, :, 0:3 * D_MODEL],
                                         win_sem.at[i % WIN_SLOTS])

        for n in range(WP_SLOTS - 1):
            proj_copy(n).start()
        lax.fori_loop(0, n_win, win_body, 0)
        for i in range(WIN_SLOTS - 1):
            ada_copy(i).start()

        for n, (_, _, dst, scale) in enumerate(chunks):
            if n + WP_SLOTS - 1 < len(chunks):
                proj_copy(n + WP_SLOTS - 1).start()
            proj_copy(n).wait()
            v = stage_proj[n % WP_SLOTS]
            wproj_ref[dst:dst + WP_CHUNK, 0:D_MODEL] = (v if scale == 1.0 else scale * v).astype(BF16)

        c = c_ref[...]
        c_act = (c * jax.nn.sigmoid(c)).astype(BF16)
        mod = jnp.broadcast_to(bada_ref[...], (BATCH, 3 * D_MODEL))
        for i in range(n_win):
            if i + WIN_SLOTS - 1 < n_win:
                ada_copy(i + WIN_SLOTS - 1).start()
            ada_copy(i).wait()
            w_chunk = stage_in[i % WIN_SLOTS, :, 0:3 * D_MODEL].astype(BF16)
            mod = mod + _dot(c_act[:, i * WIN_CHUNK:(i + 1) * WIN_CHUNK], w_chunk)
        mod_scr[...] = mod

        for cblk in range(NG):
            s0 = _gate_win_start(cblk)
            for half, w_ref in enumerate((rgwa_ref, rgwx_ref)):
                rows = w_ref[s0:s0 + GATE_WIN, :].astype(BF16)
                dense = _dot(rows, spread_ref[cblk, 0:RG_HEAD_DIM, :]) * samehead_ref[cblk].astype(F32)
                wg_ref[cblk, :, half * GB:(half + 1) * GB] = dense.astype(BF16)

    @pl.when(step == 0)
    def _():
        ubuf[:, 0:SC_HIST, :] = jnp.zeros((NA, SC_HIST, CB), F32)
        vbuf[0:RG_HIST, :] = jnp.zeros((RG_HIST, RG_W), F32)
        hstate[...] = jnp.zeros_like(hstate)
        for g in range(2):
            for cp in in_copies(g):
                cp.start()
        small = [pltpu.make_async_copy(src, rgp.at[r0:r0 + src.shape[0], 0:RG_W], rgp_sem.at[0])
                 for src, r0 in ((rgcw_hbm, RGP_CW), (rgcb_hbm, RGP_CB), (gba_hbm, RGP_BA), (gbx_hbm, RGP_BX),
                                 (lam_hbm, RGP_LAM))]
        for cp in small:
            cp.start()
        load_weights()
        for cp in small:
            cp.wait()
        for cp in in_copies(0):
            cp.wait()
        norm_in(0, 0, h_tb.at[0])
        block0_dots_to_scratch(h_tb.at[0])

    @pl.when(step >= OBUFS)
    def _():
        for cp in out_copies(step - OBUFS):
            cp.wait()

    for cp in in_copies(step + 2):
        cp.start()
    for cp in in_copies(step + 1):
        cp.wait()

    def mixer_a_block(j, z):
        zb, zc, zv, hzg = z
        u = zc * zv
        ubuf[j, SC_HIST:SC_HIST + ROWS, :] = u
        cw = sccw_ref[:, j * CB:(j + 1) * CB]
        conv = sccb_ref[:, j * CB:(j + 1) * CB] + ubuf[j, 0:ROWS, :] * cw[0:1, :]
        conv = conv + ubuf[j, BATCH:BATCH + ROWS, :] * cw[1:2, :]
        conv = conv + u * cw[2:3, :]
        ubuf[j, 0:SC_HIST, :] = u[ROWS - SC_HIST:ROWS, :]
        s_scr[j] = (zb * conv * _half_silu_from_half(hzg)).astype(BF16)

    def gate_dot(c):
        start = _gate_win_start(c)
        slabs = vbf if start % LANES == 0 else vbs
        s0 = start // LANES
        lhs = jnp.concatenate([slabs[s0 + i] for i in range(GATE_WIN // LANES)], axis=1)
        return _dot(lhs, wg_ref[c])

    def gate_dots(h_cur, k):
        pieces = [gate_dot(c) for c in range(k * CB // GB, (k + 1) * CB // GB)]
        hg_a = jnp.concatenate([p[:, 0:GB] for p in pieces], axis=1)
        hg_x = jnp.concatenate([p[:, GB:2 * GB] for p in pieces], axis=1)
        return (hg_a, hg_x), proj(h_cur, O_RGG + k * CB, CB)

    def mixer_b_block(k, hg, hzg):
        hg_a, hg_x = hg
        cols = slice(k * CB, (k + 1) * CB)
        t_r = jnp.tanh(hg_a + 0.5 * rgp[RGP_BA:RGP_BA + 1, cols])
        t_i = jnp.tanh(hg_x + 0.5 * rgp[RGP_BX:RGP_BX + 1, cols])
        half_k = (-0.5 * LRU_C) * jax.nn.softplus(-rgp[RGP_LAM:RGP_LAM + 1, cols])
        log_a = half_k + half_k * t_r
        a = jnp.exp(log_a)
        a_scr[...] = a
        one_minus_a2 = jnp.tanh(log_a) * (-1.0 - a * a)
        root = jnp.where(one_minus_a2 > 0.0, one_minus_a2 * lax.rsqrt(one_minus_a2), 0.0)
        hvk = hvf[k]
        b_scr[...] = root * (hvk + hvk * t_i)
        h = hstate[k]
        for t in range(TS):
            rows = slice(t * BATCH, (t + 1) * BATCH)
            h = a_scr[rows, :] * h + b_scr[rows, :]
            yrg_scr[rows, :] = h
        hstate[k] = h
        s2_scr[k] = (yrg_scr[...] * _half_silu_from_half(hzg)).astype(BF16)

    def process_tile(c):
        h_cur = h_tb.at[c % 2]
        h_next = h_tb.at[(c + 1) % 2]

        z = tuple(z0_scr[n] for n in range(4))
        for j in range(NA):
            z_next = mixer_a_dots(h_cur, j + 1) if j + 1 < NA else None
            if j == 0:
                if c + 1 < TPS:
                    norm_in(xg_cur, c + 1, h_next)
                else:
                    norm_in(xg_next, 0, h_next)
            mixer_a_block(j, z)
            z = z_next

        vz = proj(h_cur, O_RGV, RG_W)
        m_pieces = [proj(h_cur, O_M, M_PIECE)]
        s_all = jnp.concatenate([s_scr[j] for j in range(NA)], axis=1)
        ya_half = _dot(s_all, wproj_ref[WP_SC:WP_SC + D_MODEL, 0:D_MODEL])
        vbuf[RG_HIST:RG_HIST + ROWS, :] = vz
        rcw = 0.5 * rgp[RGP_CW:RGP_CW + RG_K, 0:RG_W]
        hv = 0.5 * rgp[RGP_CB:RGP_CB + 1, 0:RG_W] + vbuf[0:ROWS, :] * rcw[0:1, :]
        hv = hv + vbuf[BATCH:BATCH + ROWS, :] * rcw[1:2, :]
        hv = hv + vbuf[2 * BATCH:2 * BATCH + ROWS, :] * rcw[2:3, :]
        hv = hv + vz * rcw[3:4, :]
        vbuf[0:RG_HIST, :] = vz[ROWS - RG_HIST:ROWS, :]
        for cidx in range(NG):
            vbf[cidx] = hv[:, cidx * LANES:(cidx + 1) * LANES].astype(BF16)
        for cidx in range(NG - 1):
            lo = GATE_SHIFT + cidx * LANES
            vbs[cidx] = hv[:, lo:lo + LANES].astype(BF16)
        for k in range(NB):
            hvf[k] = hv[:, k * CB:(k + 1) * CB]

        hg, hzg = gate_dots(h_cur, 0)
        for k in range(NB):
            hg_next, hzg_next = gate_dots(h_cur, k + 1) if k + 1 < NB else (None, None)
            if 1 <= k < NB - 1:
                m_pieces.append(proj(h_cur, O_M + k * M_PIECE, M_PIECE))
            mixer_b_block(k, hg, hzg)
            hg, hzg = hg_next, hzg_next
        m = jnp.concatenate(m_pieces, axis=1)

        s2_all = jnp.concatenate([s2_scr[k] for k in range(NB)], axis=1)
        yb_half = _dot(s2_all, wproj_ref[WP_RG:WP_RG + RG_W, 0:D_MODEL])

        t_a = jnp.tanh(m[:, 0:D_MODEL] + 0.5 * bm_ref[0:1, :])
        t_b = jnp.tanh(m[:, D_MODEL:2 * D_MODEL] + 0.5 * bm_ref[1:2, :])
        merged = ((ya_half + ya_half * t_a) + (yb_half + yb_half * t_b)).astype(BF16)
        y3 = _dot(merged, wproj_ref[WP_OUT:WP_OUT + D_MODEL, 0:D_MODEL]).reshape(TS, BATCH, D_MODEL)
        block0_dots_to_scratch(h_next)
        xo = xbuf[xg_cur, c * TS:(c + 1) * TS] + mod_scr[:, 2 * D_MODEL:3 * D_MODEL] * y3
        ms = jnp.mean(xo * xo, axis=-1, keepdims=True)
        obuf[og_cur, c * TS:(c + 1) * TS] = xo * lax.rsqrt(ms + EPS) * gfinal_ref[...]

    for c in range(TPS):
        process_tile(c)

    for cp in out_copies(step):
        cp.start(priority=1)

    @pl.when(step == NSTEPS - 1)
    def _():
        for cp in in_copies(step + 2):
            cp.wait()
        for cp in out_copies(step - 1):
            cp.wait()
        for cp in out_copies(step):
            cp.wait()


def _const_spec(shape):
    nd = len(shape)
    return pl.BlockSpec(shape, lambda i, _nd=nd: (0,) * _nd, pipeline_mode=pl.Buffered(1))


def _gate_layout_constants():
    starts = np.array([_gate_win_start(c) for c in range(NG)])
    row_ch = starts[:, None] + np.arange(GATE_WIN)[None, :]
    col_ch = np.arange(NG)[:, None] * GB + np.arange(GB)[None, :]
    same_head = (row_ch // RG_HEAD_DIM)[:, :, None] == (col_ch // RG_HEAD_DIM)[:, None, :]
    spread = np.arange(LANES)[None, :, None] == (col_ch % RG_HEAD_DIM)[:, None, :]
    return jnp.asarray(spread, BF16), jnp.asarray(same_head, BF16)


def kernel(x, c, w_ada, b_ada, g_norm, w_in, sc_conv_w, sc_conv_b, sc_w_out, rg_conv_w, rg_conv_b,
           rg_w_a, rg_b_a, rg_w_x, rg_b_x, rg_lambda, rg_w_out, b_merge, w_out, g_final):
    assert x.shape == (BATCH, SEQ, D_MODEL) and w_ada.shape[0] == 1
    l = 0
    spread, same_head = _gate_layout_constants()

    vmem_operands = (
        c, b_ada[l].reshape(1, 3 * D_MODEL), g_norm[l].reshape(1, D_MODEL), g_final.reshape(1, D_MODEL),
        sc_conv_w[l], sc_conv_b[l].reshape(1, D_MODEL), b_merge[l],
        rg_w_a[l].reshape(RG_W, RG_HEAD_DIM), rg_w_x[l].reshape(RG_W, RG_HEAD_DIM), spread, same_head)
    hbm_weights = (rg_conv_w[l], rg_conv_b[l].reshape(1, RG_W), rg_b_a[l].reshape(1, RG_W),
                   rg_b_x[l].reshape(1, RG_W), rg_lambda[l].reshape(1, RG_W),
                   w_ada[l], w_in[l], sc_w_out[l], rg_w_out[l], w_out[l])
    operands = (x,) + vmem_operands + hbm_weights
    any_spec = pl.BlockSpec(memory_space=pl.ANY)
    in_specs = [any_spec] + [_const_spec(op.shape) for op in vmem_operands] + [any_spec] * len(hbm_weights)
    scratch = [
        pltpu.VMEM((8, 2 * D_MODEL), F32),
        pltpu.SemaphoreType.DMA((1,)),
        pltpu.VMEM((BATCH, 3 * D_MODEL), F32),
        pltpu.VMEM((NG, GATE_WIN, 2 * GB), BF16),
        pltpu.VMEM((D_MODEL, P_IN + PITCH_PAD), BF16),
        pltpu.VMEM((WP_ROWS, D_MODEL + PITCH_PAD), BF16),
        pltpu.VMEM((WIN_SLOTS, WIN_CHUNK, P_IN), F32),
        pltpu.VMEM((WP_SLOTS, WP_CHUNK, D_MODEL), F32),
        pltpu.SemaphoreType.DMA((WIN_SLOTS,)),
        pltpu.SemaphoreType.DMA((WP_SLOTS,)),
        pltpu.VMEM((XBUFS, GROUP_T, BATCH, D_MODEL), F32),
        pltpu.VMEM((OBUFS, GROUP_T, BATCH, D_MODEL), F32),
        pltpu.SemaphoreType.DMA((XBUFS,)),
        pltpu.SemaphoreType.DMA((OBUFS,)),
        pltpu.VMEM((2, ROWS, D_MODEL + PITCH_PAD), BF16),
        pltpu.VMEM((4, ROWS, CB), F32),
        pltpu.VMEM((NA, SC_HIST + ROWS, CB), F32),
        pltpu.VMEM((NA, ROWS, CB), BF16),
        pltpu.VMEM((RG_HIST + ROWS, RG_W), F32),
        pltpu.VMEM((NB, ROWS, CB), F32),
        pltpu.VMEM((NG, ROWS, LANES), BF16),
        pltpu.VMEM((NG - 1, ROWS, LANES), BF16),
        pltpu.VMEM((ROWS, CB), F32),
        pltpu.VMEM((ROWS, CB), F32),
        pltpu.VMEM((ROWS, CB), F32),
        pltpu.VMEM((NB, ROWS, CB), BF16),
        pltpu.VMEM((NB, BATCH, CB), F32),
    ]
    return pl.pallas_call(
        _block_kernel,
        grid=(NSTEPS,),
        in_specs=in_specs,
        out_specs=any_spec,
        out_shape=jax.ShapeDtypeStruct((BATCH, SEQ, D_MODEL), F32),
        scratch_shapes=scratch,
        compiler_params=pltpu.CompilerParams(
            dimension_semantics=("arbitrary",), vmem_limit_bytes=VMEM_LIMIT_BYTES),
        name="hybrid_block",
    )(*operands)
```

```python
import jax
import jax.numpy as jnp
import numpy as np
from jax import lax
from jax.experimental import pallas as pl
from jax.experimental.pallas import tpu as pltpu

D_MODEL = 1024
BATCH = 16
SEQ = 2048
SC_K = 3
RG_W = 1280
RG_HEADS = 16
RG_HEAD_DIM = RG_W // RG_HEADS
RG_K = 4
LRU_C = 8.0
EPS = 1e-6

LANES = 128
MXU_DIM = 256
V7X_VMEM_BYTES = 64 * 1024 * 1024
TS = 16
TPS = 2
NTILES = SEQ // TS
NSTEPS = NTILES // TPS
assert TPS % 2 == 0
ROWS = TS * BATCH
GROUP_T = TPS * TS
XBUFS = 3
OBUFS = 2
CB = MXU_DIM
NA = D_MODEL // CB
NB = RG_W // CB
GB = LANES
NG = RG_W // GB
GATE_WIN = MXU_DIM
GATE_SHIFT = LANES // 2
SC_HIST = (SC_K - 1) * BATCH
RG_HIST = (RG_K - 1) * BATCH
VMEM_LIMIT_BYTES = V7X_VMEM_BYTES - 4 * 1024 * 1024

O_SCB, O_SCC, O_SCV, O_SCG = 0, D_MODEL, 2 * D_MODEL, 3 * D_MODEL
O_RGV = 4 * D_MODEL
O_RGG = O_RGV + RG_W
O_M = O_RGG + RG_W
P_IN = O_M + 2 * D_MODEL
M_PIECE = 2 * D_MODEL // (NB - 1)

WP_SC = 0
WP_RG = WP_SC + D_MODEL
WP_OUT = WP_RG + RG_W
WP_ROWS = WP_OUT + D_MODEL

PITCH_PAD = LANES

RGP_CW, RGP_CB, RGP_BA, RGP_BX, RGP_LAM = 0, RG_K, RG_K + 1, RG_K + 2, RG_K + 3
WIN_CHUNK = 64
WIN_SLOTS = 3
WP_CHUNK = 256
WP_SLOTS = 2
WIN_COL_SCALES = ((0, O_SCG, 1.0), (O_SCG, O_RGV, 0.5), (O_RGV, O_RGG, 1.0), (O_RGG, P_IN, 0.5))

F32 = jnp.float32
BF16 = jnp.bfloat16


def _gate_win_start(c):
    lo = (c * GB) // RG_HEAD_DIM * RG_HEAD_DIM
    hi = -(-(c * GB + GB) // RG_HEAD_DIM) * RG_HEAD_DIM
    start = min(lo // GATE_SHIFT * GATE_SHIFT, RG_W - GATE_WIN)
    assert start <= lo and hi <= start + GATE_WIN
    return start


def _dot(a, b):
    return jnp.dot(a, b, preferred_element_type=F32)


def _half_silu_from_half(hz):
    return hz + hz * jnp.tanh(hz)


def _group_copies(hbm_ref, buf_ref, sem_ref, group, slot, to_hbm):
    copies = []
    for b in range(BATCH):
        hbm = hbm_ref.at[b, pl.ds(group * GROUP_T, GROUP_T), :]
        vmem = buf_ref.at[slot, :, b, :]
        src, dst = (vmem, hbm) if to_hbm else (hbm, vmem)
        copies.append(pltpu.make_async_copy(src, dst, sem_ref.at[slot]))
    return copies


def _block_kernel(x_hbm, c_ref, bada_ref, gnorm_ref, gfinal_ref, sccw_ref, sccb_ref,
                  bm_ref, rgwa_ref, rgwx_ref, spread_ref, samehead_ref,
                  rgcw_hbm, rgcb_hbm, gba_hbm, gbx_hbm, lam_hbm,
                  wada_hbm, win_hbm, scw_hbm, rgw_hbm, ow_hbm,
                  out_hbm,
                  rgp, rgp_sem, mod_scr, wg_ref, win_ref, wproj_ref, stage_in, stage_proj, win_sem, w_sem,
                  xbuf, obuf, in_sem, out_sem,
                  h_tb, z0_scr, ubuf, s_scr, vbuf, hvf, vbf, vbs, a_scr, b_scr, yrg_scr, s2_scr, hstate):
    step = pl.program_id(0)
    xg_cur = lax.rem(step, XBUFS)
    xg_next = lax.rem(step + 1, XBUFS)
    og_cur = lax.rem(step, OBUFS)

    def in_copies(group):
        if isinstance(group, int):
            return _group_copies(x_hbm, xbuf, in_sem, min(group, NSTEPS - 1), group % XBUFS, to_hbm=False)
        return _group_copies(x_hbm, xbuf, in_sem, jnp.minimum(group, NSTEPS - 1),
                             lax.rem(group, XBUFS), to_hbm=False)

    def out_copies(group):
        return _group_copies(out_hbm, obuf, out_sem, group, lax.rem(group, OBUFS), to_hbm=True)

    def norm_in(x_slot, c, h_ref):
        x3 = xbuf[x_slot, c * TS:(c + 1) * TS]
        ms = jnp.mean(x3 * x3, axis=-1, keepdims=True)
        shift = mod_scr[:, 0:D_MODEL]
        scale1 = 1.0 + mod_scr[:, D_MODEL:2 * D_MODEL]
        hn = (x3 * lax.rsqrt(ms + EPS) * gnorm_ref[...]) * scale1 + shift
        h_ref[:, 0:D_MODEL] = hn.reshape(ROWS, D_MODEL).astype(BF16)

    def proj(h_ref, col0, width):
        h = h_ref[:, 0:D_MODEL]
        outs = [_dot(h, win_ref[n]) for n in range(col0 // CB, (col0 + width) // CB)]
        return outs[0] if len(outs) == 1 else jnp.concatenate(outs, axis=1)

    def mixer_a_dots(h_ref, j):
        return tuple(proj(h_ref, o + j * CB, CB) for o in (O_SCB, O_SCC, O_SCV, O_SCG))

    def block0_dots_to_scratch(h_ref):
        for n, zn in enumerate(mixer_a_dots(h_ref, 0)):
            z0_scr[n] = zn

    def load_weights():
        n_win = D_MODEL // WIN_CHUNK

        def win_copy(i):
            slot = i % WIN_SLOTS if isinstance(i, int) else lax.rem(i, WIN_SLOTS)
            return pltpu.make_async_copy(win_hbm.at[pl.ds(i * WIN_CHUNK, WIN_CHUNK), :],
                                         stage_in.at[slot], win_sem.at[slot])

        for i in range(WIN_SLOTS - 1):
            win_copy(i).start()

        def win_body(i, carry):
            slot = lax.rem(i, WIN_SLOTS)

            @pl.when(i + WIN_SLOTS - 1 < n_win)
            def _():
                win_copy(i + WIN_SLOTS - 1).start()

            win_copy(i).wait()
            rows = pl.ds(pl.multiple_of(i * WIN_CHUNK, WIN_CHUNK), WIN_CHUNK)
            for c0, c1, scale in WIN_COL_SCALES:
                for n in range(c0 // CB, c1 // CB):
                    v = stage_in[slot, :, n * CB:(n + 1) * CB]
                    win_ref[n, rows, :] = (v if scale == 1.0 else scale * v).astype(BF16)
            return carry

        chunks = []
        for src, dst0, nrows, scale in ((scw_hbm, WP_SC, D_MODEL, 0.5), (rgw_hbm, WP_RG, RG_W, 0.5),
                                        (ow_hbm, WP_OUT, D_MODEL, 1.0)):
            chunks += [(src, r, dst0 + r, scale) for r in range(0, nrows, WP_CHUNK)]

        def proj_copy(n):
            src, r, _, _ = chunks[n]
            return pltpu.make_async_copy(src.at[r:r + WP_CHUNK, :], stage_proj.at[n % WP_SLOTS],
                                         w_sem.at[n % WP_SLOTS])

        def ada_copy(i):
            return pltpu.make_async_copy(wada_hbm.at[i * WIN_CHUNK:(i + 1) * WIN_CHUNK, :],
                                         stage_in.at[i % WIN_SLOTS, :, 0:3 * D_MODEL],
                                         win_sem.at[i % WIN_SLOTS])

        for n in range(WP_SLOTS - 1):
            proj_copy(n).start()
        lax.fori_loop(0, n_win, win_body, 0)
        for i in range(WIN_SLOTS - 1):
            ada_copy(i).start()

        for n, (_, _, dst, scale) in enumerate(chunks):
            if n + WP_SLOTS - 1 < len(chunks):
                proj_copy(n + WP_SLOTS - 1).start()
            proj_copy(n).wait()
            v = stage_proj[n % WP_SLOTS]
            wproj_ref[dst:dst + WP_CHUNK, 0:D_MODEL] = (v if scale == 1.0 else scale * v).astype(BF16)

        c = c_ref[...]
        c_act = (c * jax.nn.sigmoid(c)).astype(BF16)
        mod = jnp.broadcast_to(bada_ref[...], (BATCH, 3 * D_MODEL))
        for i in range(n_win):
            if i + WIN_SLOTS - 1 < n_win:
                ada_copy(i + WIN_SLOTS - 1).start()
            ada_copy(i).wait()
            w_chunk = stage_in[i % WIN_SLOTS, :, 0:3 * D_MODEL].astype(BF16)
            mod = mod + _dot(c_act[:, i * WIN_CHUNK:(i + 1) * WIN_CHUNK], w_chunk)
        mod_scr[...] = mod

        for cblk in range(NG):
            s0 = _gate_win_start(cblk)
            for half, w_ref in enumerate((rgwa_ref, rgwx_ref)):
                rows = w_ref[s0:s0 + GATE_WIN, :].astype(BF16)
                dense = _dot(rows, spread_ref[cblk, 0:RG_HEAD_DIM, :]) * samehead_ref[cblk].astype(F32)
                wg_ref[cblk, :, half * GB:(half + 1) * GB] = dense.astype(BF16)

    @pl.when(step == 0)
    def _():
        ubuf[:, 0:SC_HIST, :] = jnp.zeros((NA, SC_HIST, CB), F32)
        vbuf[0:RG_HIST, :] = jnp.zeros((RG_HIST, RG_W), F32)
        hstate[...] = jnp.zeros_like(hstate)
        for g in range(2):
            for cp in in_copies(g):
                cp.start()
        small = [pltpu.make_async_copy(src, rgp.at[r0:r0 + src.shape[0], 0:RG_W], rgp_sem.at[0])
                 for src, r0 in ((rgcw_hbm, RGP_CW), (rgcb_hbm, RGP_CB), (gba_hbm, RGP_BA), (gbx_hbm, RGP_BX),
                                 (lam_hbm, RGP_LAM))]
        for cp in small:
            cp.start()
        load_weights()
        for cp in small:
            cp.wait()
        for cp in in_copies(0):
            cp.wait()
        norm_in(0, 0, h_tb.at[0])
        block0_dots_to_scratch(h_tb.at[0])

    @pl.when(step >= OBUFS)
    def _():
        for cp in out_copies(step - OBUFS):
            cp.wait()

    for cp in in_copies(step + 2):
        cp.start()
    for cp in in_copies(step + 1):
        cp.wait()

    def mixer_a_block(j, z):
        zb, zc, zv, hzg = z
        u = zc * zv
        ubuf[j, SC_HIST:SC_HIST + ROWS, :] = u
        cw = sccw_ref[:, j * CB:(j + 1) * CB]
        conv = sccb_ref[:, j * CB:(j + 1) * CB] + ubuf[j, 0:ROWS, :] * cw[0:1, :]
        conv = conv + ubuf[j, BATCH:BATCH + ROWS, :] * cw[1:2, :]
        conv = conv + u * cw[2:3, :]
        ubuf[j, 0:SC_HIST, :] = u[ROWS - SC_HIST:ROWS, :]
        s_scr[j] = (zb * conv * _half_silu_from_half(hzg)).astype(BF16)

    def gate_dot(c):
        start = _gate_win_start(c)
        slabs = vbf if start % LANES == 0 else vbs
        s0 = start // LANES
        lhs = jnp.concatenate([slabs[s0 + i] for i in range(GATE_WIN // LANES)], axis=1)
        return _dot(lhs, wg_ref[c])

    def gate_dots(h_cur, k):
        pieces = [gate_dot(c) for c in range(k * CB // GB, (k + 1) * CB // GB)]
        hg_a = jnp.concatenate([p[:, 0:GB] for p in pieces], axis=1)
        hg_x = jnp.concatenate([p[:, GB:2 * GB] for p in pieces], axis=1)
        return (hg_a, hg_x), proj(h_cur, O_RGG + k * CB, CB)

    def mixer_b_block(k, hg, hzg):
        hg_a, hg_x = hg
        cols = slice(k * CB, (k + 1) * CB)
        t_r = jnp.tanh(hg_a + 0.5 * rgp[RGP_BA:RGP_BA + 1, cols])
        t_i = jnp.tanh(hg_x + 0.5 * rgp[RGP_BX:RGP_BX + 1, cols])
        half_k = (-0.5 * LRU_C) * jax.nn.softplus(-rgp[RGP_LAM:RGP_LAM + 1, cols])
        log_a = half_k + half_k * t_r
        a = jnp.exp(log_a)
        a_scr[...] = a
        one_minus_a2 = jnp.tanh(log_a) * (-1.0 - a * a)
        root = jnp.where(one_minus_a2 > 0.0, one_minus_a2 * lax.rsqrt(one_minus_a2), 0.0)
        hvk = hvf[k]
        b_scr[...] = root * (hvk + hvk * t_i)
        h = hstate[k]
        for t in range(TS):
            rows = slice(t * BATCH, (t + 1) * BATCH)
            h = a_scr[rows, :] * h + b_scr[rows, :]
            yrg_scr[rows, :] = h
        hstate[k] = h
        s2_scr[k] = (yrg_scr[...] * _half_silu_from_half(hzg)).astype(BF16)

    def process_tile(c):
        h_cur = h_tb.at[c % 2]
        h_next = h_tb.at[(c + 1) % 2]

        z = tuple(z0_scr[n] for n in range(4))
        for j in range(NA):
            z_next = mixer_a_dots(h_cur, j + 1) if j + 1 < NA else None
            if j == 0:
                if c + 1 < TPS:
                    norm_in(xg_cur, c + 1, h_next)
                else:
                    norm_in(xg_next, 0, h_next)
            mixer_a_block(j, z)
            z = z_next

        vz = proj(h_cur, O_RGV, RG_W)
        m_pieces = [proj(h_cur, O_M, M_PIECE)]
        s_all = jnp.concatenate([s_scr[j] for j in range(NA)], axis=1)
        ya_half = _dot(s_all, wproj_ref[WP_SC:WP_SC + D_MODEL, 0:D_MODEL])
        vbuf[RG_HIST:RG_HIST + ROWS, :] = vz
        rcw = 0.5 * rgp[RGP_CW:RGP_CW + RG_K, 0:RG_W]
        hv = 0.5 * rgp[RGP_CB:RGP_CB + 1, 0:RG_W] + vbuf[0:ROWS, :] * rcw[0:1, :]
        hv = hv + vbuf[BATCH:BATCH + ROWS, :] * rcw[1:2, :]
        hv = hv + vbuf[2 * BATCH:2 * BATCH + ROWS, :] * rcw[2:3, :]
        hv = hv + vz * rcw[3:4, :]
        vbuf[0:RG_HIST, :] = vz[ROWS - RG_HIST:ROWS, :]
        for cidx in range(NG):
            vbf[cidx] = hv[:, cidx * LANES:(cidx + 1) * LANES].astype(BF16)
        for cidx in range(NG - 1):
            lo = GATE_SHIFT + cidx * LANES
            vbs[cidx] = hv[:, lo:lo + LANES].astype(BF16)
        for k in range(NB):
            hvf[k] = hv[:, k * CB:(k + 1) * CB]

        hg, hzg = gate_dots(h_cur, 0)
        for k in range(NB):
            hg_next, hzg_next = gate_dots(h_cur, k + 1) if k + 1 < NB else (None, None)
            if 1 <= k < NB - 1:
                m_pieces.append(proj(h_cur, O_M + k * M_PIECE, M_PIECE))
            mixer_b_block(k, hg, hzg)
            hg, hzg = hg_next, hzg_next
        m = jnp.concatenate(m_pieces, axis=1)

        s2_all = jnp.concatenate([s2_scr[k] for k in range(NB)], axis=1)
        yb_half = _dot(s2_all, wproj_ref[WP_RG:WP_RG + RG_W, 0:D_MODEL])

        t_a = jnp.tanh(m[:, 0:D_MODEL] + 0.5 * bm_ref[0:1, :])
        t_b = jnp.tanh(m[:, D_MODEL:2 * D_MODEL] + 0.5 * bm_ref[1:2, :])
        merged = ((ya_half + ya_half * t_a) + (yb_half + yb_half * t_b)).astype(BF16)
        y3 = _dot(merged, wproj_ref[WP_OUT:WP_OUT + D_MODEL, 0:D_MODEL]).reshape(TS, BATCH, D_MODEL)
        block0_dots_to_scratch(h_next)
        xo = xbuf[xg_cur, c * TS:(c + 1) * TS] + mod_scr[:, 2 * D_MODEL:3 * D_MODEL] * y3
        ms = jnp.mean(xo * xo, axis=-1, keepdims=True)
        obuf[og_cur, c * TS:(c + 1) * TS] = xo * lax.rsqrt(ms + EPS) * gfinal_ref[...]

    for c in range(TPS):
        process_tile(c)

    for cp in out_copies(step):
        cp.start()

    @pl.when(step == NSTEPS - 1)
    def _():
        for cp in in_copies(step + 2):
            cp.wait()
        for cp in out_copies(step - 1):
            cp.wait()
        for cp in out_copies(step):
            cp.wait()


def _const_spec(shape):
    nd = len(shape)
    return pl.BlockSpec(shape, lambda i, _nd=nd: (0,) * _nd, pipeline_mode=pl.Buffered(1))


def _gate_layout_constants():
    starts = np.array([_gate_win_start(c) for c in range(NG)])
    row_ch = starts[:, None] + np.arange(GATE_WIN)[None, :]
    col_ch = np.arange(NG)[:, None] * GB + np.arange(GB)[None, :]
    same_head = (row_ch // RG_HEAD_DIM)[:, :, None] == (col_ch // RG_HEAD_DIM)[:, None, :]
    spread = np.arange(LANES)[None, :, None] == (col_ch % RG_HEAD_DIM)[:, None, :]
    return jnp.asarray(spread, BF16), jnp.asarray(same_head, BF16)


def kernel(x, c, w_ada, b_ada, g_norm, w_in, sc_conv_w, sc_conv_b, sc_w_out, rg_conv_w, rg_conv_b,
           rg_w_a, rg_b_a, rg_w_x, rg_b_x, rg_lambda, rg_w_out, b_merge, w_out, g_final):
    assert x.shape == (BATCH, SEQ, D_MODEL) and w_ada.shape[0] == 1
    l = 0
    spread, same_head = _gate_layout_constants()

    vmem_operands = (
        c, b_ada[l].reshape(1, 3 * D_MODEL), g_norm[l].reshape(1, D_MODEL), g_final.reshape(1, D_MODEL),
        sc_conv_w[l], sc_conv_b[l].reshape(1, D_MODEL), b_merge[l],
        rg_w_a[l].reshape(RG_W, RG_HEAD_DIM), rg_w_x[l].reshape(RG_W, RG_HEAD_DIM), spread, same_head)
    hbm_weights = (rg_conv_w[l], rg_conv_b[l].reshape(1, RG_W), rg_b_a[l].reshape(1, RG_W),
                   rg_b_x[l].reshape(1, RG_W), rg_lambda[l].reshape(1, RG_W),
                   w_ada[l], w_in[l], sc_w_out[l], rg_w_out[l], w_out[l])
    operands = (x,) + vmem_operands + hbm_weights
    any_spec = pl.BlockSpec(memory_space=pl.ANY)
    in_specs = [any_spec] + [_const_spec(op.shape) for op in vmem_operands] + [any_spec] * len(hbm_weights)
    scratch = [
        pltpu.VMEM((8, 2 * D_MODEL), F32),
        pltpu.SemaphoreType.DMA((1,)),
        pltpu.VMEM((BATCH, 3 * D_MODEL), F32),
        pltpu.VMEM((NG, GATE_WIN, 2 * GB), BF16),
        pltpu.VMEM((P_IN // CB, D_MODEL, CB), BF16),
        pltpu.VMEM((WP_ROWS, D_MODEL + PITCH_PAD), BF16),
        pltpu.VMEM((WIN_SLOTS, WIN_CHUNK, P_IN), F32),
        pltpu.VMEM((WP_SLOTS, WP_CHUNK, D_MODEL), F32),
        pltpu.SemaphoreType.DMA((WIN_SLOTS,)),
        pltpu.SemaphoreType.DMA((WP_SLOTS,)),
        pltpu.VMEM((XBUFS, GROUP_T, BATCH, D_MODEL), F32),
        pltpu.VMEM((OBUFS, GROUP_T, BATCH, D_MODEL), F32),
        pltpu.SemaphoreType.DMA((XBUFS,)),
        pltpu.SemaphoreType.DMA((OBUFS,)),
        pltpu.VMEM((2, ROWS, D_MODEL + PITCH_PAD), BF16),
        pltpu.VMEM((4, ROWS, CB), F32),
        pltpu.VMEM((NA, SC_HIST + ROWS, CB), F32),
        pltpu.VMEM((NA, ROWS, CB), BF16),
        pltpu.VMEM((RG_HIST + ROWS, RG_W), F32),
        pltpu.VMEM((NB, ROWS, CB), F32),
        pltpu.VMEM((NG, ROWS, LANES), BF16),
        pltpu.VMEM((NG - 1, ROWS, LANES), BF16),
        pltpu.VMEM((ROWS, CB), F32),
        pltpu.VMEM((ROWS, CB), F32),
        pltpu.VMEM((ROWS, CB), F32),
        pltpu.VMEM((NB, ROWS, CB), BF16),
        pltpu.VMEM((NB, BATCH, CB), F32),
    ]
    return pl.pallas_call(
        _block_kernel,
        grid=(NSTEPS,),
        in_specs=in_specs,
        out_specs=any_spec,
        out_shape=jax.ShapeDtypeStruct((BATCH, SEQ, D_MODEL), F32),
        scratch_shapes=scratch,
        compiler_params=pltpu.CompilerParams(
            dimension_semantics=("arbitrary",), vmem_limit_bytes=VMEM_LIMIT_BYTES),
        name="hybrid_block",
    )(*operands)
```

```python
import jax
import jax.numpy as jnp
import numpy as np
from jax import lax
from jax.experimental import pallas as pl
from jax.experimental.pallas import tpu as pltpu

D_MODEL = 1024
BATCH = 16
SEQ = 2048
SC_K = 3
RG_W = 1280
RG_HEADS = 16
RG_HEAD_DIM = RG_W // RG_HEADS
RG_K = 4
LRU_C = 8.0
EPS = 1e-6

LANES = 128
MXU_DIM = 256
V7X_VMEM_BYTES = 64 * 1024 * 1024
TS = 16
TPS = 2
NTILES = SEQ // TS
NSTEPS = NTILES // TPS
assert TPS % 2 == 0
ROWS = TS * BATCH
GROUP_T = TPS * TS
XBUFS = 3
OBUFS = 2
CB = MXU_DIM
NA = D_MODEL // CB
NB = RG_W // CB
GB = LANES
NG = RG_W // GB
GATE_WIN = MXU_DIM
GATE_SHIFT = LANES // 2
SC_HIST = (SC_K - 1) * BATCH
RG_HIST = (RG_K - 1) * BATCH
VMEM_LIMIT_BYTES = V7X_VMEM_BYTES - 4 * 1024 * 1024

O_SCB, O_SCC, O_SCV, O_SCG = 0, D_MODEL, 2 * D_MODEL, 3 * D_MODEL
O_RGV = 4 * D_MODEL
O_RGG = O_RGV + RG_W
O_M = O_RGG + RG_W
P_IN = O_M + 2 * D_MODEL
M_PIECE = 2 * D_MODEL // (NB - 1)

WP_SC = 0
WP_RG = WP_SC + D_MODEL
WP_OUT = WP_RG + RG_W
WP_ROWS = WP_OUT + D_MODEL

PITCH_PAD = LANES

RGP_CW, RGP_CB, RGP_BA, RGP_BX, RGP_LAM = 0, RG_K, RG_K + 1, RG_K + 2, RG_K + 3
WIN_CHUNK = 64
WIN_SLOTS = 3
WP_CHUNK = 256
WP_SLOTS = 2
WIN_COL_SCALES = ((0, O_SCG, 1.0), (O_SCG, O_RGV, 0.5), (O_RGV, O_RGG, 1.0), (O_RGG, P_IN, 0.5))

F32 = jnp.float32
BF16 = jnp.bfloat16


def _gate_win_start(c):
    lo = (c * GB) // RG_HEAD_DIM * RG_HEAD_DIM
    hi = -(-(c * GB + GB) // RG_HEAD_DIM) * RG_HEAD_DIM
    start = min(lo // GATE_SHIFT * GATE_SHIFT, RG_W - GATE_WIN)
    assert start <= lo and hi <= start + GATE_WIN
    return start


def _dot(a, b):
    return jnp.dot(a, b, preferred_element_type=F32)


def _half_silu_from_half(hz):
    return hz + hz * jnp.tanh(hz)


def _group_copies(hbm_ref, buf_ref, sem_ref, group, slot, to_hbm):
    copies = []
    for b in range(BATCH):
        hbm = hbm_ref.at[b, pl.ds(group * GROUP_T, GROUP_T), :]
        vmem = buf_ref.at[slot, :, b, :]
        src, dst = (vmem, hbm) if to_hbm else (hbm, vmem)
        copies.append(pltpu.make_async_copy(src, dst, sem_ref.at[slot]))
    return copies


def _block_kernel(x_hbm, c_ref, bada_ref, gnorm_ref, gfinal_ref, sccw_ref, sccb_ref,
                  bm_ref, rgwa_ref, rgwx_ref, spread_ref, samehead_ref,
                  rgcw_hbm, rgcb_hbm, gba_hbm, gbx_hbm, lam_hbm,
                  wada_hbm, win_hbm, scw_hbm, rgw_hbm, ow_hbm,
                  out_hbm,
                  rgp, rgp_sem, mod_scr, wg_ref, win_ref, wproj_ref, stage_in, stage_proj, win_sem, w_sem,
                  xbuf, obuf, in_sem, out_sem,
                  h_tb, z0_scr, ubuf, s_scr, vbuf, hvf, vbf, vbs, a_scr, b_scr, yrg_scr, s2_scr, hstate):
    step = pl.program_id(0)
    xg_cur = lax.rem(step, XBUFS)
    xg_next = lax.rem(step + 1, XBUFS)
    og_cur = lax.rem(step, OBUFS)

    def in_copies(group):
        if isinstance(group, int):
            return _group_copies(x_hbm, xbuf, in_sem, min(group, NSTEPS - 1), group % XBUFS, to_hbm=False)
        return _group_copies(x_hbm, xbuf, in_sem, jnp.minimum(group, NSTEPS - 1),
                             lax.rem(group, XBUFS), to_hbm=False)

    def out_copies(group):
        return _group_copies(out_hbm, obuf, out_sem, group, lax.rem(group, OBUFS), to_hbm=True)

    def norm_in(x_slot, c, h_ref):
        x3 = xbuf[x_slot, c * TS:(c + 1) * TS]
        ms = jnp.mean(x3 * x3, axis=-1, keepdims=True)
        shift = mod_scr[:, 0:D_MODEL]
        scale1 = 1.0 + mod_scr[:, D_MODEL:2 * D_MODEL]
        hn = (x3 * lax.rsqrt(ms + EPS) * gnorm_ref[...]) * scale1 + shift
        hb = hn.reshape(ROWS, D_MODEL).astype(BF16)
        for kb in range(D_MODEL // CB):
            h_ref[kb] = hb[:, kb * CB:(kb + 1) * CB]

    def proj(h_ref, col0, width):
        h = jnp.concatenate([h_ref[kb] for kb in range(D_MODEL // CB)], axis=1)
        outs = [_dot(h, win_ref[n]) for n in range(col0 // CB, (col0 + width) // CB)]
        return outs[0] if len(outs) == 1 else jnp.concatenate(outs, axis=1)

    def mixer_a_dots(h_ref, j):
        return tuple(proj(h_ref, o + j * CB, CB) for o in (O_SCB, O_SCC, O_SCV, O_SCG))

    def block0_dots_to_scratch(h_ref):
        for n, zn in enumerate(mixer_a_dots(h_ref, 0)):
            z0_scr[n] = zn

    def load_weights():
        n_win = D_MODEL // WIN_CHUNK

        def win_copy(i):
            slot = i % WIN_SLOTS if isinstance(i, int) else lax.rem(i, WIN_SLOTS)
            return pltpu.make_async_copy(win_hbm.at[pl.ds(i * WIN_CHUNK, WIN_CHUNK), :],
                                         stage_in.at[slot], win_sem.at[slot])

        for i in range(WIN_SLOTS - 1):
            win_copy(i).start()

        def win_body(i, carry):
            slot = lax.rem(i, WIN_SLOTS)

            @pl.when(i + WIN_SLOTS - 1 < n_win)
            def _():
                win_copy(i + WIN_SLOTS - 1).start()

            win_copy(i).wait()
            rows = pl.ds(pl.multiple_of(i * WIN_CHUNK, WIN_CHUNK), WIN_CHUNK)
            for c0, c1, scale in WIN_COL_SCALES:
                for n in range(c0 // CB, c1 // CB):
                    v = stage_in[slot, :, n * CB:(n + 1) * CB]
                    win_ref[n, rows, :] = (v if scale == 1.0 else scale * v).astype(BF16)
            return carry

        chunks = []
        for src, dst0, nrows, scale in ((scw_hbm, WP_SC, D_MODEL, 0.5), (rgw_hbm, WP_RG, RG_W, 0.5),
                                        (ow_hbm, WP_OUT, D_MODEL, 1.0)):
            chunks += [(src, r, dst0 + r, scale) for r in range(0, nrows, WP_CHUNK)]

        def proj_copy(n):
            src, r, _, _ = chunks[n]
            return pltpu.make_async_copy(src.at[r:r + WP_CHUNK, :], stage_proj.at[n % WP_SLOTS],
                                         w_sem.at[n % WP_SLOTS])

        def ada_copy(i):
            return pltpu.make_async_copy(wada_hbm.at[i * WIN_CHUNK:(i + 1) * WIN_CHUNK, :],
                                         stage_in.at[i % WIN_SLOTS, :, 0:3 * D_MODEL],
                                         win_sem.at[i % WIN_SLOTS])

        for n in range(WP_SLOTS - 1):
            proj_copy(n).start()
        lax.fori_loop(0, n_win, win_body, 0)
        for i in range(WIN_SLOTS - 1):
            ada_copy(i).start()

        for n, (_, _, dst, scale) in enumerate(chunks):
            if n + WP_SLOTS - 1 < len(chunks):
                proj_copy(n + WP_SLOTS - 1).start()
            proj_copy(n).wait()
            v = stage_proj[n % WP_SLOTS]
            wproj_ref[dst:dst + WP_CHUNK, 0:D_MODEL] = (v if scale == 1.0 else scale * v).astype(BF16)

        c = c_ref[...]
        c_act = (c * jax.nn.sigmoid(c)).astype(BF16)
        mod = jnp.broadcast_to(bada_ref[...], (BATCH, 3 * D_MODEL))
        for i in range(n_win):
            if i + WIN_SLOTS - 1 < n_win:
                ada_copy(i + WIN_SLOTS - 1).start()
            ada_copy(i).wait()
            w_chunk = stage_in[i % WIN_SLOTS, :, 0:3 * D_MODEL].astype(BF16)
            mod = mod + _dot(c_act[:, i * WIN_CHUNK:(i + 1) * WIN_CHUNK], w_chunk)
        mod_scr[...] = mod

        for cblk in range(NG):
            s0 = _gate_win_start(cblk)
            for half, w_ref in enumerate((rgwa_ref, rgwx_ref)):
                rows = w_ref[s0:s0 + GATE_WIN, :].astype(BF16)
                dense = _dot(rows, spread_ref[cblk, 0:RG_HEAD_DIM, :]) * samehead_ref[cblk].astype(F32)
                wg_ref[cblk, :, half * GB:(half + 1) * GB] = dense.astype(BF16)

    @pl.when(step == 0)
    def _():
        ubuf[:, 0:SC_HIST, :] = jnp.zeros((NA, SC_HIST, CB), F32)
        vbuf[0:RG_HIST, :] = jnp.zeros((RG_HIST, RG_W), F32)
        hstate[...] = jnp.zeros_like(hstate)
        for g in range(2):
            for cp in in_copies(g):
                cp.start()
        small = [pltpu.make_async_copy(src, rgp.at[r0:r0 + src.shape[0], 0:RG_W], rgp_sem.at[0])
                 for src, r0 in ((rgcw_hbm, RGP_CW), (rgcb_hbm, RGP_CB), (gba_hbm, RGP_BA), (gbx_hbm, RGP_BX),
                                 (lam_hbm, RGP_LAM))]
        for cp in small:
            cp.start()
        load_weights()
        for cp in small:
            cp.wait()
        for cp in in_copies(0):
            cp.wait()
        norm_in(0, 0, h_tb.at[0])
        block0_dots_to_scratch(h_tb.at[0])

    @pl.when(step >= OBUFS)
    def _():
        for cp in out_copies(step - OBUFS):
            cp.wait()

    for cp in in_copies(step + 2):
        cp.start()
    for cp in in_copies(step + 1):
        cp.wait()

    def mixer_a_block(j, z):
        zb, zc, zv, hzg = z
        u = zc * zv
        ubuf[j, SC_HIST:SC_HIST + ROWS, :] = u
        cw = sccw_ref[:, j * CB:(j + 1) * CB]
        conv = sccb_ref[:, j * CB:(j + 1) * CB] + ubuf[j, 0:ROWS, :] * cw[0:1, :]
        conv = conv + ubuf[j, BATCH:BATCH + ROWS, :] * cw[1:2, :]
        conv = conv + u * cw[2:3, :]
        ubuf[j, 0:SC_HIST, :] = u[ROWS - SC_HIST:ROWS, :]
        s_scr[j] = (zb * conv * _half_silu_from_half(hzg)).astype(BF16)

    def gate_dot(c):
        start = _gate_win_start(c)
        slabs = vbf if start % LANES == 0 else vbs
        s0 = start // LANES
        lhs = jnp.concatenate([slabs[s0 + i] for i in range(GATE_WIN // LANES)], axis=1)
        return _dot(lhs, wg_ref[c])

    def gate_dots(h_cur, k):
        pieces = [gate_dot(c) for c in range(k * CB // GB, (k + 1) * CB // GB)]
        hg_a = jnp.concatenate([p[:, 0:GB] for p in pieces], axis=1)
        hg_x = jnp.concatenate([p[:, GB:2 * GB] for p in pieces], axis=1)
        return (hg_a, hg_x), proj(h_cur, O_RGG + k * CB, CB)

    def mixer_b_block(k, hg, hzg):
        hg_a, hg_x = hg
        cols = slice(k * CB, (k + 1) * CB)
        t_r = jnp.tanh(hg_a + 0.5 * rgp[RGP_BA:RGP_BA + 1, cols])
        t_i = jnp.tanh(hg_x + 0.5 * rgp[RGP_BX:RGP_BX + 1, cols])
        half_k = (-0.5 * LRU_C) * jax.nn.softplus(-rgp[RGP_LAM:RGP_LAM + 1, cols])
        log_a = half_k + half_k * t_r
        a = jnp.exp(log_a)
        a_scr[...] = a
        one_minus_a2 = jnp.tanh(log_a) * (-1.0 - a * a)
        root = jnp.where(one_minus_a2 > 0.0, one_minus_a2 * lax.rsqrt(one_minus_a2), 0.0)
        hvk = hvf[k]
        b_scr[...] = root * (hvk + hvk * t_i)
        h = hstate[k]
        for t in range(TS):
            rows = slice(t * BATCH, (t + 1) * BATCH)
            h = a_scr[rows, :] * h + b_scr[rows, :]
            yrg_scr[rows, :] = h
        hstate[k] = h
        s2_scr[k] = (yrg_scr[...] * _half_silu_from_half(hzg)).astype(BF16)

    def process_tile(c):
        h_cur = h_tb.at[c % 2]
        h_next = h_tb.at[(c + 1) % 2]

        z = tuple(z0_scr[n] for n in range(4))
        for j in range(NA):
            z_next = mixer_a_dots(h_cur, j + 1) if j + 1 < NA else None
            if j == 0:
                if c + 1 < TPS:
                    norm_in(xg_cur, c + 1, h_next)
                else:
                    norm_in(xg_next, 0, h_next)
            mixer_a_block(j, z)
            z = z_next

        vz = proj(h_cur, O_RGV, RG_W)
        m_pieces = [proj(h_cur, O_M, M_PIECE)]
        s_all = jnp.concatenate([s_scr[j] for j in range(NA)], axis=1)
        ya_half = _dot(s_all, wproj_ref[WP_SC:WP_SC + D_MODEL, 0:D_MODEL])
        vbuf[RG_HIST:RG_HIST + ROWS, :] = vz
        rcw = 0.5 * rgp[RGP_CW:RGP_CW + RG_K, 0:RG_W]
        hv = 0.5 * rgp[RGP_CB:RGP_CB + 1, 0:RG_W] + vbuf[0:ROWS, :] * rcw[0:1, :]
        hv = hv + vbuf[BATCH:BATCH + ROWS, :] * rcw[1:2, :]
        hv = hv + vbuf[2 * BATCH:2 * BATCH + ROWS, :] * rcw[2:3, :]
        hv = hv + vz * rcw[3:4, :]
        vbuf[0:RG_HIST, :] = vz[ROWS - RG_HIST:ROWS, :]
        for cidx in range(NG):
            vbf[cidx] = hv[:, cidx * LANES:(cidx + 1) * LANES].astype(BF16)
        for cidx in range(NG - 1):
            lo = GATE_SHIFT + cidx * LANES
            vbs[cidx] = hv[:, lo:lo + LANES].astype(BF16)
        for k in range(NB):
            hvf[k] = hv[:, k * CB:(k + 1) * CB]

        hg, hzg = gate_dots(h_cur, 0)
        for k in range(NB):
            hg_next, hzg_next = gate_dots(h_cur, k + 1) if k + 1 < NB else (None, None)
            if 1 <= k < NB - 1:
                m_pieces.append(proj(h_cur, O_M + k * M_PIECE, M_PIECE))
            mixer_b_block(k, hg, hzg)
            hg, hzg = hg_next, hzg_next
        m = jnp.concatenate(m_pieces, axis=1)

        s2_all = jnp.concatenate([s2_scr[k] for k in range(NB)], axis=1)
        yb_half = _dot(s2_all, wproj_ref[WP_RG:WP_RG + RG_W, 0:D_MODEL])

        t_a = jnp.tanh(m[:, 0:D_MODEL] + 0.5 * bm_ref[0:1, :])
        t_b = jnp.tanh(m[:, D_MODEL:2 * D_MODEL] + 0.5 * bm_ref[1:2, :])
        merged = ((ya_half + ya_half * t_a) + (yb_half + yb_half * t_b)).astype(BF16)
        y3 = _dot(merged, wproj_ref[WP_OUT:WP_OUT + D_MODEL, 0:D_MODEL]).reshape(TS, BATCH, D_MODEL)
        block0_dots_to_scratch(h_next)
        xo = xbuf[xg_cur, c * TS:(c + 1) * TS] + mod_scr[:, 2 * D_MODEL:3 * D_MODEL] * y3
        ms = jnp.mean(xo * xo, axis=-1, keepdims=True)
        obuf[og_cur, c * TS:(c + 1) * TS] = xo * lax.rsqrt(ms + EPS) * gfinal_ref[...]

    for c in range(TPS):
        process_tile(c)

    for cp in out_copies(step):
        cp.start()

    @pl.when(step == NSTEPS - 1)
    def _():
        for cp in in_copies(step + 2):
            cp.wait()
        for cp in out_copies(step - 1):
            cp.wait()
        for cp in out_copies(step):
            cp.wait()


def _const_spec(shape):
    nd = len(shape)
    return pl.BlockSpec(shape, lambda i, _nd=nd: (0,) * _nd, pipeline_mode=pl.Buffered(1))


def _gate_layout_constants():
    starts = np.array([_gate_win_start(c) for c in range(NG)])
    row_ch = starts[:, None] + np.arange(GATE_WIN)[None, :]
    col_ch = np.arange(NG)[:, None] * GB + np.arange(GB)[None, :]
    same_head = (row_ch // RG_HEAD_DIM)[:, :, None] == (col_ch // RG_HEAD_DIM)[:, None, :]
    spread = np.arange(LANES)[None, :, None] == (col_ch % RG_HEAD_DIM)[:, None, :]
    return jnp.asarray(spread, BF16), jnp.asarray(same_head, BF16)


def kernel(x, c, w_ada, b_ada, g_norm, w_in, sc_conv_w, sc_conv_b, sc_w_out, rg_conv_w, rg_conv_b,
           rg_w_a, rg_b_a, rg_w_x, rg_b_x, rg_lambda, rg_w_out, b_merge, w_out, g_final):
    assert x.shape == (BATCH, SEQ, D_MODEL) and w_ada.shape[0] == 1
    l = 0
    spread, same_head = _gate_layout_constants()

    vmem_operands = (
        c, b_ada[l].reshape(1, 3 * D_MODEL), g_norm[l].reshape(1, D_MODEL), g_final.reshape(1, D_MODEL),
        sc_conv_w[l], sc_conv_b[l].reshape(1, D_MODEL), b_merge[l],
        rg_w_a[l].reshape(RG_W, RG_HEAD_DIM), rg_w_x[l].reshape(RG_W, RG_HEAD_DIM), spread, same_head)
    hbm_weights = (rg_conv_w[l], rg_conv_b[l].reshape(1, RG_W), rg_b_a[l].reshape(1, RG_W),
                   rg_b_x[l].reshape(1, RG_W), rg_lambda[l].reshape(1, RG_W),
                   w_ada[l], w_in[l], sc_w_out[l], rg_w_out[l], w_out[l])
    operands = (x,) + vmem_operands + hbm_weights
    any_spec = pl.BlockSpec(memory_space=pl.ANY)
    in_specs = [any_spec] + [_const_spec(op.shape) for op in vmem_operands] + [any_spec] * len(hbm_weights)
    scratch = [
        pltpu.VMEM((8, 2 * D_MODEL), F32),
        pltpu.SemaphoreType.DMA((1,)),
        pltpu.VMEM((BATCH, 3 * D_MODEL), F32),
        pltpu.VMEM((NG, GATE_WIN, 2 * GB), BF16),
        pltpu.VMEM((P_IN // CB, D_MODEL, CB), BF16),
        pltpu.VMEM((WP_ROWS, D_MODEL + PITCH_PAD), BF16),
        pltpu.VMEM((WIN_SLOTS, WIN_CHUNK, P_IN), F32),
        pltpu.VMEM((WP_SLOTS, WP_CHUNK, D_MODEL), F32),
        pltpu.SemaphoreType.DMA((WIN_SLOTS,)),
        pltpu.SemaphoreType.DMA((WP_SLOTS,)),
        pltpu.VMEM((XBUFS, GROUP_T, BATCH, D_MODEL), F32),
        pltpu.VMEM((OBUFS, GROUP_T, BATCH, D_MODEL), F32),
        pltpu.SemaphoreType.DMA((XBUFS,)),
        pltpu.SemaphoreType.DMA((OBUFS,)),
        pltpu.VMEM((2, D_MODEL // CB, ROWS, CB), BF16),
        pltpu.VMEM((4, ROWS, CB), F32),
        pltpu.VMEM((NA, SC_HIST + ROWS, CB), F32),
        pltpu.VMEM((NA, ROWS, CB), BF16),
        pltpu.VMEM((RG_HIST + ROWS, RG_W), F32),
        pltpu.VMEM((NB, ROWS, CB), F32),
        pltpu.VMEM((NG, ROWS, LANES), BF16),
        pltpu.VMEM((NG - 1, ROWS, LANES), BF16),
        pltpu.VMEM((ROWS, CB), F32),
        pltpu.VMEM((ROWS, CB), F32),
        pltpu.VMEM((ROWS, CB), F32),
        pltpu.VMEM((NB, ROWS, CB), BF16),
        pltpu.VMEM((NB, BATCH, CB), F32),
    ]
    return pl.pallas_call(
        _block_kernel,
        grid=(NSTEPS,),
        in_specs=in_specs,
        out_specs=any_spec,
        out_shape=jax.ShapeDtypeStruct((BATCH, SEQ, D_MODEL), F32),
        scratch_shapes=scratch,
        compiler_params=pltpu.CompilerParams(
            dimension_semantics=("arbitrary",), vmem_limit_bytes=VMEM_LIMIT_BYTES),
        name="hybrid_block",
    )(*operands)
```

```python
import jax
import jax.numpy as jnp
import numpy as np
from jax import lax
from jax.experimental import pallas as pl
from jax.experimental.pallas import tpu as pltpu

D_MODEL = 1024
BATCH = 16
SEQ = 2048
SC_K = 3
RG_W = 1280
RG_HEADS = 16
RG_HEAD_DIM = RG_W // RG_HEADS
RG_K = 4
LRU_C = 8.0
EPS = 1e-6

LANES = 128
MXU_DIM = 256
V7X_VMEM_BYTES = 64 * 1024 * 1024
TS = 16
TPS = 2
NTILES = SEQ // TS
NSTEPS = NTILES // TPS
assert TPS % 2 == 0
ROWS = TS * BATCH
GROUP_T = TPS * TS
XBUFS = 3
OBUFS = 2
CB = MXU_DIM
NA = D_MODEL // CB
NB = RG_W // CB
GB = LANES
NG = RG_W // GB
GATE_WIN = MXU_DIM
GATE_SHIFT = LANES // 2
SC_HIST = (SC_K - 1) * BATCH
RG_HIST = (RG_K - 1) * BATCH
VMEM_LIMIT_BYTES = V7X_VMEM_BYTES - 4 * 1024 * 1024

O_SCB, O_SCC, O_SCV, O_SCG = 0, D_MODEL, 2 * D_MODEL, 3 * D_MODEL
O_RGV = 4 * D_MODEL
O_RGG = O_RGV + RG_W
O_M = O_RGG + RG_W
P_IN = O_M + 2 * D_MODEL
M_PIECE = 2 * D_MODEL // (NB - 1)

WP_SC = 0
WP_RG = WP_SC + D_MODEL
WP_OUT = WP_RG + RG_W
WP_ROWS = WP_OUT + D_MODEL

RGP_CW, RGP_CB, RGP_BA, RGP_BX, RGP_LAM = 0, RG_K, RG_K + 1, RG_K + 2, RG_K + 3
WIN_CHUNK = 64
WIN_SLOTS = 3
WP_CHUNK = 256
WP_SLOTS = 3
WIN_COL_SCALES = ((0, O_SCG, 1.0), (O_SCG, O_RGV, 0.5), (O_RGV, O_RGG, 1.0), (O_RGG, P_IN, 0.5))

F32 = jnp.float32
BF16 = jnp.bfloat16


def _gate_win_start(c):
    lo = (c * GB) // RG_HEAD_DIM * RG_HEAD_DIM
    hi = -(-(c * GB + GB) // RG_HEAD_DIM) * RG_HEAD_DIM
    start = min(lo // GATE_SHIFT * GATE_SHIFT, RG_W - GATE_WIN)
    assert start <= lo and hi <= start + GATE_WIN
    return start


def _dot(a, b):
    return jnp.dot(a, b, preferred_element_type=F32)


def _half_silu_from_half(hz):
    return hz + hz * jnp.tanh(hz)


def _group_copies(hbm_ref, buf_ref, sem_ref, group, slot, to_hbm):
    copies = []
    for b in range(BATCH):
        hbm = hbm_ref.at[b, pl.ds(group * GROUP_T, GROUP_T), :]
        vmem = buf_ref.at[slot, :, b, :]
        src, dst = (vmem, hbm) if to_hbm else (hbm, vmem)
        copies.append(pltpu.make_async_copy(src, dst, sem_ref.at[slot]))
    return copies


def _block_kernel(x_hbm, c_ref, bada_ref, gnorm_ref, gfinal_ref, sccw_ref, sccb_ref,
                  bm_ref, rgwa_ref, rgwx_ref, spread_ref, samehead_ref,
                  rgcw_hbm, rgcb_hbm, gba_hbm, gbx_hbm, lam_hbm,
                  wada_hbm, win_hbm, scw_hbm, rgw_hbm, ow_hbm,
                  out_hbm,
                  rgp, rgp_sem, mod_scr, wg_ref, win_ref, wproj_ref, stage_in, stage_proj, win_sem, w_sem,
                  xbuf, obuf, in_sem, out_sem,
                  h_tb, z0_scr, ubuf, s_scr, vbuf, hvf, vbf, vbs, a_scr, b_scr, yrg_scr, s2_scr, hstate):
    step = pl.program_id(0)
    xg_cur = lax.rem(step, XBUFS)
    xg_next = lax.rem(step + 1, XBUFS)
    og_cur = lax.rem(step, OBUFS)

    def in_copies(group):
        if isinstance(group, int):
            return _group_copies(x_hbm, xbuf, in_sem, min(group, NSTEPS - 1), group % XBUFS, to_hbm=False)
        return _group_copies(x_hbm, xbuf, in_sem, jnp.minimum(group, NSTEPS - 1),
                             lax.rem(group, XBUFS), to_hbm=False)

    def out_copies(group):
        return _group_copies(out_hbm, obuf, out_sem, group, lax.rem(group, OBUFS), to_hbm=True)

    def norm_in(x_slot, c, h_ref):
        x3 = xbuf[x_slot, c * TS:(c + 1) * TS]
        ms = jnp.mean(x3 * x3, axis=-1, keepdims=True)
        shift = mod_scr[:, 0:D_MODEL]
        scale1 = 1.0 + mod_scr[:, D_MODEL:2 * D_MODEL]
        hn = (x3 * lax.rsqrt(ms + EPS) * gnorm_ref[...]) * scale1 + shift
        hb = hn.reshape(ROWS, D_MODEL).astype(BF16)
        for kb in range(D_MODEL // CB):
            h_ref[kb] = hb[:, kb * CB:(kb + 1) * CB]

    def proj(h_ref, col0, width):
        h = jnp.concatenate([h_ref[kb] for kb in range(D_MODEL // CB)], axis=1)
        outs = [_dot(h, win_ref[n]) for n in range(col0 // CB, (col0 + width) // CB)]
        return outs[0] if len(outs) == 1 else jnp.concatenate(outs, axis=1)

    def out_proj(lhs, row0, nrows):
        return jnp.concatenate([_dot(lhs, wproj_ref[nb, row0:row0 + nrows, :])
                                for nb in range(D_MODEL // CB)], axis=1)

    def mixer_a_dots(h_ref, j):
        return tuple(proj(h_ref, o + j * CB, CB) for o in (O_SCB, O_SCC, O_SCV, O_SCG))

    def block0_dots_to_scratch(h_ref):
        for n, zn in enumerate(mixer_a_dots(h_ref, 0)):
            z0_scr[n] = zn

    def load_weights():
        n_win = D_MODEL // WIN_CHUNK

        def win_copy(i):
            slot = i % WIN_SLOTS if isinstance(i, int) else lax.rem(i, WIN_SLOTS)
            return pltpu.make_async_copy(win_hbm.at[pl.ds(i * WIN_CHUNK, WIN_CHUNK), :],
                                         stage_in.at[slot], win_sem.at[slot])

        for i in range(WIN_SLOTS - 1):
            win_copy(i).start()

        def win_body(i, carry):
            slot = lax.rem(i, WIN_SLOTS)

            @pl.when(i + WIN_SLOTS - 1 < n_win)
            def _():
                win_copy(i + WIN_SLOTS - 1).start()

            win_copy(i).wait()
            rows = pl.ds(pl.multiple_of(i * WIN_CHUNK, WIN_CHUNK), WIN_CHUNK)
            for c0, c1, scale in WIN_COL_SCALES:
                for n in range(c0 // CB, c1 // CB):
                    v = stage_in[slot, :, n * CB:(n + 1) * CB]
                    win_ref[n, rows, :] = (v if scale == 1.0 else scale * v).astype(BF16)
            return carry

        chunks = []
        for src, dst0, nrows, scale in ((scw_hbm, WP_SC, D_MODEL, 0.5), (rgw_hbm, WP_RG, RG_W, 0.5),
                                        (ow_hbm, WP_OUT, D_MODEL, 1.0)):
            chunks += [(src, r, dst0 + r, scale) for r in range(0, nrows, WP_CHUNK)]

        def proj_copy(n):
            src, r, _, _ = chunks[n]
            return pltpu.make_async_copy(src.at[r:r + WP_CHUNK, :], stage_proj.at[n % WP_SLOTS],
                                         w_sem.at[n % WP_SLOTS])

        def ada_copy(i):
            return pltpu.make_async_copy(wada_hbm.at[i * WIN_CHUNK:(i + 1) * WIN_CHUNK, :],
                                         stage_in.at[i % WIN_SLOTS, :, 0:3 * D_MODEL],
                                         win_sem.at[i % WIN_SLOTS])

        for n in range(WP_SLOTS - 1):
            proj_copy(n).start()
        lax.fori_loop(0, n_win, win_body, 0)
        for i in range(WIN_SLOTS - 1):
            ada_copy(i).start()

        for n, (_, _, dst, scale) in enumerate(chunks):
            if n + WP_SLOTS - 1 < len(chunks):
                proj_copy(n + WP_SLOTS - 1).start()
            proj_copy(n).wait()
            v = stage_proj[n % WP_SLOTS]
            vb = (v if scale == 1.0 else scale * v).astype(BF16)
            for nb in range(D_MODEL // CB):
                wproj_ref[nb, dst:dst + WP_CHUNK, :] = vb[:, nb * CB:(nb + 1) * CB]

        c = c_ref[...]
        c_act = (c * jax.nn.sigmoid(c)).astype(BF16)
        mod = jnp.broadcast_to(bada_ref[...], (BATCH, 3 * D_MODEL))
        for i in range(n_win):
            if i + WIN_SLOTS - 1 < n_win:
                ada_copy(i + WIN_SLOTS - 1).start()
            ada_copy(i).wait()
            w_chunk = stage_in[i % WIN_SLOTS, :, 0:3 * D_MODEL].astype(BF16)
            mod = mod + _dot(c_act[:, i * WIN_CHUNK:(i + 1) * WIN_CHUNK], w_chunk)
        mod_scr[...] = mod

        for cblk in range(NG):
            s0 = _gate_win_start(cblk)
            for half, w_ref in enumerate((rgwa_ref, rgwx_ref)):
                rows = w_ref[s0:s0 + GATE_WIN, :].astype(BF16)
                dense = _dot(rows, spread_ref[cblk, 0:RG_HEAD_DIM, :]) * samehead_ref[cblk].astype(F32)
                wg_ref[cblk, :, half * GB:(half + 1) * GB] = dense.astype(BF16)

    @pl.when(step == 0)
    def _():
        ubuf[:, 0:SC_HIST, :] = jnp.zeros((NA, SC_HIST, CB), F32)
        vbuf[0:RG_HIST, :] = jnp.zeros((RG_HIST, RG_W), F32)
        hstate[...] = jnp.zeros_like(hstate)
        for g in range(2):
            for cp in in_copies(g):
                cp.start()
        small = [pltpu.make_async_copy(src, rgp.at[r0:r0 + src.shape[0], 0:RG_W], rgp_sem.at[0])
                 for src, r0 in ((rgcw_hbm, RGP_CW), (rgcb_hbm, RGP_CB), (gba_hbm, RGP_BA), (gbx_hbm, RGP_BX),
                                 (lam_hbm, RGP_LAM))]
        for cp in small:
            cp.start()
        load_weights()
        for cp in small:
            cp.wait()
        for cp in in_copies(0):
            cp.wait()
        norm_in(0, 0, h_tb.at[0])
        block0_dots_to_scratch(h_tb.at[0])

    @pl.when(step >= OBUFS)
    def _():
        for cp in out_copies(step - OBUFS):
            cp.wait()

    for cp in in_copies(step + 2):
        cp.start()
    for cp in in_copies(step + 1):
        cp.wait()

    def mixer_a_block(j, z):
        zb, zc, zv, hzg = z
        u = zc * zv
        ubuf[j, SC_HIST:SC_HIST + ROWS, :] = u
        cw = sccw_ref[:, j * CB:(j + 1) * CB]
        conv = sccb_ref[:, j * CB:(j + 1) * CB] + ubuf[j, 0:ROWS, :] * cw[0:1, :]
        conv = conv + ubuf[j, BATCH:BATCH + ROWS, :] * cw[1:2, :]
        conv = conv + u * cw[2:3, :]
        ubuf[j, 0:SC_HIST, :] = u[ROWS - SC_HIST:ROWS, :]
        s_scr[j] = (zb * conv * _half_silu_from_half(hzg)).astype(BF16)

    def gate_dot(c):
        start = _gate_win_start(c)
        slabs = vbf if start % LANES == 0 else vbs
        s0 = start // LANES
        lhs = jnp.concatenate([slabs[s0 + i] for i in range(GATE_WIN // LANES)], axis=1)
        return _dot(lhs, wg_ref[c])

    def gate_dots(h_cur, k):
        pieces = [gate_dot(c) for c in range(k * CB // GB, (k + 1) * CB // GB)]
        hg_a = jnp.concatenate([p[:, 0:GB] for p in pieces], axis=1)
        hg_x = jnp.concatenate([p[:, GB:2 * GB] for p in pieces], axis=1)
        return (hg_a, hg_x), proj(h_cur, O_RGG + k * CB, CB)

    def mixer_b_block(k, hg, hzg):
        hg_a, hg_x = hg
        cols = slice(k * CB, (k + 1) * CB)
        t_r = jnp.tanh(hg_a + 0.5 * rgp[RGP_BA:RGP_BA + 1, cols])
        t_i = jnp.tanh(hg_x + 0.5 * rgp[RGP_BX:RGP_BX + 1, cols])
        half_k = (-0.5 * LRU_C) * jax.nn.softplus(-rgp[RGP_LAM:RGP_LAM + 1, cols])
        log_a = half_k + half_k * t_r
        a = jnp.exp(log_a)
        a_scr[...] = a
        one_minus_a2 = jnp.tanh(log_a) * (-1.0 - a * a)
        root = jnp.where(one_minus_a2 > 0.0, one_minus_a2 * lax.rsqrt(one_minus_a2), 0.0)
        hvk = hvf[k]
        b_scr[...] = root * (hvk + hvk * t_i)
        h = hstate[k]
        for t in range(TS):
            rows = slice(t * BATCH, (t + 1) * BATCH)
            h = a_scr[rows, :] * h + b_scr[rows, :]
            yrg_scr[rows, :] = h
        hstate[k] = h
        s2_scr[k] = (yrg_scr[...] * _half_silu_from_half(hzg)).astype(BF16)

    def process_tile(c):
        h_cur = h_tb.at[c % 2]
        h_next = h_tb.at[(c + 1) % 2]

        z = tuple(z0_scr[n] for n in range(4))
        for j in range(NA):
            z_next = mixer_a_dots(h_cur, j + 1) if j + 1 < NA else None
            if j == 0:
                if c + 1 < TPS:
                    norm_in(xg_cur, c + 1, h_next)
                else:
                    norm_in(xg_next, 0, h_next)
            mixer_a_block(j, z)
            z = z_next

        vz = proj(h_cur, O_RGV, RG_W)
        m_pieces = [proj(h_cur, O_M, M_PIECE)]
        s_all = jnp.concatenate([s_scr[j] for j in range(NA)], axis=1)
        ya_half = out_proj(s_all, WP_SC, D_MODEL)
        vbuf[RG_HIST:RG_HIST + ROWS, :] = vz
        rcw = 0.5 * rgp[RGP_CW:RGP_CW + RG_K, 0:RG_W]
        hv = 0.5 * rgp[RGP_CB:RGP_CB + 1, 0:RG_W] + vbuf[0:ROWS, :] * rcw[0:1, :]
        hv = hv + vbuf[BATCH:BATCH + ROWS, :] * rcw[1:2, :]
        hv = hv + vbuf[2 * BATCH:2 * BATCH + ROWS, :] * rcw[2:3, :]
        hv = hv + vz * rcw[3:4, :]
        vbuf[0:RG_HIST, :] = vz[ROWS - RG_HIST:ROWS, :]
        for cidx in range(NG):
            vbf[cidx] = hv[:, cidx * LANES:(cidx + 1) * LANES].astype(BF16)
        for cidx in range(NG - 1):
            lo = GATE_SHIFT + cidx * LANES
            vbs[cidx] = hv[:, lo:lo + LANES].astype(BF16)
        for k in range(NB):
            hvf[k] = hv[:, k * CB:(k + 1) * CB]

        hg, hzg = gate_dots(h_cur, 0)
        for k in range(NB):
            hg_next, hzg_next = gate_dots(h_cur, k + 1) if k + 1 < NB else (None, None)
            if 1 <= k < NB - 1:
                m_pieces.append(proj(h_cur, O_M + k * M_PIECE, M_PIECE))
            mixer_b_block(k, hg, hzg)
            hg, hzg = hg_next, hzg_next
        m = jnp.concatenate(m_pieces, axis=1)

        s2_all = jnp.concatenate([s2_scr[k] for k in range(NB)], axis=1)
        yb_half = out_proj(s2_all, WP_RG, RG_W)

        t_a = jnp.tanh(m[:, 0:D_MODEL] + 0.5 * bm_ref[0:1, :])
        t_b = jnp.tanh(m[:, D_MODEL:2 * D_MODEL] + 0.5 * bm_ref[1:2, :])
        merged = ((ya_half + ya_half * t_a) + (yb_half + yb_half * t_b)).astype(BF16)
        y3 = out_proj(merged, WP_OUT, D_MODEL).reshape(TS, BATCH, D_MODEL)
        block0_dots_to_scratch(h_next)
        xo = xbuf[xg_cur, c * TS:(c + 1) * TS] + mod_scr[:, 2 * D_MODEL:3 * D_MODEL] * y3
        ms = jnp.mean(xo * xo, axis=-1, keepdims=True)
        obuf[og_cur, c * TS:(c + 1) * TS] = xo * lax.rsqrt(ms + EPS) * gfinal_ref[...]

    for c in range(TPS):
        process_tile(c)

    for cp in out_copies(step):
        cp.start()

    @pl.when(step == NSTEPS - 1)
    def _():
        for cp in in_copies(step + 2):
            cp.wait()
        for cp in out_copies(step - 1):
            cp.wait()
        for cp in out_copies(step):
            cp.wait()


def _const_spec(shape):
    nd = len(shape)
    return pl.BlockSpec(shape, lambda i, _nd=nd: (0,) * _nd, pipeline_mode=pl.Buffered(1))


def _gate_layout_constants():
    starts = np.array([_gate_win_start(c) for c in range(NG)])
    row_ch = starts[:, None] + np.arange(GATE_WIN)[None, :]
    col_ch = np.arange(NG)[:, None] * GB + np.arange(GB)[None, :]
    same_head = (row_ch // RG_HEAD_DIM)[:, :, None] == (col_ch // RG_HEAD_DIM)[:, None, :]
    spread = np.arange(LANES)[None, :, None] == (col_ch % RG_HEAD_DIM)[:, None, :]
    return jnp.asarray(spread, BF16), jnp.asarray(same_head, BF16)


def kernel(x, c, w_ada, b_ada, g_norm, w_in, sc_conv_w, sc_conv_b, sc_w_out, rg_conv_w, rg_conv_b,
           rg_w_a, rg_b_a, rg_w_x, rg_b_x, rg_lambda, rg_w_out, b_merge, w_out, g_final):
    assert x.shape == (BATCH, SEQ, D_MODEL) and w_ada.shape[0] == 1
    l = 0
    spread, same_head = _gate_layout_constants()

    vmem_operands = (
        c, b_ada[l].reshape(1, 3 * D_MODEL), g_norm[l].reshape(1, D_MODEL), g_final.reshape(1, D_MODEL),
        sc_conv_w[l], sc_conv_b[l].reshape(1, D_MODEL), b_merge[l],
        rg_w_a[l].reshape(RG_W, RG_HEAD_DIM), rg_w_x[l].reshape(RG_W, RG_HEAD_DIM), spread, same_head)
    hbm_weights = (rg_conv_w[l], rg_conv_b[l].reshape(1, RG_W), rg_b_a[l].reshape(1, RG_W),
                   rg_b_x[l].reshape(1, RG_W), rg_lambda[l].reshape(1, RG_W),
                   w_ada[l], w_in[l], sc_w_out[l], rg_w_out[l], w_out[l])
    operands = (x,) + vmem_operands + hbm_weights
    any_spec = pl.BlockSpec(memory_space=pl.ANY)
    in_specs = [any_spec] + [_const_spec(op.shape) for op in vmem_operands] + [any_spec] * len(hbm_weights)
    scratch = [
        pltpu.VMEM((8, 2 * D_MODEL), F32),
        pltpu.SemaphoreType.DMA((1,)),
        pltpu.VMEM((BATCH, 3 * D_MODEL), F32),
        pltpu.VMEM((NG, GATE_WIN, 2 * GB), BF16),
        pltpu.VMEM((P_IN // CB, D_MODEL, CB), BF16),
        pltpu.VMEM((D_MODEL // CB, WP_ROWS, CB), BF16),
        pltpu.VMEM((WIN_SLOTS, WIN_CHUNK, P_IN), F32),
        pltpu.VMEM((WP_SLOTS, WP_CHUNK, D_MODEL), F32),
        pltpu.SemaphoreType.DMA((WIN_SLOTS,)),
        pltpu.SemaphoreType.DMA((WP_SLOTS,)),
        pltpu.VMEM((XBUFS, GROUP_T, BATCH, D_MODEL), F32),
        pltpu.VMEM((OBUFS, GROUP_T, BATCH, D_MODEL), F32),
        pltpu.SemaphoreType.DMA((XBUFS,)),
        pltpu.SemaphoreType.DMA((OBUFS,)),
        pltpu.VMEM((2, D_MODEL // CB, ROWS, CB), BF16),
        pltpu.VMEM((4, ROWS, CB), F32),
        pltpu.VMEM((NA, SC_HIST + ROWS, CB), F32),
        pltpu.VMEM((NA, ROWS, CB), BF16),
        pltpu.VMEM((RG_HIST + ROWS, RG_W), F32),
        pltpu.VMEM((NB, ROWS, CB), F32),
        pltpu.VMEM((NG, ROWS, LANES), BF16),
        pltpu.VMEM((NG - 1, ROWS, LANES), BF16),
        pltpu.VMEM((ROWS, CB), F32),
        pltpu.VMEM((ROWS, CB), F32),
        pltpu.VMEM((ROWS, CB), F32),
        pltpu.VMEM((NB, ROWS, CB), BF16),
        pltpu.VMEM((NB, BATCH, CB), F32),
    ]
    return pl.pallas_call(
        _block_kernel,
        grid=(NSTEPS,),
        in_specs=in_specs,
        out_specs=any_spec,
        out_shape=jax.ShapeDtypeStruct((BATCH, SEQ, D_MODEL), F32),
        scratch_shapes=scratch,
        compiler_params=pltpu.CompilerParams(
            dimension_semantics=("arbitrary",), vmem_limit_bytes=VMEM_LIMIT_BYTES),
        name="hybrid_block",
    )(*operands)
```

```python
import jax
import jax.numpy as jnp
import numpy as np
from jax import lax
from jax.experimental import pallas as pl
from jax.experimental.pallas import tpu as pltpu

D_MODEL = 1024
BATCH = 16
SEQ = 2048
SC_K = 3
RG_W = 1280
RG_HEADS = 16
RG_HEAD_DIM = RG_W // RG_HEADS
RG_K = 4
LRU_C = 8.0
EPS = 1e-6

LANES = 128
MXU_DIM = 256
V7X_VMEM_BYTES = 64 * 1024 * 1024
TS = 16
TPS = 2
NTILES = SEQ // TS
NSTEPS = NTILES // TPS
assert TPS % 2 == 0
ROWS = TS * BATCH
GROUP_T = TPS * TS
XBUFS = 3
OBUFS = 2
CB = MXU_DIM
NA = D_MODEL // CB
NB = RG_W // CB
GB = LANES
NG = RG_W // GB
GATE_WIN = MXU_DIM
GATE_SHIFT = LANES // 2
SC_HIST = (SC_K - 1) * BATCH
RG_HIST = (RG_K - 1) * BATCH
VMEM_LIMIT_BYTES = V7X_VMEM_BYTES - 4 * 1024 * 1024

O_SCB, O_SCC, O_SCV, O_SCG = 0, D_MODEL, 2 * D_MODEL, 3 * D_MODEL
O_RGV = 4 * D_MODEL
O_RGG = O_RGV + RG_W
O_M = O_RGG + RG_W
P_IN = O_M + 2 * D_MODEL
M_PIECE = 2 * D_MODEL // (NB - 1)

WP_SC = 0
WP_RG = WP_SC + D_MODEL
WP_OUT = WP_RG + RG_W
WP_ROWS = WP_OUT + D_MODEL

PITCH_PAD = LANES

RGP_CW, RGP_CB, RGP_BA, RGP_BX, RGP_LAM = 0, RG_K, RG_K + 1, RG_K + 2, RG_K + 3
WIN_CHUNK = 64
WIN_SLOTS = 3
WP_CHUNK = 256
WP_SLOTS = 2
WIN_COL_SCALES = ((0, O_SCG, 1.0), (O_SCG, O_RGV, 0.5), (O_RGV, O_RGG, 1.0), (O_RGG, P_IN, 0.5))

F32 = jnp.float32
BF16 = jnp.bfloat16


def _gate_win_start(c):
    lo = (c * GB) // RG_HEAD_DIM * RG_HEAD_DIM
    hi = -(-(c * GB + GB) // RG_HEAD_DIM) * RG_HEAD_DIM
    start = min(lo // GATE_SHIFT * GATE_SHIFT, RG_W - GATE_WIN)
    assert start <= lo and hi <= start + GATE_WIN
    return start


def _dot(a, b):
    return jnp.dot(a, b, preferred_element_type=F32)


def _half_silu_from_half(hz):
    return hz + hz * jnp.tanh(hz)


def _group_copies(hbm_ref, buf_ref, sem_ref, group, slot, to_hbm):
    copies = []
    for b in range(BATCH):
        hbm = hbm_ref.at[b, pl.ds(group * GROUP_T, GROUP_T), :]
        vmem = buf_ref.at[slot, :, b, :]
        src, dst = (vmem, hbm) if to_hbm else (hbm, vmem)
        copies.append(pltpu.make_async_copy(src, dst, sem_ref.at[slot]))
    return copies


def _block_kernel(x_hbm, c_ref, bada_ref, gnorm_ref, gfinal_ref, sccw_ref, sccb_ref,
                  bm_ref, rgwa_ref, rgwx_ref, spread_ref, samehead_ref,
                  rgcw_hbm, rgcb_hbm, gba_hbm, gbx_hbm, lam_hbm,
                  wada_hbm, win_hbm, scw_hbm, rgw_hbm, ow_hbm,
                  out_hbm,
                  rgp, rgp_sem, mod_scr, wg_ref, win_ref, wproj_ref, stage_in, stage_proj, win_sem, w_sem,
                  xbuf, obuf, in_sem, out_sem,
                  h_tb, z0_scr, ubuf, s_scr, vbuf, hvf, vbf, vbs, a_scr, b_scr, yrg_scr, s2_scr, hstate):
    step = pl.program_id(0)
    xg_cur = lax.rem(step, XBUFS)
    xg_next = lax.rem(step + 1, XBUFS)
    og_cur = lax.rem(step, OBUFS)

    def in_copies(group):
        if isinstance(group, int):
            return _group_copies(x_hbm, xbuf, in_sem, min(group, NSTEPS - 1), group % XBUFS, to_hbm=False)
        return _group_copies(x_hbm, xbuf, in_sem, jnp.minimum(group, NSTEPS - 1),
                             lax.rem(group, XBUFS), to_hbm=False)

    def out_copies(group):
        return _group_copies(out_hbm, obuf, out_sem, group, lax.rem(group, OBUFS), to_hbm=True)

    def norm_in(x_slot, c, h_ref):
        x3 = xbuf[x_slot, c * TS:(c + 1) * TS]
        ms = jnp.mean(x3 * x3, axis=-1, keepdims=True)
        shift = mod_scr[:, 0:D_MODEL]
        scale1 = 1.0 + mod_scr[:, D_MODEL:2 * D_MODEL]
        hn = (x3 * lax.rsqrt(ms + EPS) * gnorm_ref[...]) * scale1 + shift
        hb = hn.reshape(ROWS, D_MODEL).astype(BF16)
        for kb in range(D_MODEL // CB):
            h_ref[kb] = hb[:, kb * CB:(kb + 1) * CB]

    def proj(h_ref, col0, width):
        h = jnp.concatenate([h_ref[kb] for kb in range(D_MODEL // CB)], axis=1)
        outs = [_dot(h, win_ref[n]) for n in range(col0 // CB, (col0 + width) // CB)]
        return outs[0] if len(outs) == 1 else jnp.concatenate(outs, axis=1)

    def mixer_a_dots(h_ref, j):
        return tuple(proj(h_ref, o + j * CB, CB) for o in (O_SCB, O_SCC, O_SCV, O_SCG))

    def block0_dots_to_scratch(h_ref):
        for n, zn in enumerate(mixer_a_dots(h_ref, 0)):
            z0_scr[n] = zn

    def load_weights():
        n_win = D_MODEL // WIN_CHUNK

        def win_copy(i):
            slot = i % WIN_SLOTS if isinstance(i, int) else lax.rem(i, WIN_SLOTS)
            return pltpu.make_async_copy(win_hbm.at[pl.ds(i * WIN_CHUNK, WIN_CHUNK), :],
                                         stage_in.at[slot], win_sem.at[slot])

        for i in range(WIN_SLOTS - 1):
            win_copy(i).start()

        def win_body(i, carry):
            slot = lax.rem(i, WIN_SLOTS)

            @pl.when(i + WIN_SLOTS - 1 < n_win)
            def _():
                win_copy(i + WIN_SLOTS - 1).start()

            win_copy(i).wait()
            rows = pl.ds(pl.multiple_of(i * WIN_CHUNK, WIN_CHUNK), WIN_CHUNK)
            for c0, c1, scale in WIN_COL_SCALES:
                for n in range(c0 // CB, c1 // CB):
                    v = stage_in[slot, :, n * CB:(n + 1) * CB]
                    win_ref[n, rows, :] = (v if scale == 1.0 else scale * v).astype(BF16)
            return carry

        chunks = []
        for src, dst0, nrows, scale in ((scw_hbm, WP_SC, D_MODEL, 0.5), (rgw_hbm, WP_RG, RG_W, 0.5),
                                        (ow_hbm, WP_OUT, D_MODEL, 1.0)):
            chunks += [(src, r, dst0 + r, scale) for r in range(0, nrows, WP_CHUNK)]

        def proj_copy(n):
            src, r, _, _ = chunks[n]
            return pltpu.make_async_copy(src.at[r:r + WP_CHUNK, :], stage_proj.at[n % WP_SLOTS],
                                         w_sem.at[n % WP_SLOTS])

        def ada_copy(i):
            return pltpu.make_async_copy(wada_hbm.at[i * WIN_CHUNK:(i + 1) * WIN_CHUNK, :],
                                         stage_in.at[i % WIN_SLOTS, :, 0:3 * D_MODEL],
                                         win_sem.at[i % WIN_SLOTS])

        for n in range(WP_SLOTS - 1):
            proj_copy(n).start()
        lax.fori_loop(0, n_win, win_body, 0)
        for i in range(WIN_SLOTS - 1):
            ada_copy(i).start()

        for n, (_, _, dst, scale) in enumerate(chunks):
            if n + WP_SLOTS - 1 < len(chunks):
                proj_copy(n + WP_SLOTS - 1).start()
            proj_copy(n).wait()
            v = stage_proj[n % WP_SLOTS]
            wproj_ref[dst:dst + WP_CHUNK, 0:D_MODEL] = (v if scale == 1.0 else scale * v).astype(BF16)

        c = c_ref[...]
        c_act = (c * jax.nn.sigmoid(c)).astype(BF16)
        mod = jnp.broadcast_to(bada_ref[...], (BATCH, 3 * D_MODEL))
        for i in range(n_win):
            if i + WIN_SLOTS - 1 < n_win:
                ada_copy(i + WIN_SLOTS - 1).start()
            ada_copy(i).wait()
            w_chunk = stage_in[i % WIN_SLOTS, :, 0:3 * D_MODEL].astype(BF16)
            mod = mod + _dot(c_act[:, i * WIN_CHUNK:(i + 1) * WIN_CHUNK], w_chunk)
        mod_scr[...] = mod

        for cblk in range(NG):
            s0 = _gate_win_start(cblk)
            for half, w_ref in enumerate((rgwa_ref, rgwx_ref)):
                rows = w_ref[s0:s0 + GATE_WIN, :].astype(BF16)
                dense = _dot(rows, spread_ref[cblk, 0:RG_HEAD_DIM, :]) * samehead_ref[cblk].astype(F32)
                wg_ref[cblk, :, half * GB:(half + 1) * GB] = dense.astype(BF16)

    @pl.when(step == 0)
    def _():
        ubuf[:, 0:SC_HIST, :] = jnp.zeros((NA, SC_HIST, CB), F32)
        vbuf[0:RG_HIST, :] = jnp.zeros((RG_HIST, RG_W), F32)
        hstate[...] = jnp.zeros_like(hstate)
        for g in range(2):
            for cp in in_copies(g):
                cp.start()
        small = [pltpu.make_async_copy(src, rgp.at[r0:r0 + src.shape[0], 0:RG_W], rgp_sem.at[0])
                 for src, r0 in ((rgcw_hbm, RGP_CW), (rgcb_hbm, RGP_CB), (gba_hbm, RGP_BA), (gbx_hbm, RGP_BX),
                                 (lam_hbm, RGP_LAM))]
        for cp in small:
            cp.start()
        load_weights()
        for cp in small:
            cp.wait()
        for cp in in_copies(0):
            cp.wait()
        norm_in(0, 0, h_tb.at[0])
        block0_dots_to_scratch(h_tb.at[0])

    @pl.when(step >= OBUFS)
    def _():
        for cp in out_copies(step - OBUFS):
            cp.wait()

    for cp in in_copies(step + 1):
        cp.wait()

    def mixer_a_block(j, z):
        zb, zc, zv, hzg = z
        u = zc * zv
        ubuf[j, SC_HIST:SC_HIST + ROWS, :] = u
        cw = sccw_ref[:, j * CB:(j + 1) * CB]
        conv = sccb_ref[:, j * CB:(j + 1) * CB] + ubuf[j, 0:ROWS, :] * cw[0:1, :]
        conv = conv + ubuf[j, BATCH:BATCH + ROWS, :] * cw[1:2, :]
        conv = conv + u * cw[2:3, :]
        ubuf[j, 0:SC_HIST, :] = u[ROWS - SC_HIST:ROWS, :]
        s_scr[j] = (zb * conv * _half_silu_from_half(hzg)).astype(BF16)

    def gate_dot(c):
        start = _gate_win_start(c)
        slabs = vbf if start % LANES == 0 else vbs
        s0 = start // LANES
        lhs = jnp.concatenate([slabs[s0 + i] for i in range(GATE_WIN // LANES)], axis=1)
        return _dot(lhs, wg_ref[c])

    def gate_dots(h_cur, k):
        pieces = [gate_dot(c) for c in range(k * CB // GB, (k + 1) * CB // GB)]
        hg_a = jnp.concatenate([p[:, 0:GB] for p in pieces], axis=1)
        hg_x = jnp.concatenate([p[:, GB:2 * GB] for p in pieces], axis=1)
        return (hg_a, hg_x), proj(h_cur, O_RGG + k * CB, CB)

    def mixer_b_block(k, hg, hzg):
        hg_a, hg_x = hg
        cols = slice(k * CB, (k + 1) * CB)
        t_r = jnp.tanh(hg_a + 0.5 * rgp[RGP_BA:RGP_BA + 1, cols])
        t_i = jnp.tanh(hg_x + 0.5 * rgp[RGP_BX:RGP_BX + 1, cols])
        half_k = (-0.5 * LRU_C) * jax.nn.softplus(-rgp[RGP_LAM:RGP_LAM + 1, cols])
        log_a = half_k + half_k * t_r
        a = jnp.exp(log_a)
        a_scr[...] = a
        one_minus_a2 = jnp.tanh(log_a) * (-1.0 - a * a)
        root = jnp.where(one_minus_a2 > 0.0, one_minus_a2 * lax.rsqrt(one_minus_a2), 0.0)
        hvk = hvf[k]
        b_scr[...] = root * (hvk + hvk * t_i)
        h = hstate[k]
        for t in range(TS):
            rows = slice(t * BATCH, (t + 1) * BATCH)
            h = a_scr[rows, :] * h + b_scr[rows, :]
            yrg_scr[rows, :] = h
        hstate[k] = h
        s2_scr[k] = (yrg_scr[...] * _half_silu_from_half(hzg)).astype(BF16)

    def process_tile(c):
        h_cur = h_tb.at[c % 2]
        h_next = h_tb.at[(c + 1) % 2]

        z = tuple(z0_scr[n] for n in range(4))
        for j in range(NA):
            z_next = mixer_a_dots(h_cur, j + 1) if j + 1 < NA else None
            if j == 0:
                if c + 1 < TPS:
                    norm_in(xg_cur, c + 1, h_next)
                else:
                    norm_in(xg_next, 0, h_next)
            mixer_a_block(j, z)
            z = z_next

        vz = proj(h_cur, O_RGV, RG_W)
        m_pieces = [proj(h_cur, O_M, M_PIECE)]
        s_all = jnp.concatenate([s_scr[j] for j in range(NA)], axis=1)
        ya_half = _dot(s_all, wproj_ref[WP_SC:WP_SC + D_MODEL, 0:D_MODEL])
        vbuf[RG_HIST:RG_HIST + ROWS, :] = vz
        rcw = 0.5 * rgp[RGP_CW:RGP_CW + RG_K, 0:RG_W]
        hv = 0.5 * rgp[RGP_CB:RGP_CB + 1, 0:RG_W] + vbuf[0:ROWS, :] * rcw[0:1, :]
        hv = hv + vbuf[BATCH:BATCH + ROWS, :] * rcw[1:2, :]
        hv = hv + vbuf[2 * BATCH:2 * BATCH + ROWS, :] * rcw[2:3, :]
        hv = hv + vz * rcw[3:4, :]
        vbuf[0:RG_HIST, :] = vz[ROWS - RG_HIST:ROWS, :]
        for cidx in range(NG):
            vbf[cidx] = hv[:, cidx * LANES:(cidx + 1) * LANES].astype(BF16)
        for cidx in range(NG - 1):
            lo = GATE_SHIFT + cidx * LANES
            vbs[cidx] = hv[:, lo:lo + LANES].astype(BF16)
        for k in range(NB):
            hvf[k] = hv[:, k * CB:(k + 1) * CB]

        hg, hzg = gate_dots(h_cur, 0)
        for k in range(NB):
            hg_next, hzg_next = gate_dots(h_cur, k + 1) if k + 1 < NB else (None, None)
            if 1 <= k < NB - 1:
                m_pieces.append(proj(h_cur, O_M + k * M_PIECE, M_PIECE))
            mixer_b_block(k, hg, hzg)
            hg, hzg = hg_next, hzg_next
        m = jnp.concatenate(m_pieces, axis=1)

        s2_all = jnp.concatenate([s2_scr[k] for k in range(NB)], axis=1)
        yb_half = _dot(s2_all, wproj_ref[WP_RG:WP_RG + RG_W, 0:D_MODEL])

        t_a = jnp.tanh(m[:, 0:D_MODEL] + 0.5 * bm_ref[0:1, :])
        t_b = jnp.tanh(m[:, D_MODEL:2 * D_MODEL] + 0.5 * bm_ref[1:2, :])
        merged = ((ya_half + ya_half * t_a) + (yb_half + yb_half * t_b)).astype(BF16)
        y3 = _dot(merged, wproj_ref[WP_OUT:WP_OUT + D_MODEL, 0:D_MODEL]).reshape(TS, BATCH, D_MODEL)
        if c == TPS - 1:
            for cp in in_copies(step + 2):
                cp.start()
        block0_dots_to_scratch(h_next)
        xo = xbuf[xg_cur, c * TS:(c + 1) * TS] + mod_scr[:, 2 * D_MODEL:3 * D_MODEL] * y3
        ms = jnp.mean(xo * xo, axis=-1, keepdims=True)
        obuf[og_cur, c * TS:(c + 1) * TS] = xo * lax.rsqrt(ms + EPS) * gfinal_ref[...]

    for c in range(TPS):
        process_tile(c)

    for cp in out_copies(step):
        cp.start()

    @pl.when(step == NSTEPS - 1)
    def _():
        for cp in in_copies(step + 2):
            cp.wait()
        for cp in out_copies(step - 1):
            cp.wait()
        for cp in out_copies(step):
            cp.wait()


def _const_spec(shape):
    nd = len(shape)
    return pl.BlockSpec(shape, lambda i, _nd=nd: (0,) * _nd, pipeline_mode=pl.Buffered(1))


def _gate_layout_constants():
    starts = np.array([_gate_win_start(c) for c in range(NG)])
    row_ch = starts[:, None] + np.arange(GATE_WIN)[None, :]
    col_ch = np.arange(NG)[:, None] * GB + np.arange(GB)[None, :]
    same_head = (row_ch // RG_HEAD_DIM)[:, :, None] == (col_ch // RG_HEAD_DIM)[:, None, :]
    spread = np.arange(LANES)[None, :, None] == (col_ch % RG_HEAD_DIM)[:, None, :]
    return jnp.asarray(spread, BF16), jnp.asarray(same_head, BF16)


def kernel(x, c, w_ada, b_ada, g_norm, w_in, sc_conv_w, sc_conv_b, sc_w_out, rg_conv_w, rg_conv_b,
           rg_w_a, rg_b_a, rg_w_x, rg_b_x, rg_lambda, rg_w_out, b_merge, w_out, g_final):
    assert x.shape == (BATCH, SEQ, D_MODEL) and w_ada.shape[0] == 1
    l = 0
    spread, same_head = _gate_layout_constants()

    vmem_operands = (
        c, b_ada[l].reshape(1, 3 * D_MODEL), g_norm[l].reshape(1, D_MODEL), g_final.reshape(1, D_MODEL),
        sc_conv_w[l], sc_conv_b[l].reshape(1, D_MODEL), b_merge[l],
        rg_w_a[l].reshape(RG_W, RG_HEAD_DIM), rg_w_x[l].reshape(RG_W, RG_HEAD_DIM), spread, same_head)
    hbm_weights = (rg_conv_w[l], rg_conv_b[l].reshape(1, RG_W), rg_b_a[l].reshape(1, RG_W),
                   rg_b_x[l].reshape(1, RG_W), rg_lambda[l].reshape(1, RG_W),
                   w_ada[l], w_in[l], sc_w_out[l], rg_w_out[l], w_out[l])
    operands = (x,) + vmem_operands + hbm_weights
    any_spec = pl.BlockSpec(memory_space=pl.ANY)
    in_specs = [any_spec] + [_const_spec(op.shape) for op in vmem_operands] + [any_spec] * len(hbm_weights)
    scratch = [
        pltpu.VMEM((8, 2 * D_MODEL), F32),
        pltpu.SemaphoreType.DMA((1,)),
        pltpu.VMEM((BATCH, 3 * D_MODEL), F32),
        pltpu.VMEM((NG, GATE_WIN, 2 * GB), BF16),
        pltpu.VMEM((P_IN // CB, D_MODEL, CB), BF16),
        pltpu.VMEM((WP_ROWS, D_MODEL + PITCH_PAD), BF16),
        pltpu.VMEM((WIN_SLOTS, WIN_CHUNK, P_IN), F32),
        pltpu.VMEM((WP_SLOTS, WP_CHUNK, D_MODEL), F32),
        pltpu.SemaphoreType.DMA((WIN_SLOTS,)),
        pltpu.SemaphoreType.DMA((WP_SLOTS,)),
        pltpu.VMEM((XBUFS, GROUP_T, BATCH, D_MODEL), F32),
        pltpu.VMEM((OBUFS, GROUP_T, BATCH, D_MODEL), F32),
        pltpu.SemaphoreType.DMA((XBUFS,)),
        pltpu.SemaphoreType.DMA((OBUFS,)),
        pltpu.VMEM((2, D_MODEL // CB, ROWS, CB), BF16),
        pltpu.VMEM((4, ROWS, CB), F32),
        pltpu.VMEM((NA, SC_HIST + ROWS, CB), F32),
        pltpu.VMEM((NA, ROWS, CB), BF16),
        pltpu.VMEM((RG_HIST + ROWS, RG_W), F32),
        pltpu.VMEM((NB, ROWS, CB), F32),
        pltpu.VMEM((NG, ROWS, LANES), BF16),
        pltpu.VMEM((NG - 1, ROWS, LANES), BF16),
        pltpu.VMEM((ROWS, CB), F32),
        pltpu.VMEM((ROWS, CB), F32),
        pltpu.VMEM((ROWS, CB), F32),
        pltpu.VMEM((NB, ROWS, CB), BF16),
        pltpu.VMEM((NB, BATCH, CB), F32),
    ]
    return pl.pallas_call(
        _block_kernel,
        grid=(NSTEPS,),
        in_specs=in_specs,
        out_specs=any_spec,
        out_shape=jax.ShapeDtypeStruct((BATCH, SEQ, D_MODEL), F32),
        scratch_shapes=scratch,
        compiler_params=pltpu.CompilerParams(
            dimension_semantics=("arbitrary",), vmem_limit_bytes=VMEM_LIMIT_BYTES),
        name="hybrid_block",
    )(*operands)
```

```python
import jax
import jax.numpy as jnp
import numpy as np
from jax import lax
from jax.experimental import pallas as pl
from jax.experimental.pallas import tpu as pltpu

D_MODEL = 1024
BATCH = 16
SEQ = 2048
SC_K = 3
RG_W = 1280
RG_HEADS = 16
RG_HEAD_DIM = RG_W // RG_HEADS
RG_K = 4
LRU_C = 8.0
EPS = 1e-6

LANES = 128
MXU_DIM = 256
V7X_VMEM_BYTES = 64 * 1024 * 1024
TS = 16
TPS = 2
NTILES = SEQ // TS
NSTEPS = NTILES // TPS
assert TPS % 2 == 0
ROWS = TS * BATCH
GROUP_T = TPS * TS
XBUFS = 3
OBUFS = 2
CB = MXU_DIM
NA = D_MODEL // CB
NB = RG_W // CB
GB = LANES
NG = RG_W // GB
GATE_WIN = MXU_DIM
GATE_SHIFT = LANES // 2
SC_HIST = (SC_K - 1) * BATCH
RG_HIST = (RG_K - 1) * BATCH
VMEM_LIMIT_BYTES = V7X_VMEM_BYTES - 4 * 1024 * 1024

O_SCB, O_SCC, O_SCV, O_SCG = 0, D_MODEL, 2 * D_MODEL, 3 * D_MODEL
O_RGV = 4 * D_MODEL
O_RGG = O_RGV + RG_W
O_M = O_RGG + RG_W
P_IN = O_M + 2 * D_MODEL
M_PIECE = 2 * D_MODEL // (NB - 1)

WP_SC = 0
WP_RG = WP_SC + D_MODEL
WP_OUT = WP_RG + RG_W
WP_ROWS = WP_OUT + D_MODEL

PITCH_PAD = LANES

RGP_CW, RGP_CB, RGP_BA, RGP_BX, RGP_LAM = 0, RG_K, RG_K + 1, RG_K + 2, RG_K + 3
WIN_CHUNK = 64
WIN_SLOTS = 3
WP_CHUNK = 256
WP_SLOTS = 2
WIN_COL_SCALES = ((0, O_SCG, 1.0), (O_SCG, O_RGV, 0.5), (O_RGV, O_RGG, 1.0), (O_RGG, P_IN, 0.5))

F32 = jnp.float32
BF16 = jnp.bfloat16


def _gate_win_start(c):
    lo = (c * GB) // RG_HEAD_DIM * RG_HEAD_DIM
    hi = -(-(c * GB + GB) // RG_HEAD_DIM) * RG_HEAD_DIM
    start = min(lo // GATE_SHIFT * GATE_SHIFT, RG_W - GATE_WIN)
    assert start <= lo and hi <= start + GATE_WIN
    return start


def _dot(a, b):
    return jnp.dot(a, b, preferred_element_type=F32)


def _half_silu_from_half(hz):
    return hz + hz * jnp.tanh(hz)


def _group_copies(hbm_ref, buf_ref, sem_ref, group, slot, to_hbm):
    copies = []
    for b in range(BATCH):
        hbm = hbm_ref.at[b, pl.ds(group * GROUP_T, GROUP_T), :]
        vmem = buf_ref.at[slot, :, b, :]
        src, dst = (vmem, hbm) if to_hbm else (hbm, vmem)
        copies.append(pltpu.make_async_copy(src, dst, sem_ref.at[slot]))
    return copies


def _block_kernel(x_hbm, c_ref, bada_ref, gnorm_ref, gfinal_ref, sccw_ref, sccb_ref,
                  bm_ref, rgwa_ref, rgwx_ref, spread_ref, samehead_ref,
                  rgcw_hbm, rgcb_hbm, gba_hbm, gbx_hbm, lam_hbm,
                  wada_hbm, win_hbm, scw_hbm, rgw_hbm, ow_hbm,
                  out_hbm,
                  rgp, rgp_sem, mod_scr, wg_ref, win_ref, wproj_ref, stage_in, stage_proj, win_sem, w_sem,
                  xbuf, obuf, in_sem, out_sem,
                  h_tb, z0_scr, ubuf, s_scr, vbuf, hvf, vbf, vbs, a_scr, b_scr, yrg_scr, s2_scr, hstate):
    step = pl.program_id(0)
    xg_cur = lax.rem(step, XBUFS)
    xg_next = lax.rem(step + 1, XBUFS)
    og_cur = lax.rem(step, OBUFS)

    def in_copies(group):
        if isinstance(group, int):
            return _group_copies(x_hbm, xbuf, in_sem, min(group, NSTEPS - 1), group % XBUFS, to_hbm=False)
        return _group_copies(x_hbm, xbuf, in_sem, jnp.minimum(group, NSTEPS - 1),
                             lax.rem(group, XBUFS), to_hbm=False)

    def out_copies(group):
        return _group_copies(out_hbm, obuf, out_sem, group, lax.rem(group, OBUFS), to_hbm=True)

    def norm_in(x_slot, c, h_ref):
        x3 = xbuf[x_slot, c * TS:(c + 1) * TS]
        ms = jnp.mean(x3 * x3, axis=-1, keepdims=True)
        shift = mod_scr[:, 0:D_MODEL]
        scale1 = 1.0 + mod_scr[:, D_MODEL:2 * D_MODEL]
        hn = (x3 * lax.rsqrt(ms + EPS) * gnorm_ref[...]) * scale1 + shift
        hb = hn.reshape(ROWS, D_MODEL).astype(BF16)
        for kb in range(D_MODEL // CB):
            h_ref[kb] = hb[:, kb * CB:(kb + 1) * CB]

    def proj(h_ref, col0, width):
        h = jnp.concatenate([h_ref[kb] for kb in range(D_MODEL // CB)], axis=1)
        outs = [_dot(h, win_ref[n]) for n in range(col0 // CB, (col0 + width) // CB)]
        return outs[0] if len(outs) == 1 else jnp.concatenate(outs, axis=1)

    def mixer_a_dots(h_ref, j):
        return tuple(proj(h_ref, o + j * CB, CB) for o in (O_SCB, O_SCC, O_SCV, O_SCG))

    def block0_dots_to_scratch(h_ref):
        for n, zn in enumerate(mixer_a_dots(h_ref, 0)):
            z0_scr[n] = zn

    def load_weights():
        n_win = D_MODEL // WIN_CHUNK

        def win_copy(i):
            slot = i % WIN_SLOTS if isinstance(i, int) else lax.rem(i, WIN_SLOTS)
            return pltpu.make_async_copy(win_hbm.at[pl.ds(i * WIN_CHUNK, WIN_CHUNK), :],
                                         stage_in.at[slot], win_sem.at[slot])

        for i in range(WIN_SLOTS - 1):
            win_copy(i).start()

        def win_body(i, carry):
            slot = lax.rem(i, WIN_SLOTS)

            @pl.when(i + WIN_SLOTS - 1 < n_win)
            def _():
                win_copy(i + WIN_SLOTS - 1).start()

            win_copy(i).wait()
            rows = pl.ds(pl.multiple_of(i * WIN_CHUNK, WIN_CHUNK), WIN_CHUNK)
            for c0, c1, scale in WIN_COL_SCALES:
                for n in range(c0 // CB, c1 // CB):
                    v = stage_in[slot, :, n * CB:(n + 1) * CB]
                    win_ref[n, rows, :] = (v if scale == 1.0 else scale * v).astype(BF16)
            return carry

        chunks = []
        for src, dst0, nrows, scale in ((scw_hbm, WP_SC, D_MODEL, 0.5), (rgw_hbm, WP_RG, RG_W, 0.5),
                                        (ow_hbm, WP_OUT, D_MODEL, 1.0)):
            chunks += [(src, r, dst0 + r, scale) for r in range(0, nrows, WP_CHUNK)]

        def proj_copy(n):
            src, r, _, _ = chunks[n]
            return pltpu.make_async_copy(src.at[r:r + WP_CHUNK, :], stage_proj.at[n % WP_SLOTS],
                                         w_sem.at[n % WP_SLOTS])

        def ada_copy(i):
            return pltpu.make_async_copy(wada_hbm.at[i * WIN_CHUNK:(i + 1) * WIN_CHUNK, :],
                                         stage_in.at[i % WIN_SLOTS, :, 0:3 * D_MODEL],
                                         win_sem.at[i % WIN_SLOTS])

        for n in range(WP_SLOTS - 1):
            proj_copy(n).start()
        lax.fori_loop(0, n_win, win_body, 0)
        for i in range(WIN_SLOTS - 1):
            ada_copy(i).start()

        for n, (_, _, dst, scale) in enumerate(chunks):
            if n + WP_SLOTS - 1 < len(chunks):
                proj_copy(n + WP_SLOTS - 1).start()
            proj_copy(n).wait()
            v = stage_proj[n % WP_SLOTS]
            wproj_ref[dst:dst + WP_CHUNK, 0:D_MODEL] = (v if scale == 1.0 else scale * v).astype(BF16)

        c = c_ref[...]
        c_act = (c * jax.nn.sigmoid(c)).astype(BF16)
        mod = jnp.broadcast_to(bada_ref[...], (BATCH, 3 * D_MODEL))
        for i in range(n_win):
            if i + WIN_SLOTS - 1 < n_win:
                ada_copy(i + WIN_SLOTS - 1).start()
            ada_copy(i).wait()
            w_chunk = stage_in[i % WIN_SLOTS, :, 0:3 * D_MODEL].astype(BF16)
            mod = mod + _dot(c_act[:, i * WIN_CHUNK:(i + 1) * WIN_CHUNK], w_chunk)
        mod_scr[...] = mod

        for cblk in range(NG):
            s0 = _gate_win_start(cblk)
            for half, w_ref in enumerate((rgwa_ref, rgwx_ref)):
                rows = w_ref[s0:s0 + GATE_WIN, :].astype(BF16)
                dense = _dot(rows, spread_ref[cblk, 0:RG_HEAD_DIM, :]) * samehead_ref[cblk].astype(F32)
                wg_ref[cblk, :, half * GB:(half + 1) * GB] = dense.astype(BF16)

    @pl.when(step == 0)
    def _():
        ubuf[:, 0:SC_HIST, :] = jnp.zeros((NA, SC_HIST, CB), F32)
        vbuf[0:RG_HIST, :] = jnp.zeros((RG_HIST, RG_W), F32)
        hstate[...] = jnp.zeros_like(hstate)
        for g in range(2):
            for cp in in_copies(g):
                cp.start()
        small = [pltpu.make_async_copy(src, rgp.at[r0:r0 + src.shape[0], 0:RG_W], rgp_sem.at[0])
                 for src, r0 in ((rgcw_hbm, RGP_CW), (rgcb_hbm, RGP_CB), (gba_hbm, RGP_BA), (gbx_hbm, RGP_BX),
                                 (lam_hbm, RGP_LAM))]
        for cp in small:
            cp.start()
        load_weights()
        for cp in small:
            cp.wait()
        for cp in in_copies(0):
            cp.wait()
        norm_in(0, 0, h_tb.at[0])
        block0_dots_to_scratch(h_tb.at[0])

    @pl.when(step >= OBUFS)
    def _():
        for cp in out_copies(step - OBUFS):
            cp.wait()

    for cp in in_copies(step + 1):
        cp.wait()

    def mixer_a_block(j, z):
        zb, zc, zv, hzg = z
        u = zc * zv
        ubuf[j, SC_HIST:SC_HIST + ROWS, :] = u
        cw = sccw_ref[:, j * CB:(j + 1) * CB]
        conv = sccb_ref[:, j * CB:(j + 1) * CB] + ubuf[j, 0:ROWS, :] * cw[0:1, :]
        conv = conv + ubuf[j, BATCH:BATCH + ROWS, :] * cw[1:2, :]
        conv = conv + u * cw[2:3, :]
        ubuf[j, 0:SC_HIST, :] = u[ROWS - SC_HIST:ROWS, :]
        s_scr[j] = (zb * conv * _half_silu_from_half(hzg)).astype(BF16)

    def gate_dot(c):
        start = _gate_win_start(c)
        slabs = vbf if start % LANES == 0 else vbs
        s0 = start // LANES
        lhs = jnp.concatenate([slabs[s0 + i] for i in range(GATE_WIN // LANES)], axis=1)
        return _dot(lhs, wg_ref[c])

    def gate_dots(h_cur, k):
        pieces = [gate_dot(c) for c in range(k * CB // GB, (k + 1) * CB // GB)]
        hg_a = jnp.concatenate([p[:, 0:GB] for p in pieces], axis=1)
        hg_x = jnp.concatenate([p[:, GB:2 * GB] for p in pieces], axis=1)
        return (hg_a, hg_x), proj(h_cur, O_RGG + k * CB, CB)

    def mixer_b_block(k, hg, hzg):
        hg_a, hg_x = hg
        cols = slice(k * CB, (k + 1) * CB)
        t_r = jnp.tanh(hg_a + 0.5 * rgp[RGP_BA:RGP_BA + 1, cols])
        t_i = jnp.tanh(hg_x + 0.5 * rgp[RGP_BX:RGP_BX + 1, cols])
        half_k = (-0.5 * LRU_C) * jax.nn.softplus(-rgp[RGP_LAM:RGP_LAM + 1, cols])
        log_a = half_k + half_k * t_r
        a = jnp.exp(log_a)
        a_scr[...] = a
        one_minus_a2 = jnp.tanh(log_a) * (-1.0 - a * a)
        root = jnp.where(one_minus_a2 > 0.0, one_minus_a2 * lax.rsqrt(one_minus_a2), 0.0)
        hvk = hvf[k]
        b_scr[...] = root * (hvk + hvk * t_i)
        h = hstate[k]
        for t in range(TS):
            rows = slice(t * BATCH, (t + 1) * BATCH)
            h = a_scr[rows, :] * h + b_scr[rows, :]
            yrg_scr[rows, :] = h
        hstate[k] = h
        s2_scr[k] = (yrg_scr[...] * _half_silu_from_half(hzg)).astype(BF16)

    def process_tile(c):
        h_cur = h_tb.at[c % 2]
        h_next = h_tb.at[(c + 1) % 2]

        z = tuple(z0_scr[n] for n in range(4))
        for j in range(NA):
            z_next = mixer_a_dots(h_cur, j + 1) if j + 1 < NA else None
            if j == 0:
                if c + 1 < TPS:
                    norm_in(xg_cur, c + 1, h_next)
                else:
                    norm_in(xg_next, 0, h_next)
            mixer_a_block(j, z)
            z = z_next

        vz = proj(h_cur, O_RGV, RG_W)
        m_pieces = [proj(h_cur, O_M, M_PIECE)]
        s_all = jnp.concatenate([s_scr[j] for j in range(NA)], axis=1)
        ya_half = _dot(s_all, wproj_ref[WP_SC:WP_SC + D_MODEL, 0:D_MODEL])
        vbuf[RG_HIST:RG_HIST + ROWS, :] = vz
        rcw = 0.5 * rgp[RGP_CW:RGP_CW + RG_K, 0:RG_W]
        hv = 0.5 * rgp[RGP_CB:RGP_CB + 1, 0:RG_W] + vbuf[0:ROWS, :] * rcw[0:1, :]
        hv = hv + vbuf[BATCH:BATCH + ROWS, :] * rcw[1:2, :]
        hv = hv + vbuf[2 * BATCH:2 * BATCH + ROWS, :] * rcw[2:3, :]
        hv = hv + vz * rcw[3:4, :]
        vbuf[0:RG_HIST, :] = vz[ROWS - RG_HIST:ROWS, :]
        for cidx in range(NG):
            vbf[cidx] = hv[:, cidx * LANES:(cidx + 1) * LANES].astype(BF16)
        for cidx in range(NG - 1):
            lo = GATE_SHIFT + cidx * LANES
            vbs[cidx] = hv[:, lo:lo + LANES].astype(BF16)
        for k in range(NB):
            hvf[k] = hv[:, k * CB:(k + 1) * CB]

        hg, hzg = gate_dots(h_cur, 0)
        for k in range(NB):
            hg_next, hzg_next = gate_dots(h_cur, k + 1) if k + 1 < NB else (None, None)
            if 1 <= k < NB - 1:
                m_pieces.append(proj(h_cur, O_M + k * M_PIECE, M_PIECE))
            mixer_b_block(k, hg, hzg)
            hg, hzg = hg_next, hzg_next
        m = jnp.concatenate(m_pieces, axis=1)

        s2_all = jnp.concatenate([s2_scr[k] for k in range(NB)], axis=1)
        yb_half = _dot(s2_all, wproj_ref[WP_RG:WP_RG + RG_W, 0:D_MODEL])

        t_a = jnp.tanh(m[:, 0:D_MODEL] + 0.5 * bm_ref[0:1, :])
        t_b = jnp.tanh(m[:, D_MODEL:2 * D_MODEL] + 0.5 * bm_ref[1:2, :])
        merged = ((ya_half + ya_half * t_a) + (yb_half + yb_half * t_b)).astype(BF16)
        y3 = _dot(merged, wproj_ref[WP_OUT:WP_OUT + D_MODEL, 0:D_MODEL]).reshape(TS, BATCH, D_MODEL)
        block0_dots_to_scratch(h_next)
        xo = xbuf[xg_cur, c * TS:(c + 1) * TS] + mod_scr[:, 2 * D_MODEL:3 * D_MODEL] * y3
        ms = jnp.mean(xo * xo, axis=-1, keepdims=True)
        obuf[og_cur, c * TS:(c + 1) * TS] = xo * lax.rsqrt(ms + EPS) * gfinal_ref[...]

    for c in range(TPS):
        process_tile(c)

    for cp in out_copies(step):
        cp.start()
    for cp in in_copies(step + 2):
        cp.start()

    @pl.when(step == NSTEPS - 1)
    def _():
        for cp in in_copies(step + 2):
            cp.wait()
        for cp in out_copies(step - 1):
            cp.wait()
        for cp in out_copies(step):
            cp.wait()


def _const_spec(shape):
    nd = len(shape)
    return pl.BlockSpec(shape, lambda i, _nd=nd: (0,) * _nd, pipeline_mode=pl.Buffered(1))


def _gate_layout_constants():
    starts = np.array([_gate_win_start(c) for c in range(NG)])
    row_ch = starts[:, None] + np.arange(GATE_WIN)[None, :]
    col_ch = np.arange(NG)[:, None] * GB + np.arange(GB)[None, :]
    same_head = (row_ch // RG_HEAD_DIM)[:, :, None] == (col_ch // RG_HEAD_DIM)[:, None, :]
    spread = np.arange(LANES)[None, :, None] == (col_ch % RG_HEAD_DIM)[:, None, :]
    return jnp.asarray(spread, BF16), jnp.asarray(same_head, BF16)


def kernel(x, c, w_ada, b_ada, g_norm, w_in, sc_conv_w, sc_conv_b, sc_w_out, rg_conv_w, rg_conv_b,
           rg_w_a, rg_b_a, rg_w_x, rg_b_x, rg_lambda, rg_w_out, b_merge, w_out, g_final):
    assert x.shape == (BATCH, SEQ, D_MODEL) and w_ada.shape[0] == 1
    l = 0
    spread, same_head = _gate_layout_constants()

    vmem_operands = (
        c, b_ada[l].reshape(1, 3 * D_MODEL), g_norm[l].reshape(1, D_MODEL), g_final.reshape(1, D_MODEL),
        sc_conv_w[l], sc_conv_b[l].reshape(1, D_MODEL), b_merge[l],
        rg_w_a[l].reshape(RG_W, RG_HEAD_DIM), rg_w_x[l].reshape(RG_W, RG_HEAD_DIM), spread, same_head)
    hbm_weights = (rg_conv_w[l], rg_conv_b[l].reshape(1, RG_W), rg_b_a[l].reshape(1, RG_W),
                   rg_b_x[l].reshape(1, RG_W), rg_lambda[l].reshape(1, RG_W),
                   w_ada[l], w_in[l], sc_w_out[l], rg_w_out[l], w_out[l])
    operands = (x,) + vmem_operands + hbm_weights
    any_spec = pl.BlockSpec(memory_space=pl.ANY)
    in_specs = [any_spec] + [_const_spec(op.shape) for op in vmem_operands] + [any_spec] * len(hbm_weights)
    scratch = [
        pltpu.VMEM((8, 2 * D_MODEL), F32),
        pltpu.SemaphoreType.DMA((1,)),
        pltpu.VMEM((BATCH, 3 * D_MODEL), F32),
        pltpu.VMEM((NG, GATE_WIN, 2 * GB), BF16),
        pltpu.VMEM((P_IN // CB, D_MODEL, CB), BF16),
        pltpu.VMEM((WP_ROWS, D_MODEL + PITCH_PAD), BF16),
        pltpu.VMEM((WIN_SLOTS, WIN_CHUNK, P_IN), F32),
        pltpu.VMEM((WP_SLOTS, WP_CHUNK, D_MODEL), F32),
        pltpu.SemaphoreType.DMA((WIN_SLOTS,)),
        pltpu.SemaphoreType.DMA((WP_SLOTS,)),
        pltpu.VMEM((XBUFS, GROUP_T, BATCH, D_MODEL), F32),
        pltpu.VMEM((OBUFS, GROUP_T, BATCH, D_MODEL), F32),
        pltpu.SemaphoreType.DMA((XBUFS,)),
        pltpu.SemaphoreType.DMA((OBUFS,)),
        pltpu.VMEM((2, D_MODEL // CB, ROWS, CB), BF16),
        pltpu.VMEM((4, ROWS, CB), F32),
        pltpu.VMEM((NA, SC_HIST + ROWS, CB), F32),
        pltpu.VMEM((NA, ROWS, CB), BF16),
        pltpu.VMEM((RG_HIST + ROWS, RG_W), F32),
        pltpu.VMEM((NB, ROWS, CB), F32),
        pltpu.VMEM((NG, ROWS, LANES), BF16),
        pltpu.VMEM((NG - 1, ROWS, LANES), BF16),
        pltpu.VMEM((ROWS, CB), F32),
        pltpu.VMEM((ROWS, CB), F32),
        pltpu.VMEM((ROWS, CB), F32),
        pltpu.VMEM((NB, ROWS, CB), BF16),
        pltpu.VMEM((NB, BATCH, CB), F32),
    ]
    return pl.pallas_call(
        _block_kernel,
        grid=(NSTEPS,),
        in_specs=in_specs,
        out_specs=any_spec,
        out_shape=jax.ShapeDtypeStruct((BATCH, SEQ, D_MODEL), F32),
        scratch_shapes=scratch,
        compiler_params=pltpu.CompilerParams(
            dimension_semantics=("arbitrary",), vmem_limit_bytes=VMEM_LIMIT_BYTES),
        name="hybrid_block",
    )(*operands)
```
